```python
import math
import jax, jax.numpy as jnp
from jax import lax
import numpy as np

D_MODEL = 1024
BATCH = 8
SEQ = 4096
DEPTH = 2

PLE_DIM = 256
RWKV_HEADS = 8
RWKV_HEAD_DIM = 64
RWKV_WIDTH = RWKV_HEADS * RWKV_HEAD_DIM
DECAY_LORA = 64
AAA_LORA = 64
MV_LORA = 32
GATE_LORA = 128
LNX_EPS = 64e-5
SWA_Q_HEADS = 8
SWA_KV_HEADS = 2
SWA_HEAD_DIM = 64
SWA_GROUP = SWA_Q_HEADS // SWA_KV_HEADS
SWA_WIDTH = SWA_Q_HEADS * SWA_HEAD_DIM
SWA_KV_WIDTH = SWA_KV_HEADS * SWA_HEAD_DIM
WINDOW = 128
BLOCK = 128
ATTN_SCALE = 1.0 / math.sqrt(SWA_HEAD_DIM)
REL_BUCKETS = 32
REL_MAX_DIST = 128
MIX_WIDTH = RWKV_WIDTH + SWA_WIDTH
D_FF = 4 * D_MODEL
NORM_EPS = 1e-6
RWKV_COLS = 3 * RWKV_WIDTH + DECAY_LORA + AAA_LORA + GATE_LORA
SWA_COLS = SWA_WIDTH + 2 * SWA_KV_WIDTH
IN_COLS = RWKV_COLS + SWA_COLS

kernel_name = "hymba_rwkv7_swa_sink_hybrid"


def rms_norm(x, g):
    xf = x.astype(jnp.float32)
    y = xf * lax.rsqrt(jnp.mean(xf * xf, axis=-1, keepdims=True) + NORM_EPS)
    return (y * g.astype(jnp.float32)).astype(x.dtype)


def token_shift(z, mu):
    prev = jnp.concatenate([jnp.zeros_like(z[:, :1]), z[:, :-1]], axis=1)
    return z + (prev - z) * mu


def t5_causal_bucket(dist):
    max_exact = REL_BUCKETS // 2
    n = jnp.maximum(dist, 0)
    nf = jnp.maximum(n, 1).astype(jnp.float32)
    large = max_exact + (jnp.log(nf / max_exact) / math.log(REL_MAX_DIST / max_exact)
                         * (REL_BUCKETS - max_exact)).astype(jnp.int32)
    large = jnp.minimum(large, REL_BUCKETS - 1)
    return jnp.where(n < max_exact, n, large)


def rwkv7_scan(r, decay, k, v, kk, a):
    def step(state, inp):
        r_t, w_t, k_t, v_t, kk_t, a_t = inp
        sa = jnp.einsum('bhvk,bhk->bhv', state, -kk_t)
        state = (state * w_t[:, :, None, :]
                 + sa[..., None] * (kk_t * a_t)[:, :, None, :]
                 + v_t[..., None] * k_t[:, :, None, :])
        y = jnp.einsum('bhvk,bhk->bhv', state, r_t)
        return state, y
    B, S, H, N = r.shape
    xs = tuple(jnp.swapaxes(t, 0, 1) for t in (r, decay, k, v, kk, a))
    s0 = jnp.zeros((B, H, N, N), jnp.float32)
    _, y = lax.scan(step, s0, xs)
    return jnp.swapaxes(y, 0, 1)


def swa_sink_attention(q, k, v, sinks, rel_bias):
    B, S = q.shape[0], q.shape[1]
    nb = S // BLOCK
    f32 = jnp.float32
    qb = q.astype(f32).reshape(B, nb, BLOCK, SWA_KV_HEADS, SWA_GROUP, SWA_HEAD_DIM)
    kb = k.astype(f32).reshape(B, nb, BLOCK, SWA_KV_HEADS, SWA_HEAD_DIM)
    vb = v.astype(f32).reshape(B, nb, BLOCK, SWA_KV_HEADS, SWA_HEAD_DIM)

    def with_prev(t):
        prev = jnp.concatenate([jnp.zeros_like(t[:, :1]), t[:, :-1]], axis=1)
        return jnp.concatenate([prev, t], axis=2)

    kw, vw = with_prev(kb), with_prev(vb)
    qi = jnp.arange(BLOCK)[:, None] + BLOCK
    ki = jnp.arange(2 * BLOCK)[None, :]
    dist = qi - ki
    blk = jnp.arange(nb)[:, None, None]
    valid = (dist >= 0) & (dist < WINDOW) & (blk * BLOCK + ki[None] - BLOCK >= 0)
    bias = rel_bias.astype(f32)[t5_causal_bucket(dist)]
    bias = jnp.transpose(bias, (2, 0, 1)).reshape(SWA_KV_HEADS, SWA_GROUP, BLOCK, 2 * BLOCK)
    logits = jnp.einsum('bnqhgd,bnkhd->bnhgqk', qb, kw) * ATTN_SCALE + bias
    logits = jnp.where(valid[None, :, None, None], logits, -jnp.inf)
    sink = sinks.astype(f32).reshape(SWA_KV_HEADS, SWA_GROUP)[None, None, :, :, None, None]
    m = jnp.maximum(jnp.max(logits, axis=-1, keepdims=True), sink)
    e = jnp.exp(logits - m)
    probs = e / (jnp.sum(e, axis=-1, keepdims=True) + jnp.exp(sink - m))
    out = jnp.einsum('bnhgqk,bnkhd->bnqhgd', probs, vw)
    return out.reshape(B, S, SWA_WIDTH).astype(q.dtype)


def setup_inputs(seed: int = 0) -> dict:
    key = jax.random.key(seed)
    ks = iter(jax.random.split(key, 40))
    f32 = jnp.float32

    def nrm(shape, scale):
        return jax.random.normal(next(ks), shape, f32) * scale

    def gain(shape):
        return 1.0 + nrm(shape, 0.05)

    L, Lm1 = DEPTH, DEPTH - 1
    return {
        "x": nrm((BATCH, SEQ, D_MODEL), 1.0),
        "p": nrm((DEPTH, BATCH, SEQ, PLE_DIM), 1.0),
        "norm_mix_g": gain((L, D_MODEL)),
        "w_in": nrm((L, D_MODEL, IN_COLS), D_MODEL ** -0.5),
        "mu_shift": jax.random.uniform(next(ks), (L, RWKV_COLS), f32),
        "w0": jax.random.uniform(next(ks), (L, RWKV_WIDTH), f32, -5.0, 0.0),
        "w_up": nrm((L, DECAY_LORA, RWKV_WIDTH), 0.1),
        "a0": nrm((L, RWKV_WIDTH), 0.1),
        "a_up": nrm((L, AAA_LORA, RWKV_WIDTH), 0.5 * AAA_LORA ** -0.5),
        "g_up": nrm((L, GATE_LORA, RWKV_WIDTH), GATE_LORA ** -0.5),
        "vres_down": nrm((Lm1, D_MODEL, MV_LORA), D_MODEL ** -0.5),
        "mu_vres": jax.random.uniform(next(ks), (Lm1, MV_LORA), f32),
        "v0": 1.0 + nrm((Lm1, RWKV_WIDTH), 0.1),
        "vres_up": nrm((Lm1, MV_LORA, RWKV_WIDTH), 0.5 * MV_LORA ** -0.5),
        "k_k": 0.85 + nrm((L, RWKV_WIDTH), 0.05),
        "k_a": 1.0 + nrm((L, RWKV_WIDTH), 0.05),
        "r_k": nrm((L, RWKV_HEADS, RWKV_HEAD_DIM), 0.1),
        "lnx_g": gain((L, RWKV_WIDTH)),
        "lnx_b": nrm((L, RWKV_WIDTH), 0.01),
        "sinks": nrm((L, SWA_Q_HEADS), 1.0),
        "rel_bias": nrm((REL_BUCKETS, SWA_Q_HEADS), 0.5),
        "w_out": nrm((L, MIX_WIDTH, D_MODEL), MIX_WIDTH ** -0.5),
        "norm_mlp_g": gain((L, D_MODEL)),
        "w_ff_up": nrm((L, D_MODEL, D_FF), D_MODEL ** -0.5),
        "w_ff_down": nrm((L, D_FF, D_MODEL), D_FF ** -0.5),
        "ple_proj": nrm((L, PLE_DIM, D_MODEL), PLE_DIM ** -0.5),
        "ple_norm_g": gain((L, D_MODEL)),
        "ple_gate": nrm((L, D_MODEL, D_MODEL), D_MODEL ** -0.5),
        "final_norm_g": gain((D_MODEL,)),
    }


def reference(x, p, norm_mix_g, w_in, mu_shift, w0, w_up, a0, a_up, g_up,
              vres_down, mu_vres, v0, vres_up, k_k, k_a, r_k, lnx_g, lnx_b,
              sinks, rel_bias, w_out, norm_mlp_g, w_ff_up, w_ff_down,
              ple_proj, ple_norm_g, ple_gate, final_norm_g):
    f32 = jnp.float32
    B, S = x.shape[0], x.shape[1]
    H, N = RWKV_HEADS, RWKV_HEAD_DIM
    v_first = None
    for i in range(DEPTH):
        h = rms_norm(x, norm_mix_g[i])
        w_cols = w_in[i]
        if i > 0:
            w_cols = jnp.concatenate([w_cols, vres_down[i - 1]], axis=1)
        z = h @ w_cols

        zs = token_shift(z[..., :RWKV_COLS].astype(f32), mu_shift[i].astype(f32))
        o = np.cumsum([RWKV_WIDTH, RWKV_WIDTH, RWKV_WIDTH, DECAY_LORA, AAA_LORA])
        r, k, v = zs[..., :o[0]], zs[..., o[0]:o[1]], zs[..., o[1]:o[2]]
        xw, xa, xg = zs[..., o[2]:o[3]], zs[..., o[3]:o[4]], zs[..., o[4]:]
        w = -jax.nn.softplus(-(w0[i].astype(f32) + jnp.tanh(xw) @ w_up[i].astype(f32))) - 0.5
        decay = jnp.exp(-jnp.exp(w))
        if i == 0:
            v_first = v
        else:
            zv = token_shift(z[..., IN_COLS:].astype(f32), mu_vres[i - 1].astype(f32))
            v = v + (v_first - v) * jax.nn.sigmoid(v0[i - 1].astype(f32) + zv @ vres_up[i - 1].astype(f32))
        a = jax.nn.sigmoid(a0[i].astype(f32) + xa @ a_up[i].astype(f32))
        g = jax.nn.sigmoid(xg) @ g_up[i].astype(f32)
        kk = (k * k_k[i].astype(f32)).reshape(B, S, H, N)
        kk = kk / jnp.maximum(jnp.linalg.norm(kk, axis=-1, keepdims=True), 1e-12)
        k = k * (1.0 + (a - 1.0) * k_a[i].astype(f32))
        rh, kh, vh = r.reshape(B, S, H, N), k.reshape(B, S, H, N), v.reshape(B, S, H, N)
        y = rwkv7_scan(rh, decay.reshape(B, S, H, N), kh, vh, kk, a.reshape(B, S, H, N))
        ym = jnp.mean(y, axis=-1, keepdims=True)
        yv = jnp.mean(jnp.square(y - ym), axis=-1, keepdims=True)
        y = ((y - ym) * lax.rsqrt(yv + LNX_EPS)).reshape(B, S, RWKV_WIDTH)
        y = (y * lnx_g[i].astype(f32) + lnx_b[i].astype(f32)).reshape(B, S, H, N)
        y = y + jnp.sum(rh * kh * r_k[i].astype(f32), axis=-1, keepdims=True) * vh
        rwkv_out = (y.reshape(B, S, RWKV_WIDTH) * g).astype(x.dtype)

        zq = z[..., RWKV_COLS:RWKV_COLS + SWA_WIDTH].reshape(B, S, SWA_Q_HEADS, SWA_HEAD_DIM)
        zk = z[..., RWKV_COLS + SWA_WIDTH:RWKV_COLS + SWA_WIDTH + SWA_KV_WIDTH].reshape(B, S, SWA_KV_HEADS, SWA_HEAD_DIM)
        zvv = z[..., RWKV_COLS + SWA_WIDTH + SWA_KV_WIDTH:IN_COLS].reshape(B, S, SWA_KV_HEADS, SWA_HEAD_DIM)
        swa_out = swa_sink_attention(zq, zk, zvv, sinks[i], rel_bias)

        x = x + jnp.concatenate([rwkv_out, swa_out], axis=-1) @ w_out[i]

        u = rms_norm(x, norm_mlp_g[i])
        x = x + jnp.square(jax.nn.relu(u @ w_ff_up[i])) @ w_ff_down[i]

        e = rms_norm(p[i] @ ple_proj[i], ple_norm_g[i])
        x = x + e * jax.nn.sigmoid(x @ ple_gate[i])
    return rms_norm(x, final_norm_g)
```

```python
import functools
import math

import jax
import jax.numpy as jnp
import numpy as np
from jax import lax
from jax.experimental import pallas as pl
from jax.experimental.pallas import tpu as pltpu

F32 = jnp.float32
BF16 = jnp.bfloat16

D_MODEL = 1024
PLE_DIM = 256
HEADS = 8
HEAD_DIM = 64
RWKV_WIDTH = HEADS * HEAD_DIM
DECAY_LORA = 64
AAA_LORA = 64
MV_LORA = 32
GATE_LORA = 128
LNX_EPS = 64e-5
SWA_Q_HEADS = 8
SWA_KV_HEADS = 2
SWA_GROUP = SWA_Q_HEADS // SWA_KV_HEADS
SWA_WIDTH = SWA_Q_HEADS * HEAD_DIM
SWA_KV_WIDTH = SWA_KV_HEADS * HEAD_DIM
WINDOW = 128
BLOCK = 128
ATTN_SCALE = 1.0 / math.sqrt(HEAD_DIM)
REL_BUCKETS = 32
REL_MAX_DIST = 128
D_FF = 4 * D_MODEL
NORM_EPS = 1e-6
RWKV_COLS = 3 * RWKV_WIDTH + DECAY_LORA + AAA_LORA + GATE_LORA
SWA_COLS = SWA_WIDTH + 2 * SWA_KV_WIDTH
IN_COLS = RWKV_COLS + SWA_COLS

LANES = 128
VRES_PAD = LANES
CHUNK = 64
MASK_VALUE = -1e30

TM_PROJ = 512
TQ_SWA = 512
FF_CHUNK = 1024
VMEM_LIMIT = 56 * 1024 * 1024


def _rms(x, g):
    ms = jnp.mean(x * x, axis=-1, keepdims=True)
    return (x * lax.rsqrt(ms + NORM_EPS)) * g


def _bdot(a, b):
    return jnp.dot(a.astype(BF16), b.astype(BF16), preferred_element_type=F32)


def _split3(x):
    h1 = x.astype(BF16)
    r1 = x - h1.astype(F32)
    h2 = r1.astype(BF16)
    h3 = (r1 - h2.astype(F32)).astype(BF16)
    return h1, h2, h3


def _dot01_left(m01, x):
    h1, h2, h3 = _split3(x)
    d = lambda h: jnp.dot(m01, h, preferred_element_type=F32)
    return d(h1) + d(h2) + d(h3)


def _dot01_right(x, m01):
    h1, h2, h3 = _split3(x)
    d = lambda h: jnp.dot(h, m01, preferred_element_type=F32)
    return d(h1) + d(h2) + d(h3)


def _inproj_kernel(x_ref, g_ref, w_ref, *out_refs):
    h = _rms(x_ref[...], g_ref[...]).astype(BF16)
    off = 0
    for o_ref in out_refs:
        n = o_ref.shape[-1]
        o_ref[...] = jnp.dot(h, w_ref[:, off:off + n], preferred_element_type=F32)
        off += n


def _inproj(x2, g, w_bf16, widths):
    T = x2.shape[0]
    nc = w_bf16.shape[1]
    assert sum(widths) == nc
    return pl.pallas_call(
        _inproj_kernel,
        grid=(T // TM_PROJ,),
        in_specs=[
            pl.BlockSpec((TM_PROJ, D_MODEL), lambda i: (i, 0)),
            pl.BlockSpec((1, D_MODEL), lambda i: (0, 0)),
            pl.BlockSpec((D_MODEL, nc), lambda i: (0, 0)),
        ],
        out_specs=[pl.BlockSpec((TM_PROJ, n), lambda i: (i, 0)) for n in widths],
        out_shape=[jax.ShapeDtypeStruct((T, n), F32) for n in widths],
        compiler_params=pltpu.CompilerParams(
            dimension_semantics=("arbitrary",), vmem_limit_bytes=VMEM_LIMIT),
        name="inproj",
    )(x2, g, w_bf16)


def _sigmoid(x):
    return 1.0 / (1.0 + jnp.exp(-x))


def _softplus(x):
    return jnp.maximum(x, 0.0) + jnp.log1p(jnp.exp(-jnp.abs(x)))


def _shift(z, prev_row, mu):
    rolled = pltpu.roll(z, 1, 0)
    row = lax.broadcasted_iota(jnp.int32, z.shape, 0)
    prev = jnp.where(row == 0, prev_row, rolled)
    return z + (prev - z) * mu


def _smm(a, b):
    return jnp.dot(a, b, preferred_element_type=F32, precision=lax.Precision.HIGHEST)


def _smm_nt(a, b):
    return lax.dot_general(a, b, (((1,), (1,)), ((), ())), preferred_element_type=F32,
                           precision=lax.Precision.HIGHEST)


def _smm_tn(a, b):
    return lax.dot_general(a, b, (((0,), (0,)), ((), ())), preferred_element_type=F32,
                           precision=lax.Precision.HIGHEST)


def _rwkv_kernel(has_vres, n_chunks, *refs):
    if has_vres:
        (zr_ref, zv_ref, vf_ref, mu_ref, w0_ref, wup_ref, a0_ref, aup_ref, gup_ref, kk_ref, ka_ref,
         rk_ref, lg_ref, lb_ref, muv_ref, v0_ref, vup_ref, seg_ref, tril_ref, blk_ref,
         out_ref, h_ref, prev_ref, prevv_ref) = refs
    else:
        (zr_ref, mu_ref, w0_ref, wup_ref, a0_ref, aup_ref, gup_ref, kk_ref, ka_ref,
         rk_ref, lg_ref, lb_ref, seg_ref, tril_ref, blk_ref,
         out_ref, vf_out_ref, h_ref, prev_ref) = refs

    c_idx = pl.program_id(1)

    @pl.when(c_idx == 0)
    def _():
        h_ref[...] = jnp.zeros_like(h_ref)
        prev_ref[...] = jnp.zeros_like(prev_ref)
        if has_vres:
            prevv_ref[...] = jnp.zeros_like(prevv_ref)

    W = RWKV_WIDTH
    z = zr_ref[...]
    tt = z.shape[0]
    zs = _shift(z, prev_ref[...], mu_ref[...])
    prev_ref[...] = z[tt - 1:tt, :]
    r = zs[:, 0:W]
    k = zs[:, W:2 * W]
    v = zs[:, 2 * W:3 * W]
    o1 = 3 * W
    xw = zs[:, o1:o1 + DECAY_LORA]
    xa = zs[:, o1 + DECAY_LORA:o1 + DECAY_LORA + AAA_LORA]
    xg = zs[:, o1 + DECAY_LORA + AAA_LORA:]

    wraw = -_softplus(-(w0_ref[...] + _bdot(jnp.tanh(xw), wup_ref[...]))) - 0.5
    ld = -jnp.exp(wraw)
    a = _sigmoid(a0_ref[...] + _bdot(xa, aup_ref[...]))
    g = _bdot(_sigmoid(xg), gup_ref[...])
    if has_vres:
        zv = zv_ref[...]
        zvs = _shift(zv, prevv_ref[...], muv_ref[...])
        prevv_ref[...] = zv[tt - 1:tt, :]
        v = v + (vf_ref[...] - v) * _sigmoid(v0_ref[...] + _bdot(zvs, vup_ref[...]))
    else:
        vf_out_ref[...] = v

    seg = seg_ref[...]
    kk = k * kk_ref[...]
    nrm = jnp.sqrt(_dot01_right(kk * kk, seg))
    kk = kk / jnp.maximum(nrm, 1e-12)
    kadj = k * (1.0 + (a - 1.0) * ka_ref[...])
    b = kk * a

    c = _dot01_left(tril_ref[...], ld)
    cl = _dot01_left(blk_ref[...], ld)
    qt = kk * jnp.exp(c - ld)
    rt = r * jnp.exp(c)
    en = jnp.exp(-c)
    kt = kadj * en
    bt = b * en
    el = jnp.exp(cl - c)
    kh = kadj * el
    bh = b * el
    pl_all = jnp.exp(cl)

    L = CHUNK
    ri = lax.broadcasted_iota(jnp.int32, (L, L), 0)
    ci = lax.broadcasted_iota(jnp.int32, (L, L), 1)
    strict = ri > ci
    incl = ri >= ci
    eye = (ri == ci).astype(F32)

    y_rows = []
    for ch in range(n_chunks):
        rs = slice(ch * L, (ch + 1) * L)
        y_heads = []
        for hd in range(HEADS):
            sl = slice(hd * HEAD_DIM, (hd + 1) * HEAD_DIM)
            qh, rh, kth, bth = qt[rs, sl], rt[rs, sl], kt[rs, sl], bt[rs, sl]
            vh = v[rs, sl]
            amat = _smm_nt(jnp.concatenate([qh, rh], axis=0), jnp.concatenate([kth, bth], axis=0))
            a_qk = jnp.where(strict, amat[:L, :L], 0.0)
            a_qb = jnp.where(strict, amat[:L, L:], 0.0)
            a_rk = jnp.where(incl, amat[L:, :L], 0.0)
            a_rb = jnp.where(incl, amat[L:, L:], 0.0)
            tinv = eye - a_qb
            pw = a_qb
            n_sq = int(math.log2(L)) - 1
            for _ in range(n_sq):
                pw = _smm(pw, pw)
                tinv = tinv + _smm(tinv, pw)
            av = _smm(jnp.concatenate([a_qk, a_rk], axis=0), vh)
            x = _smm(tinv, jnp.concatenate([qh, av[:L]], axis=1))
            ry = jnp.concatenate([rh, av[L:]], axis=1) - _smm(a_rb, x)
            bx = _smm_tn(bh[rs, sl], x)
            kv = _smm_tn(kh[rs, sl], vh)
            mmat = eye * pl_all[ch * L:ch * L + 1, sl] - bx[:, :HEAD_DIM]
            nmat = kv - bx[:, HEAD_DIM:]
            h0 = h_ref[:, sl]
            y_heads.append(_smm(ry[:, :HEAD_DIM], h0) + ry[:, HEAD_DIM:])
            h_ref[:, sl] = _smm(mmat, h0) + nmat
        y_rows.append(jnp.concatenate(y_heads, axis=1))
    y = y_rows[0] if n_chunks == 1 else jnp.concatenate(y_rows, axis=0)

    inv_n = 1.0 / HEAD_DIM
    ym = _dot01_right(y, seg) * inv_n
    yc = y - ym
    yv = _dot01_right(yc * yc, seg) * inv_n
    y = yc * lax.rsqrt(yv + LNX_EPS)
    y = y * lg_ref[...] + lb_ref[...]
    y = y + _dot01_right(r * kadj * rk_ref[...], seg) * v
    out_ref[...] = y * g


def _rwkv(has_vres, zr, zv, v_first, prm, consts, n_chunks):
    B, S, _ = zr.shape
    tt = n_chunks * CHUNK
    W = RWKV_WIDTH
    tile = lambda n: pl.BlockSpec((None, tt, n), lambda b, c: (b, c, 0))
    full = lambda a: pl.BlockSpec(a.shape, lambda b, c: (0,) * a.ndim)
    names = ["mu", "w0", "w_up", "a0", "a_up", "g_up", "k_k", "k_a", "r_k", "lnx_g", "lnx_b"]
    if has_vres:
        names += ["mu_vres", "v0", "vres_up"]
    params = [prm[n] for n in names] + list(consts)
    if has_vres:
        acts = [zr, zv, v_first]
        act_specs = [tile(RWKV_COLS), tile(VRES_PAD), tile(W)]
        out_shape = [jax.ShapeDtypeStruct((B, S, W), F32)]
        out_specs = [tile(W)]
        scratch = [pltpu.VMEM((HEAD_DIM, W), F32), pltpu.VMEM((1, RWKV_COLS), F32),
                   pltpu.VMEM((1, VRES_PAD), F32)]
    else:
        acts = [zr]
        act_specs = [tile(RWKV_COLS)]
        out_shape = [jax.ShapeDtypeStruct((B, S, W), F32)] * 2
        out_specs = [tile(W), tile(W)]
        scratch = [pltpu.VMEM((HEAD_DIM, W), F32), pltpu.VMEM((1, RWKV_COLS), F32)]
    return pl.pallas_call(
        functools.partial(_rwkv_kernel, has_vres, n_chunks),
        grid=(B, S // tt),
        in_specs=act_specs + [full(a) for a in params],
        out_specs=out_specs,
        out_shape=out_shape,
        scratch_shapes=scratch,
        compiler_params=pltpu.CompilerParams(
            dimension_semantics=("arbitrary", "arbitrary"), vmem_limit_bytes=VMEM_LIMIT),
        name="rwkv_vres" if has_vres else "rwkv",
    )(*acts, *params)


def _swa_kernel(n_blk, q_ref, kv_ref, kvp_ref, bias_ref, sink_ref, o_ref):
    c_idx = pl.program_id(1)
    col = lax.broadcasted_iota(jnp.int32, (SWA_GROUP * BLOCK, 2 * BLOCK), 1)
    for j in range(n_blk):
        rows = slice(j * BLOCK, (j + 1) * BLOCK)
        if j == 0:
            kv = jnp.concatenate([kvp_ref[...], kv_ref[0:BLOCK, :]], axis=0)
            has_prev = c_idx > 0
        else:
            kv = kv_ref[(j - 1) * BLOCK:(j + 1) * BLOCK, :]
            has_prev = None
        q = q_ref[rows, :] * ATTN_SCALE
        outs = [None] * SWA_Q_HEADS
        for gi in range(SWA_KV_HEADS):
            kg = kv[:, gi * HEAD_DIM:(gi + 1) * HEAD_DIM]
            vg = kv[:, SWA_KV_WIDTH + gi * HEAD_DIM:SWA_KV_WIDTH + (gi + 1) * HEAD_DIM]
            qs = jnp.concatenate(
                [q[:, (gi * SWA_GROUP + i) * HEAD_DIM:(gi * SWA_GROUP + i + 1) * HEAD_DIM]
                 for i in range(SWA_GROUP)], axis=0)
            gr = slice(gi * SWA_GROUP * BLOCK, (gi + 1) * SWA_GROUP * BLOCK)
            logits = lax.dot_general(qs.astype(BF16), kg.astype(BF16), (((1,), (1,)), ((), ())),
                                     preferred_element_type=F32) + bias_ref[gr, :]
            if has_prev is not None:
                logits = jnp.where((col >= BLOCK) | has_prev, logits, MASK_VALUE)
            sink = sink_ref[gr, :]
            m = jnp.maximum(jnp.max(logits, axis=-1, keepdims=True), sink)
            e = jnp.exp(logits - m)
            den = jnp.sum(e, axis=-1, keepdims=True) + jnp.exp(sink - m)
            og = _bdot(e, vg) / den
            for i in range(SWA_GROUP):
                outs[gi * SWA_GROUP + i] = og[i * BLOCK:(i + 1) * BLOCK, :]
        o_ref[rows, :] = jnp.concatenate(outs, axis=1)


def _swa(zq, zkv, bias_m, sink_col):
    B, S, _ = zq.shape
    n_blk = TQ_SWA // BLOCK
    return pl.pallas_call(
        functools.partial(_swa_kernel, n_blk),
        grid=(B, S // TQ_SWA),
        in_specs=[
            pl.BlockSpec((None, TQ_SWA, SWA_WIDTH), lambda b, c: (b, c, 0)),
            pl.BlockSpec((None, TQ_SWA, 2 * SWA_KV_WIDTH), lambda b, c: (b, c, 0)),
            pl.BlockSpec((None, BLOCK, 2 * SWA_KV_WIDTH),
                         lambda b, c: (b, jnp.maximum(c * n_blk - 1, 0), 0)),
            pl.BlockSpec(bias_m.shape, lambda b, c: (0, 0)),
            pl.BlockSpec(sink_col.shape, lambda b, c: (0, 0)),
        ],
        out_specs=pl.BlockSpec((None, TQ_SWA, SWA_WIDTH), lambda b, c: (b, c, 0)),
        out_shape=jax.ShapeDtypeStruct((B, S, SWA_WIDTH), F32),
        compiler_params=pltpu.CompilerParams(
            dimension_semantics=("arbitrary", "arbitrary"), vmem_limit_bytes=VMEM_LIMIT),
        name="swa",
    )(zq, zkv, zkv, bias_m, sink_col)


def _post_kernel(final, x_ref, ro_ref, so_ref, p_ref, wo_ref, gm_ref, up_ref, dn_ref, pp_ref, pg_ref,
                 gate_ref, fg_ref, o_ref):
    W = RWKV_WIDTH
    x = x_ref[...]
    x = x + _bdot(ro_ref[...], wo_ref[0:W, :]) + _bdot(so_ref[...], wo_ref[W:2 * W, :])
    u = _rms(x, gm_ref[...]).astype(BF16)
    acc = jnp.zeros_like(x)
    for c in range(D_FF // FF_CHUNK):
        cs = slice(c * FF_CHUNK, (c + 1) * FF_CHUNK)
        hid = jnp.dot(u, up_ref[:, cs], preferred_element_type=F32)
        hid = jnp.square(jnp.maximum(hid, 0.0))
        acc = acc + jnp.dot(hid.astype(BF16), dn_ref[cs, :], preferred_element_type=F32)
    x = x + acc
    e = _rms(_bdot(p_ref[...], pp_ref[...]), pg_ref[...])
    x = x + e * _sigmoid(_bdot(x, gate_ref[...]))
    if final:
        x = _rms(x, fg_ref[...])
    o_ref[...] = x


def _post(final, x2, ro, so, p2, wo, gm, up, dn, pp, pg, gate, fg):
    T = x2.shape[0]
    tile = lambda n: pl.BlockSpec((TM_PROJ, n), lambda i: (i, 0))
    res = lambda a: pl.BlockSpec(a.shape, lambda i: (0, 0), pipeline_mode=pl.Buffered(1))
    weights = [wo, gm, up, dn, pp, pg, gate, fg]
    return pl.pallas_call(
        functools.partial(_post_kernel, final),
        grid=(T // TM_PROJ,),
        in_specs=[tile(D_MODEL), tile(RWKV_WIDTH), tile(SWA_WIDTH), tile(PLE_DIM)] + [res(a) for a in weights],
        out_specs=tile(D_MODEL),
        out_shape=jax.ShapeDtypeStruct((T, D_MODEL), F32),
        compiler_params=pltpu.CompilerParams(
            dimension_semantics=("arbitrary",), vmem_limit_bytes=VMEM_LIMIT),
        name="post_final" if final else "post",
    )(x2, ro, so, p2, *weights)


def _t5_causal_bucket(dist):
    max_exact = REL_BUCKETS // 2
    n = jnp.maximum(dist, 0)
    nf = jnp.maximum(n, 1).astype(F32)
    large = max_exact + (jnp.log(nf / max_exact) / math.log(REL_MAX_DIST / max_exact)
                         * (REL_BUCKETS - max_exact)).astype(jnp.int32)
    large = jnp.minimum(large, REL_BUCKETS - 1)
    return jnp.where(n < max_exact, n, large)


def _scan_consts(tt):
    t = np.arange(tt)
    same = (t[:, None] // CHUNK) == (t[None, :] // CHUNK)
    tril = same & (t[:, None] >= t[None, :])
    hh = np.arange(RWKV_WIDTH) // HEAD_DIM
    seg = hh[:, None] == hh[None, :]
    return (jnp.asarray(seg, BF16), jnp.asarray(tril, BF16), jnp.asarray(same, BF16))


def kernel(x, p, norm_mix_g, w_in, mu_shift, w0, w_up, a0, a_up, g_up, vres_down, mu_vres, v0, vres_up,
           k_k, k_a, r_k, lnx_g, lnx_b, sinks, rel_bias, w_out, norm_mlp_g, w_ff_up, w_ff_down,
           ple_proj, ple_norm_g, ple_gate, final_norm_g):
    B, S, _ = x.shape
    depth = w_in.shape[0]
    T = B * S
    n_chunks = 1
    row = lambda a: a.reshape(1, -1).astype(F32)

    qi = jnp.arange(BLOCK)[:, None] + BLOCK
    ki = jnp.arange(2 * BLOCK)[None, :]
    dist = qi - ki
    valid = (dist >= 0) & (dist < WINDOW)
    bias = rel_bias.astype(F32)[_t5_causal_bucket(dist)]
    bias = jnp.where(valid[..., None], bias, MASK_VALUE)
    bias_m = jnp.transpose(bias, (2, 0, 1)).reshape(SWA_Q_HEADS * BLOCK, 2 * BLOCK)
    consts = _scan_consts(n_chunks * CHUNK)

    x2 = x.reshape(T, D_MODEL)
    v_first = None
    for i in range(depth):
        w_cols = w_in[i]
        widths = [RWKV_COLS, SWA_WIDTH, 2 * SWA_KV_WIDTH]
        if i > 0:
            pad = jnp.zeros((D_MODEL, VRES_PAD - MV_LORA), w_cols.dtype)
            w_cols = jnp.concatenate([w_cols, vres_down[i - 1], pad], axis=1)
            widths.append(VRES_PAD)
        zs = _inproj(x2, row(norm_mix_g[i]), w_cols.astype(BF16), widths)
        zr = zs[0].reshape(B, S, RWKV_COLS)
        zq = zs[1].reshape(B, S, SWA_WIDTH)
        zkv = zs[2].reshape(B, S, 2 * SWA_KV_WIDTH)

        prm = {
            "mu": row(mu_shift[i]), "w0": row(w0[i]), "w_up": w_up[i].astype(BF16), "a0": row(a0[i]),
            "a_up": a_up[i].astype(BF16), "g_up": g_up[i].astype(BF16), "k_k": row(k_k[i]),
            "k_a": row(k_a[i]), "r_k": row(r_k[i]), "lnx_g": row(lnx_g[i]), "lnx_b": row(lnx_b[i]),
        }
        if i == 0:
            rwkv_out, v_first = _rwkv(False, zr, None, None, prm, consts, n_chunks)
        else:
            prm["mu_vres"] = jnp.pad(row(mu_vres[i - 1]), ((0, 0), (0, VRES_PAD - MV_LORA)))
            prm["v0"] = row(v0[i - 1])
            prm["vres_up"] = jnp.pad(vres_up[i - 1], ((0, VRES_PAD - MV_LORA), (0, 0))).astype(BF16)
            zv = zs[3].reshape(B, S, VRES_PAD)
            (rwkv_out,) = _rwkv(True, zr, zv, v_first, prm, consts, n_chunks)

        sink_col = jnp.repeat(sinks[i].astype(F32), BLOCK).reshape(SWA_Q_HEADS * BLOCK, 1)
        swa_out = _swa(zq, zkv, bias_m, sink_col)

        x2 = _post(i == depth - 1, x2, rwkv_out.reshape(T, RWKV_WIDTH), swa_out.reshape(T, SWA_WIDTH),
                   p[i].reshape(T, PLE_DIM), w_out[i].astype(BF16), row(norm_mlp_g[i]),
                   w_ff_up[i].astype(BF16), w_ff_down[i].astype(BF16), ple_proj[i].astype(BF16),
                   row(ple_norm_g[i]), ple_gate[i].astype(BF16), row(final_norm_g))
    return x2.reshape(B, S, D_MODEL)
```

```python
import functools
import math

import jax
import jax.numpy as jnp
import numpy as np
from jax import lax
from jax.experimental import pallas as pl
from jax.experimental.pallas import tpu as pltpu

F32 = jnp.float32
BF16 = jnp.bfloat16

D_MODEL = 1024
PLE_DIM = 256
HEADS = 8
HEAD_DIM = 64
RWKV_WIDTH = HEADS * HEAD_DIM
DECAY_LORA = 64
AAA_LORA = 64
MV_LORA = 32
GATE_LORA = 128
LNX_EPS = 64e-5
SWA_Q_HEADS = 8
SWA_KV_HEADS = 2
SWA_GROUP = SWA_Q_HEADS // SWA_KV_HEADS
SWA_WIDTH = SWA_Q_HEADS * HEAD_DIM
SWA_KV_WIDTH = SWA_KV_HEADS * HEAD_DIM
WINDOW = 128
BLOCK = 128
ATTN_SCALE = 1.0 / math.sqrt(HEAD_DIM)
REL_BUCKETS = 32
REL_MAX_DIST = 128
D_FF = 4 * D_MODEL
NORM_EPS = 1e-6
RWKV_COLS = 3 * RWKV_WIDTH + DECAY_LORA + AAA_LORA + GATE_LORA
SWA_COLS = SWA_WIDTH + 2 * SWA_KV_WIDTH
IN_COLS = RWKV_COLS + SWA_COLS

LANES = 128
VRES_PAD = LANES
CHUNK = 64
SCAN_CHUNKS_PER_STEP = 2
MASK_VALUE = -1e30

TM_PROJ = 512
TQ_SWA = 512
FF_CHUNK = 1024
VMEM_LIMIT = 56 * 1024 * 1024


def _rms(x, g):
    ms = jnp.mean(x * x, axis=-1, keepdims=True)
    return (x * lax.rsqrt(ms + NORM_EPS)) * g


def _bdot(a, b):
    return jnp.dot(a.astype(BF16), b.astype(BF16), preferred_element_type=F32)


def _split3(x):
    h1 = x.astype(BF16)
    r1 = x - h1.astype(F32)
    h2 = r1.astype(BF16)
    h3 = (r1 - h2.astype(F32)).astype(BF16)
    return h1, h2, h3


def _dot01_left(m01, x):
    h1, h2, h3 = _split3(x)
    d = lambda h: jnp.dot(m01, h, preferred_element_type=F32)
    return d(h1) + d(h2) + d(h3)


def _dot01_right(x, m01):
    h1, h2, h3 = _split3(x)
    d = lambda h: jnp.dot(h, m01, preferred_element_type=F32)
    return d(h1) + d(h2) + d(h3)


def _inproj_kernel(x_ref, g_ref, w_ref, *out_refs):
    h = _rms(x_ref[...], g_ref[...]).astype(BF16)
    off = 0
    for o_ref in out_refs:
        n = o_ref.shape[-1]
        o_ref[...] = jnp.dot(h, w_ref[:, off:off + n], preferred_element_type=F32)
        off += n


def _inproj(x2, g, w_bf16, widths):
    T = x2.shape[0]
    nc = w_bf16.shape[1]
    assert sum(widths) == nc
    return pl.pallas_call(
        _inproj_kernel,
        grid=(T // TM_PROJ,),
        in_specs=[
            pl.BlockSpec((TM_PROJ, D_MODEL), lambda i: (i, 0)),
            pl.BlockSpec((1, D_MODEL), lambda i: (0, 0)),
            pl.BlockSpec((D_MODEL, nc), lambda i: (0, 0)),
        ],
        out_specs=[pl.BlockSpec((TM_PROJ, n), lambda i: (i, 0)) for n in widths],
        out_shape=[jax.ShapeDtypeStruct((T, n), F32) for n in widths],
        compiler_params=pltpu.CompilerParams(
            dimension_semantics=("arbitrary",), vmem_limit_bytes=VMEM_LIMIT),
        name="inproj",
    )(x2, g, w_bf16)


def _sigmoid(x):
    return 1.0 / (1.0 + jnp.exp(-x))


def _softplus(x):
    return jnp.maximum(x, 0.0) + jnp.log1p(jnp.exp(-jnp.abs(x)))


def _shift(z, prev_row, mu):
    rolled = pltpu.roll(z, 1, 0)
    row = lax.broadcasted_iota(jnp.int32, z.shape, 0)
    prev = jnp.where(row == 0, prev_row, rolled)
    return z + (prev - z) * mu


def _rwkv_kernel(has_vres, n_chunks, *refs):
    if has_vres:
        (zr_ref, zv_ref, vf_ref, mu_ref, w0_ref, wup_ref, a0_ref, aup_ref, gup_ref, kk_ref, ka_ref,
         rk_ref, lg_ref, lb_ref, muv_ref, v0_ref, vup_ref, seg_ref, tril_ref, blk_ref,
         out_ref, h_ref, prev_ref, prevv_ref) = refs
    else:
        (zr_ref, mu_ref, w0_ref, wup_ref, a0_ref, aup_ref, gup_ref, kk_ref, ka_ref,
         rk_ref, lg_ref, lb_ref, seg_ref, tril_ref, blk_ref,
         out_ref, vf_out_ref, h_ref, prev_ref) = refs

    c_idx = pl.program_id(1)

    @pl.when(c_idx == 0)
    def _():
        h_ref[...] = jnp.zeros_like(h_ref)
        prev_ref[...] = jnp.zeros_like(prev_ref)
        if has_vres:
            prevv_ref[...] = jnp.zeros_like(prevv_ref)

    W = RWKV_WIDTH
    z = zr_ref[...]
    tt = z.shape[0]
    zs = _shift(z, prev_ref[...], mu_ref[...])
    prev_ref[...] = z[tt - 1:tt, :]
    r = zs[:, 0:W]
    k = zs[:, W:2 * W]
    v = zs[:, 2 * W:3 * W]
    o1 = 3 * W
    xw = zs[:, o1:o1 + DECAY_LORA]
    xa = zs[:, o1 + DECAY_LORA:o1 + DECAY_LORA + AAA_LORA]
    xg = zs[:, o1 + DECAY_LORA + AAA_LORA:]

    wraw = -_softplus(-(w0_ref[...] + _bdot(jnp.tanh(xw), wup_ref[...]))) - 0.5
    ld = -jnp.exp(wraw)
    a = _sigmoid(a0_ref[...] + _bdot(xa, aup_ref[...]))
    g = _bdot(_sigmoid(xg), gup_ref[...])
    if has_vres:
        zv = zv_ref[...]
        zvs = _shift(zv, prevv_ref[...], muv_ref[...])
        prevv_ref[...] = zv[tt - 1:tt, :]
        v = v + (vf_ref[...] - v) * _sigmoid(v0_ref[...] + _bdot(zvs, vup_ref[...]))
    else:
        vf_out_ref[...] = v

    seg = seg_ref[...]
    kk = k * kk_ref[...]
    nrm = jnp.sqrt(_dot01_right(kk * kk, seg))
    kk = kk / jnp.maximum(nrm, 1e-12)
    kadj = k * (1.0 + (a - 1.0) * ka_ref[...])
    b = kk * a

    c = _dot01_left(tril_ref[...], ld)
    cl = _dot01_left(blk_ref[...], ld)
    qt = kk * jnp.exp(c - ld)
    rt = r * jnp.exp(c)
    en = jnp.exp(-c)
    kt = kadj * en
    bt = b * en
    el = jnp.exp(cl - c)
    kh = kadj * el
    bh = b * el
    pl_all = jnp.exp(cl)

    L = CHUNK
    ri = lax.broadcasted_iota(jnp.int32, (L, L), 0)
    ci = lax.broadcasted_iota(jnp.int32, (L, L), 1)
    strict = ri > ci
    incl = ri >= ci
    eye = (ri == ci).astype(F32)

    N = HEAD_DIM
    units = [(ch, hd) for ch in range(n_chunks) for hd in range(HEADS)]

    def per_unit(arr):
        arr = arr.astype(BF16)
        return [arr[ch * L:(ch + 1) * L, hd * N:(hd + 1) * N] for ch, hd in units]

    cat0 = lambda *xs: jnp.concatenate(xs, axis=0)
    cat1 = lambda *xs: jnp.concatenate(xs, axis=1)
    mm = lambda a_, b_: jnp.dot(a_, b_, preferred_element_type=F32)
    nt = lambda a_, b_: lax.dot_general(a_, b_, (((1,), (1,)), ((), ())), preferred_element_type=F32)
    tn = lambda a_, b_: lax.dot_general(a_, b_, (((0,), (0,)), ((), ())), preferred_element_type=F32)

    qh, rh, kth, bth, vh = per_unit(qt), per_unit(rt), per_unit(kt), per_unit(bt), per_unit(v)
    khh, bhh = per_unit(kh), per_unit(bh)
    amat = [nt(cat0(q_, r_), cat0(k_, b_)) for q_, r_, k_, b_ in zip(qh, rh, kth, bth)]
    a_qk = [jnp.where(strict, m_[:L, :L], 0.0).astype(BF16) for m_ in amat]
    a_qb = [jnp.where(strict, m_[:L, L:], 0.0) for m_ in amat]
    a_rk = [jnp.where(incl, m_[L:, :L], 0.0).astype(BF16) for m_ in amat]
    a_rb = [jnp.where(incl, m_[L:, L:], 0.0).astype(BF16) for m_ in amat]
    tinv = [eye - m_ for m_ in a_qb]
    pw = [m_.astype(BF16) for m_ in a_qb]
    for _ in range(int(math.log2(L)) - 1):
        pw = [mm(p_, p_).astype(BF16) for p_ in pw]
        tinv = [t_ + mm(t_.astype(BF16), p_) for t_, p_ in zip(tinv, pw)]
    av = [mm(cat0(ak_, ar_), v_) for ak_, ar_, v_ in zip(a_qk, a_rk, vh)]
    x = [mm(t_.astype(BF16), cat1(q_, av_[:L].astype(BF16))).astype(BF16)
         for t_, q_, av_ in zip(tinv, qh, av)]
    ry = [cat1(rt[ch * L:(ch + 1) * L, hd * N:(hd + 1) * N], av_[L:]) - mm(ab_, x_)
          for (ch, hd), av_, ab_, x_ in zip(units, av, a_rb, x)]
    bx = [tn(b_, x_) for b_, x_ in zip(bhh, x)]
    kv = [tn(k_, v_) for k_, v_ in zip(khh, vh)]
    mmat = [(eye * pl_all[ch * L:ch * L + 1, hd * N:(hd + 1) * N] - bx_[:, :N]).astype(BF16)
            for (ch, hd), bx_ in zip(units, bx)]
    nmat = [kv_ - bx_[:, N:] for kv_, bx_ in zip(kv, bx)]

    h_cur = [h_ref[:, hd * N:(hd + 1) * N] for hd in range(HEADS)]
    y_rows = []
    for ch in range(n_chunks):
        y_heads = []
        for hd in range(HEADS):
            u = ch * HEADS + hd
            hb = h_cur[hd].astype(BF16)
            y_heads.append(mm(ry[u][:, :N].astype(BF16), hb) + ry[u][:, N:])
            h_cur[hd] = mm(mmat[u], hb) + nmat[u]
        y_rows.append(cat1(*y_heads))
    h_ref[...] = cat1(*h_cur)
    y = y_rows[0] if n_chunks == 1 else cat0(*y_rows)

    inv_n = 1.0 / HEAD_DIM
    ym = _dot01_right(y, seg) * inv_n
    yc = y - ym
    yv = _dot01_right(yc * yc, seg) * inv_n
    y = yc * lax.rsqrt(yv + LNX_EPS)
    y = y * lg_ref[...] + lb_ref[...]
    y = y + _dot01_right(r * kadj * rk_ref[...], seg) * v
    out_ref[...] = y * g


def _rwkv(has_vres, zr, zv, v_first, prm, consts, n_chunks):
    B, S, _ = zr.shape
    tt = n_chunks * CHUNK
    W = RWKV_WIDTH
    tile = lambda n: pl.BlockSpec((None, tt, n), lambda b, c: (b, c, 0))
    full = lambda a: pl.BlockSpec(a.shape, lambda b, c: (0,) * a.ndim)
    names = ["mu", "w0", "w_up", "a0", "a_up", "g_up", "k_k", "k_a", "r_k", "lnx_g", "lnx_b"]
    if has_vres:
        names += ["mu_vres", "v0", "vres_up"]
    params = [prm[n] for n in names] + list(consts)
    if has_vres:
        acts = [zr, zv, v_first]
        act_specs = [tile(RWKV_COLS), tile(VRES_PAD), tile(W)]
        out_shape = [jax.ShapeDtypeStruct((B, S, W), F32)]
        out_specs = [tile(W)]
        scratch = [pltpu.VMEM((HEAD_DIM, W), F32), pltpu.VMEM((1, RWKV_COLS), F32),
                   pltpu.VMEM((1, VRES_PAD), F32)]
    else:
        acts = [zr]
        act_specs = [tile(RWKV_COLS)]
        out_shape = [jax.ShapeDtypeStruct((B, S, W), F32)] * 2
        out_specs = [tile(W), tile(W)]
        scratch = [pltpu.VMEM((HEAD_DIM, W), F32), pltpu.VMEM((1, RWKV_COLS), F32)]
    return pl.pallas_call(
        functools.partial(_rwkv_kernel, has_vres, n_chunks),
        grid=(B, S // tt),
        in_specs=act_specs + [full(a) for a in params],
        out_specs=out_specs,
        out_shape=out_shape,
        scratch_shapes=scratch,
        compiler_params=pltpu.CompilerParams(
            dimension_semantics=("arbitrary", "arbitrary"), vmem_limit_bytes=VMEM_LIMIT),
        name="rwkv_vres" if has_vres else "rwkv",
    )(*acts, *params)


def _swa_kernel(n_blk, q_ref, kv_ref, kvp_ref, bias_ref, sink_ref, o_ref):
    c_idx = pl.program_id(1)
    col = lax.broadcasted_iota(jnp.int32, (SWA_GROUP * BLOCK, 2 * BLOCK), 1)
    for j in range(n_blk):
        rows = slice(j * BLOCK, (j + 1) * BLOCK)
        if j == 0:
            kv = jnp.concatenate([kvp_ref[...], kv_ref[0:BLOCK, :]], axis=0)
            has_prev = c_idx > 0
        else:
            kv = kv_ref[(j - 1) * BLOCK:(j + 1) * BLOCK, :]
            has_prev = None
        q = q_ref[rows, :] * ATTN_SCALE
        outs = [None] * SWA_Q_HEADS
        for gi in range(SWA_KV_HEADS):
            kg = kv[:, gi * HEAD_DIM:(gi + 1) * HEAD_DIM]
            vg = kv[:, SWA_KV_WIDTH + gi * HEAD_DIM:SWA_KV_WIDTH + (gi + 1) * HEAD_DIM]
            qs = jnp.concatenate(
                [q[:, (gi * SWA_GROUP + i) * HEAD_DIM:(gi * SWA_GROUP + i + 1) * HEAD_DIM]
                 for i in range(SWA_GROUP)], axis=0)
            gr = slice(gi * SWA_GROUP * BLOCK, (gi + 1) * SWA_GROUP * BLOCK)
            logits = lax.dot_general(qs.astype(BF16), kg.astype(BF16), (((1,), (1,)), ((), ())),
                                     preferred_element_type=F32) + bias_ref[gr, :]
            if has_prev is not None:
                logits = jnp.where((col >= BLOCK) | has_prev, logits, MASK_VALUE)
            sink = sink_ref[gr, :]
            m = jnp.maximum(jnp.max(logits, axis=-1, keepdims=True), sink)
            e = jnp.exp(logits - m)
            den = jnp.sum(e, axis=-1, keepdims=True) + jnp.exp(sink - m)
            og = _bdot(e, vg) / den
            for i in range(SWA_GROUP):
                outs[gi * SWA_GROUP + i] = og[i * BLOCK:(i + 1) * BLOCK, :]
        o_ref[rows, :] = jnp.concatenate(outs, axis=1)


def _swa(zq, zkv, bias_m, sink_col):
    B, S, _ = zq.shape
    n_blk = TQ_SWA // BLOCK
    return pl.pallas_call(
        functools.partial(_swa_kernel, n_blk),
        grid=(B, S // TQ_SWA),
        in_specs=[
            pl.BlockSpec((None, TQ_SWA, SWA_WIDTH), lambda b, c: (b, c, 0)),
            pl.BlockSpec((None, TQ_SWA, 2 * SWA_KV_WIDTH), lambda b, c: (b, c, 0)),
            pl.BlockSpec((None, BLOCK, 2 * SWA_KV_WIDTH),
                         lambda b, c: (b, jnp.maximum(c * n_blk - 1, 0), 0)),
            pl.BlockSpec(bias_m.shape, lambda b, c: (0, 0)),
            pl.BlockSpec(sink_col.shape, lambda b, c: (0, 0)),
        ],
        out_specs=pl.BlockSpec((None, TQ_SWA, SWA_WIDTH), lambda b, c: (b, c, 0)),
        out_shape=jax.ShapeDtypeStruct((B, S, SWA_WIDTH), F32),
        compiler_params=pltpu.CompilerParams(
            dimension_semantics=("arbitrary", "arbitrary"), vmem_limit_bytes=VMEM_LIMIT),
        name="swa",
    )(zq, zkv, zkv, bias_m, sink_col)


def _post_kernel(final, x_ref, ro_ref, so_ref, p_ref, wo_ref, gm_ref, up_ref, dn_ref, pp_ref, pg_ref,
                 gate_ref, fg_ref, o_ref):
    W = RWKV_WIDTH
    x = x_ref[...]
    x = x + _bdot(ro_ref[...], wo_ref[0:W, :]) + _bdot(so_ref[...], wo_ref[W:2 * W, :])
    u = _rms(x, gm_ref[...]).astype(BF16)
    acc = jnp.zeros_like(x)
    for c in range(D_FF // FF_CHUNK):
        cs = slice(c * FF_CHUNK, (c + 1) * FF_CHUNK)
        hid = jnp.dot(u, up_ref[:, cs], preferred_element_type=F32)
        hid = jnp.square(jnp.maximum(hid, 0.0))
        acc = acc + jnp.dot(hid.astype(BF16), dn_ref[cs, :], preferred_element_type=F32)
    x = x + acc
    e = _rms(_bdot(p_ref[...], pp_ref[...]), pg_ref[...])
    x = x + e * _sigmoid(_bdot(x, gate_ref[...]))
    if final:
        x = _rms(x, fg_ref[...])
    o_ref[...] = x


def _post(final, x2, ro, so, p2, wo, gm, up, dn, pp, pg, gate, fg):
    T = x2.shape[0]
    tile = lambda n: pl.BlockSpec((TM_PROJ, n), lambda i: (i, 0))
    res = lambda a: pl.BlockSpec(a.shape, lambda i: (0, 0), pipeline_mode=pl.Buffered(1))
    weights = [wo, gm, up, dn, pp, pg, gate, fg]
    return pl.pallas_call(
        functools.partial(_post_kernel, final),
        grid=(T // TM_PROJ,),
        in_specs=[tile(D_MODEL), tile(RWKV_WIDTH), tile(SWA_WIDTH), tile(PLE_DIM)] + [res(a) for a in weights],
        out_specs=tile(D_MODEL),
        out_shape=jax.ShapeDtypeStruct((T, D_MODEL), F32),
        compiler_params=pltpu.CompilerParams(
            dimension_semantics=("arbitrary",), vmem_limit_bytes=VMEM_LIMIT),
        name="post_final" if final else "post",
    )(x2, ro, so, p2, *weights)


def _t5_causal_bucket(dist):
    max_exact = REL_BUCKETS // 2
    n = jnp.maximum(dist, 0)
    nf = jnp.maximum(n, 1).astype(F32)
    large = max_exact + (jnp.log(nf / max_exact) / math.log(REL_MAX_DIST / max_exact)
                         * (REL_BUCKETS - max_exact)).astype(jnp.int32)
    large = jnp.minimum(large, REL_BUCKETS - 1)
    return jnp.where(n < max_exact, n, large)


def _scan_consts(tt):
    t = np.arange(tt)
    same = (t[:, None] // CHUNK) == (t[None, :] // CHUNK)
    tril = same & (t[:, None] >= t[None, :])
    hh = np.arange(RWKV_WIDTH) // HEAD_DIM
    seg = hh[:, None] == hh[None, :]
    return (jnp.asarray(seg, BF16), jnp.asarray(tril, BF16), jnp.asarray(same, BF16))


def kernel(x, p, norm_mix_g, w_in, mu_shift, w0, w_up, a0, a_up, g_up, vres_down, mu_vres, v0, vres_up,
           k_k, k_a, r_k, lnx_g, lnx_b, sinks, rel_bias, w_out, norm_mlp_g, w_ff_up, w_ff_down,
           ple_proj, ple_norm_g, ple_gate, final_norm_g):
    B, S, _ = x.shape
    depth = w_in.shape[0]
    T = B * S
    n_chunks = SCAN_CHUNKS_PER_STEP
    row = lambda a: a.reshape(1, -1).astype(F32)

    qi = jnp.arange(BLOCK)[:, None] + BLOCK
    ki = jnp.arange(2 * BLOCK)[None, :]
    dist = qi - ki
    valid = (dist >= 0) & (dist < WINDOW)
    bias = rel_bias.astype(F32)[_t5_causal_bucket(dist)]
    bias = jnp.where(valid[..., None], bias, MASK_VALUE)
    bias_m = jnp.transpose(bias, (2, 0, 1)).reshape(SWA_Q_HEADS * BLOCK, 2 * BLOCK)
    consts = _scan_consts(n_chunks * CHUNK)

    x2 = x.reshape(T, D_MODEL)
    v_first = None
    for i in range(depth):
        w_cols = w_in[i]
        widths = [RWKV_COLS, SWA_WIDTH, 2 * SWA_KV_WIDTH]
        if i > 0:
            pad = jnp.zeros((D_MODEL, VRES_PAD - MV_LORA), w_cols.dtype)
            w_cols = jnp.concatenate([w_cols, vres_down[i - 1], pad], axis=1)
            widths.append(VRES_PAD)
        zs = _inproj(x2, row(norm_mix_g[i]), w_cols.astype(BF16), widths)
        zr = zs[0].reshape(B, S, RWKV_COLS)
        zq = zs[1].reshape(B, S, SWA_WIDTH)
        zkv = zs[2].reshape(B, S, 2 * SWA_KV_WIDTH)

        prm = {
            "mu": row(mu_shift[i]), "w0": row(w0[i]), "w_up": w_up[i].astype(BF16), "a0": row(a0[i]),
            "a_up": a_up[i].astype(BF16), "g_up": g_up[i].astype(BF16), "k_k": row(k_k[i]),
            "k_a": row(k_a[i]), "r_k": row(r_k[i]), "lnx_g": row(lnx_g[i]), "lnx_b": row(lnx_b[i]),
        }
        if i == 0:
            rwkv_out, v_first = _rwkv(False, zr, None, None, prm, consts, n_chunks)
        else:
            prm["mu_vres"] = jnp.pad(row(mu_vres[i - 1]), ((0, 0), (0, VRES_PAD - MV_LORA)))
            prm["v0"] = row(v0[i - 1])
            prm["vres_up"] = jnp.pad(vres_up[i - 1], ((0, VRES_PAD - MV_LORA), (0, 0))).astype(BF16)
            zv = zs[3].reshape(B, S, VRES_PAD)
            (rwkv_out,) = _rwkv(True, zr, zv, v_first, prm, consts, n_chunks)

        sink_col = jnp.repeat(sinks[i].astype(F32), BLOCK).reshape(SWA_Q_HEADS * BLOCK, 1)
        swa_out = _swa(zq, zkv, bias_m, sink_col)

        x2 = _post(i == depth - 1, x2, rwkv_out.reshape(T, RWKV_WIDTH), swa_out.reshape(T, SWA_WIDTH),
                   p[i].reshape(T, PLE_DIM), w_out[i].astype(BF16), row(norm_mlp_g[i]),
                   w_ff_up[i].astype(BF16), w_ff_down[i].astype(BF16), ple_proj[i].astype(BF16),
                   row(ple_norm_g[i]), ple_gate[i].astype(BF16), row(final_norm_g))
    return x2.reshape(B, S, D_MODEL)
```

```python
import functools
import math

import jax
import jax.numpy as jnp
import numpy as np
from jax import lax
from jax.experimental import pallas as pl
from jax.experimental.pallas import tpu as pltpu

F32 = jnp.float32
BF16 = jnp.bfloat16

D_MODEL = 1024
PLE_DIM = 256
HEADS = 8
HEAD_DIM = 64
RWKV_WIDTH = HEADS * HEAD_DIM
DECAY_LORA = 64
AAA_LORA = 64
MV_LORA = 32
GATE_LORA = 128
LNX_EPS = 64e-5
SWA_Q_HEADS = 8
SWA_KV_HEADS = 2
SWA_GROUP = SWA_Q_HEADS // SWA_KV_HEADS
SWA_WIDTH = SWA_Q_HEADS * HEAD_DIM
SWA_KV_WIDTH = SWA_KV_HEADS * HEAD_DIM
WINDOW = 128
BLOCK = 128
ATTN_SCALE = 1.0 / math.sqrt(HEAD_DIM)
REL_BUCKETS = 32
REL_MAX_DIST = 128
D_FF = 4 * D_MODEL
NORM_EPS = 1e-6
RWKV_COLS = 3 * RWKV_WIDTH + DECAY_LORA + AAA_LORA + GATE_LORA
SWA_COLS = SWA_WIDTH + 2 * SWA_KV_WIDTH
IN_COLS = RWKV_COLS + SWA_COLS

LANES = 128
VRES_PAD = LANES
CHUNK = 64
SCAN_CHUNKS_PER_STEP = 4
MASK_VALUE = -1e30

TM_PROJ = 512
TQ_SWA = 512
FF_CHUNK = 1024
VMEM_LIMIT = 56 * 1024 * 1024


def _rms(x, g):
    ms = jnp.mean(x * x, axis=-1, keepdims=True)
    return (x * lax.rsqrt(ms + NORM_EPS)) * g


def _bdot(a, b):
    return jnp.dot(a.astype(BF16), b.astype(BF16), preferred_element_type=F32)


def _split3(x):
    h1 = x.astype(BF16)
    r1 = x - h1.astype(F32)
    h2 = r1.astype(BF16)
    h3 = (r1 - h2.astype(F32)).astype(BF16)
    return h1, h2, h3


def _dot01_left(m01, x):
    h1, h2, h3 = _split3(x)
    d = lambda h: jnp.dot(m01, h, preferred_element_type=F32)
    return d(h1) + d(h2) + d(h3)


def _dot01_right(x, m01):
    h1, h2, h3 = _split3(x)
    d = lambda h: jnp.dot(h, m01, preferred_element_type=F32)
    return d(h1) + d(h2) + d(h3)


def _inproj_kernel(x_ref, g_ref, w_ref, *out_refs):
    h = _rms(x_ref[...], g_ref[...]).astype(BF16)
    off = 0
    for o_ref in out_refs:
        n = o_ref.shape[-1]
        o_ref[...] = jnp.dot(h, w_ref[:, off:off + n], preferred_element_type=F32)
        off += n


def _inproj(x2, g, w_bf16, widths):
    T = x2.shape[0]
    nc = w_bf16.shape[1]
    assert sum(widths) == nc
    return pl.pallas_call(
        _inproj_kernel,
        grid=(T // TM_PROJ,),
        in_specs=[
            pl.BlockSpec((TM_PROJ, D_MODEL), lambda i: (i, 0)),
            pl.BlockSpec((1, D_MODEL), lambda i: (0, 0)),
            pl.BlockSpec((D_MODEL, nc), lambda i: (0, 0)),
        ],
        out_specs=[pl.BlockSpec((TM_PROJ, n), lambda i: (i, 0)) for n in widths],
        out_shape=[jax.ShapeDtypeStruct((T, n), F32) for n in widths],
        compiler_params=pltpu.CompilerParams(
            dimension_semantics=("arbitrary",), vmem_limit_bytes=VMEM_LIMIT),
        name="inproj",
    )(x2, g, w_bf16)


def _sigmoid(x):
    return 1.0 / (1.0 + jnp.exp(-x))


def _softplus(x):
    return jnp.maximum(x, 0.0) + jnp.log1p(jnp.exp(-jnp.abs(x)))


def _shift(z, prev_row, mu):
    rolled = pltpu.roll(z, 1, 0)
    row = lax.broadcasted_iota(jnp.int32, z.shape, 0)
    prev = jnp.where(row == 0, prev_row, rolled)
    return z + (prev - z) * mu


def _rwkv_kernel(has_vres, n_chunks, *refs):
    if has_vres:
        (zr_ref, zv_ref, vf_ref, mu_ref, w0_ref, wup_ref, a0_ref, aup_ref, gup_ref, kk_ref, ka_ref,
         rk_ref, lg_ref, lb_ref, muv_ref, v0_ref, vup_ref, seg_ref, tril_ref, blk_ref,
         out_ref, h_ref, prev_ref, prevv_ref) = refs
    else:
        (zr_ref, mu_ref, w0_ref, wup_ref, a0_ref, aup_ref, gup_ref, kk_ref, ka_ref,
         rk_ref, lg_ref, lb_ref, seg_ref, tril_ref, blk_ref,
         out_ref, vf_out_ref, h_ref, prev_ref) = refs

    c_idx = pl.program_id(1)

    @pl.when(c_idx == 0)
    def _():
        h_ref[...] = jnp.zeros_like(h_ref)
        prev_ref[...] = jnp.zeros_like(prev_ref)
        if has_vres:
            prevv_ref[...] = jnp.zeros_like(prevv_ref)

    W = RWKV_WIDTH
    z = zr_ref[...]
    tt = z.shape[0]
    zs = _shift(z, prev_ref[...], mu_ref[...])
    prev_ref[...] = z[tt - 1:tt, :]
    r = zs[:, 0:W]
    k = zs[:, W:2 * W]
    v = zs[:, 2 * W:3 * W]
    o1 = 3 * W
    xw = zs[:, o1:o1 + DECAY_LORA]
    xa = zs[:, o1 + DECAY_LORA:o1 + DECAY_LORA + AAA_LORA]
    xg = zs[:, o1 + DECAY_LORA + AAA_LORA:]

    wraw = -_softplus(-(w0_ref[...] + _bdot(jnp.tanh(xw), wup_ref[...]))) - 0.5
    ld = -jnp.exp(wraw)
    a = _sigmoid(a0_ref[...] + _bdot(xa, aup_ref[...]))
    g = _bdot(_sigmoid(xg), gup_ref[...])
    if has_vres:
        zv = zv_ref[...]
        zvs = _shift(zv, prevv_ref[...], muv_ref[...])
        prevv_ref[...] = zv[tt - 1:tt, :]
        v = v + (vf_ref[...] - v) * _sigmoid(v0_ref[...] + _bdot(zvs, vup_ref[...]))
    else:
        vf_out_ref[...] = v

    seg = seg_ref[...]
    kk = k * kk_ref[...]
    nrm = jnp.sqrt(_dot01_right(kk * kk, seg))
    kk = kk / jnp.maximum(nrm, 1e-12)
    kadj = k * (1.0 + (a - 1.0) * ka_ref[...])
    b = kk * a

    c = _dot01_left(tril_ref[...], ld)
    cl = _dot01_left(blk_ref[...], ld)
    qt = kk * jnp.exp(c - ld)
    rt = r * jnp.exp(c)
    en = jnp.exp(-c)
    kt = kadj * en
    bt = b * en
    el = jnp.exp(cl - c)
    kh = kadj * el
    bh = b * el
    pl_all = jnp.exp(cl)

    L = CHUNK
    ri = lax.broadcasted_iota(jnp.int32, (L, L), 0)
    ci = lax.broadcasted_iota(jnp.int32, (L, L), 1)
    strict = ri > ci
    incl = ri >= ci
    eye = (ri == ci).astype(F32)

    N = HEAD_DIM
    units = [(ch, hd) for ch in range(n_chunks) for hd in range(HEADS)]

    def per_unit(arr):
        arr = arr.astype(BF16)
        return [arr[ch * L:(ch + 1) * L, hd * N:(hd + 1) * N] for ch, hd in units]

    cat0 = lambda *xs: jnp.concatenate(xs, axis=0)
    cat1 = lambda *xs: jnp.concatenate(xs, axis=1)
    mm = lambda a_, b_: jnp.dot(a_, b_, preferred_element_type=F32)
    nt = lambda a_, b_: lax.dot_general(a_, b_, (((1,), (1,)), ((), ())), preferred_element_type=F32)
    tn = lambda a_, b_: lax.dot_general(a_, b_, (((0,), (0,)), ((), ())), preferred_element_type=F32)

    qh, rh, kth, bth, vh = per_unit(qt), per_unit(rt), per_unit(kt), per_unit(bt), per_unit(v)
    khh, bhh = per_unit(kh), per_unit(bh)
    amat = [nt(cat0(q_, r_), cat0(k_, b_)) for q_, r_, k_, b_ in zip(qh, rh, kth, bth)]
    a_qk = [jnp.where(strict, m_[:L, :L], 0.0).astype(BF16) for m_ in amat]
    a_qb = [jnp.where(strict, m_[:L, L:], 0.0) for m_ in amat]
    a_rk = [jnp.where(incl, m_[L:, :L], 0.0).astype(BF16) for m_ in amat]
    a_rb = [jnp.where(incl, m_[L:, L:], 0.0).astype(BF16) for m_ in amat]
    tinv = [eye - m_ for m_ in a_qb]
    pw = [m_.astype(BF16) for m_ in a_qb]
    for _ in range(int(math.log2(L)) - 1):
        pw = [mm(p_, p_).astype(BF16) for p_ in pw]
        tinv = [t_ + mm(t_.astype(BF16), p_) for t_, p_ in zip(tinv, pw)]
    av = [mm(cat0(ak_, ar_), v_) for ak_, ar_, v_ in zip(a_qk, a_rk, vh)]
    x = [mm(t_.astype(BF16), cat1(q_, av_[:L].astype(BF16))).astype(BF16)
         for t_, q_, av_ in zip(tinv, qh, av)]
    ry = [cat1(rt[ch * L:(ch + 1) * L, hd * N:(hd + 1) * N], av_[L:]) - mm(ab_, x_)
          for (ch, hd), av_, ab_, x_ in zip(units, av, a_rb, x)]
    bx = [tn(b_, x_) for b_, x_ in zip(bhh, x)]
    kv = [tn(k_, v_) for k_, v_ in zip(khh, vh)]
    mmat = [(eye * pl_all[ch * L:ch * L + 1, hd * N:(hd + 1) * N] - bx_[:, :N]).astype(BF16)
            for (ch, hd), bx_ in zip(units, bx)]
    nmat = [kv_ - bx_[:, N:] for kv_, bx_ in zip(kv, bx)]

    h_cur = [h_ref[:, hd * N:(hd + 1) * N] for hd in range(HEADS)]
    y_rows = []
    for ch in range(n_chunks):
        y_heads = []
        for hd in range(HEADS):
            u = ch * HEADS + hd
            hb = h_cur[hd].astype(BF16)
            y_heads.append(mm(ry[u][:, :N].astype(BF16), hb) + ry[u][:, N:])
            h_cur[hd] = mm(mmat[u], hb) + nmat[u]
        y_rows.append(cat1(*y_heads))
    h_ref[...] = cat1(*h_cur)
    y = y_rows[0] if n_chunks == 1 else cat0(*y_rows)

    inv_n = 1.0 / HEAD_DIM
    ym = _dot01_right(y, seg) * inv_n
    yc = y - ym
    yv = _dot01_right(yc * yc, seg) * inv_n
    y = yc * lax.rsqrt(yv + LNX_EPS)
    y = y * lg_ref[...] + lb_ref[...]
    y = y + _dot01_right(r * kadj * rk_ref[...], seg) * v
    out_ref[...] = y * g


def _rwkv(has_vres, zr, zv, v_first, prm, consts, n_chunks):
    B, S, _ = zr.shape
    tt = n_chunks * CHUNK
    W = RWKV_WIDTH
    tile = lambda n: pl.BlockSpec((None, tt, n), lambda b, c: (b, c, 0))
    full = lambda a: pl.BlockSpec(a.shape, lambda b, c: (0,) * a.ndim)
    names = ["mu", "w0", "w_up", "a0", "a_up", "g_up", "k_k", "k_a", "r_k", "lnx_g", "lnx_b"]
    if has_vres:
        names += ["mu_vres", "v0", "vres_up"]
    params = [prm[n] for n in names] + list(consts)
    if has_vres:
        acts = [zr, zv, v_first]
        act_specs = [tile(RWKV_COLS), tile(VRES_PAD), tile(W)]
        out_shape = [jax.ShapeDtypeStruct((B, S, W), F32)]
        out_specs = [tile(W)]
        scratch = [pltpu.VMEM((HEAD_DIM, W), F32), pltpu.VMEM((1, RWKV_COLS), F32),
                   pltpu.VMEM((1, VRES_PAD), F32)]
    else:
        acts = [zr]
        act_specs = [tile(RWKV_COLS)]
        out_shape = [jax.ShapeDtypeStruct((B, S, W), F32)] * 2
        out_specs = [tile(W), tile(W)]
        scratch = [pltpu.VMEM((HEAD_DIM, W), F32), pltpu.VMEM((1, RWKV_COLS), F32)]
    return pl.pallas_call(
        functools.partial(_rwkv_kernel, has_vres, n_chunks),
        grid=(B, S // tt),
        in_specs=act_specs + [full(a) for a in params],
        out_specs=out_specs,
        out_shape=out_shape,
        scratch_shapes=scratch,
        compiler_params=pltpu.CompilerParams(
            dimension_semantics=("arbitrary", "arbitrary"), vmem_limit_bytes=VMEM_LIMIT),
        name="rwkv_vres" if has_vres else "rwkv",
    )(*acts, *params)


def _swa_kernel(n_blk, q_ref, kv_ref, kvp_ref, bkt_ref, relb_ref, sink_ref, o_ref, bias_ref):
    b_idx = pl.program_id(0)
    c_idx = pl.program_id(1)
    G2 = 2 * BLOCK

    @pl.when((b_idx == 0) & (c_idx == 0))
    def _build_bias():
        bkt = bkt_ref[...]
        for h in range(SWA_Q_HEADS):
            acc = jnp.full(bkt.shape, MASK_VALUE, F32)
            for k in range(REL_BUCKETS):
                acc = jnp.where(bkt == k, relb_ref[k, h], acc)
            pair, half = divmod(h, 2)
            bias_ref[pair * BLOCK:(pair + 1) * BLOCK, half * G2:(half + 1) * G2] = acc

    lo = lax.broadcasted_iota(jnp.int32, (1, LANES), 1) < HEAD_DIM
    ones_lo = jnp.broadcast_to(lo.astype(BF16), (G2, LANES))
    ones_hi = jnp.broadcast_to((~lo).astype(BF16), (G2, LANES))
    top = lax.broadcasted_iota(jnp.int32, (2 * BLOCK, 1), 0) < BLOCK
    prev_col = (lax.broadcasted_iota(jnp.int32, (2 * BLOCK, 2 * G2), 1) % G2) < BLOCK
    units = [(j, g) for j in range(n_blk) for g in range(SWA_KV_HEADS)]

    kk, vv = {}, {}
    for j in range(n_blk):
        if j == 0:
            kv = jnp.concatenate([kvp_ref[...], kv_ref[0:BLOCK, :]], axis=0)
        else:
            kv = kv_ref[(j - 1) * BLOCK:(j + 1) * BLOCK, :]
        kcol, vcol = kv[:, :LANES], kv[:, LANES:]
        for arr, dst, extra in ((kcol, kk, None), (vcol, vv, (ones_lo, ones_hi))):
            g0a = jnp.where(lo, arr, 0.0)
            g1b = jnp.where(lo, 0.0, arr)
            g0b = pltpu.roll(g0a, HEAD_DIM, 1)
            g1a = pltpu.roll(g1b, HEAD_DIM, 1)
            for g, (xa, xb) in enumerate(((g0a, g0b), (g1a, g1b))):
                xa, xb = xa.astype(BF16), xb.astype(BF16)
                if extra is None:
                    dst[j, g] = jnp.concatenate([xa, xb], axis=0)
                else:
                    dst[j, g] = jnp.concatenate(
                        [jnp.concatenate([xa, extra[0]], axis=1),
                         jnp.concatenate([xb, extra[1]], axis=1)], axis=0)

    def q_pairs(j, g):
        q2 = (q_ref[j * BLOCK:(j + 1) * BLOCK, g * 2 * LANES:(g + 1) * 2 * LANES] * ATTN_SCALE).astype(BF16)
        return jnp.concatenate([q2[:, :LANES], q2[:, LANES:]], axis=0)

    logits = [lax.dot_general(q_pairs(j, g), kk[j, g], (((1,), (1,)), ((), ())), preferred_element_type=F32)
              + bias_ref[g * 2 * BLOCK:(g + 1) * 2 * BLOCK, :] for j, g in units]
    logits = [jnp.where(prev_col & (c_idx == 0), MASK_VALUE, l_) if j == 0 else l_
              for (j, g), l_ in zip(units, logits)]
    sinks = [[jnp.where(top, sink_ref[4 * g + half], sink_ref[4 * g + 2 + half]) for half in range(2)]
             for j, g in units]
    ms = [[jnp.maximum(jnp.max(l_[:, half * G2:(half + 1) * G2], axis=-1, keepdims=True), s_[half])
           for half in range(2)] for l_, s_ in zip(logits, sinks)]
    es = [jnp.concatenate([jnp.exp(l_[:, half * G2:(half + 1) * G2] - m_[half]) for half in range(2)],
                          axis=1).astype(BF16) for l_, m_ in zip(logits, ms)]
    res = [jnp.dot(e_, vv[u], preferred_element_type=F32) for e_, u in zip(es, units)]
    for (j, g), r_, m_, s_ in zip(units, res, ms, sinks):
        den = r_[:, LANES:] + jnp.where(lo, jnp.exp(s_[0] - m_[0]), jnp.exp(s_[1] - m_[1]))
        out = r_[:, :LANES] / den
        rows = slice(j * BLOCK, (j + 1) * BLOCK)
        o_ref[rows, (2 * g) * LANES:(2 * g + 1) * LANES] = out[:BLOCK]
        o_ref[rows, (2 * g + 1) * LANES:(2 * g + 2) * LANES] = out[BLOCK:]


def _swa(zq, zkv, bkt, rel_bias, sinks):
    B, S, _ = zq.shape
    n_blk = TQ_SWA // BLOCK
    smem = lambda a: pl.BlockSpec(a.shape, lambda b, c: (0,) * a.ndim, memory_space=pltpu.SMEM)
    return pl.pallas_call(
        functools.partial(_swa_kernel, n_blk),
        grid=(B, S // TQ_SWA),
        in_specs=[
            pl.BlockSpec((None, TQ_SWA, SWA_WIDTH), lambda b, c: (b, c, 0)),
            pl.BlockSpec((None, TQ_SWA, 2 * SWA_KV_WIDTH), lambda b, c: (b, c, 0)),
            pl.BlockSpec((None, BLOCK, 2 * SWA_KV_WIDTH),
                         lambda b, c: (b, jnp.maximum(c * n_blk - 1, 0), 0)),
            pl.BlockSpec(bkt.shape, lambda b, c: (0, 0)),
            smem(rel_bias),
            smem(sinks),
        ],
        out_specs=pl.BlockSpec((None, TQ_SWA, SWA_WIDTH), lambda b, c: (b, c, 0)),
        out_shape=jax.ShapeDtypeStruct((B, S, SWA_WIDTH), F32),
        scratch_shapes=[pltpu.VMEM((SWA_Q_HEADS // 2 * BLOCK, 2 * 2 * BLOCK), F32)],
        compiler_params=pltpu.CompilerParams(
            dimension_semantics=("arbitrary", "arbitrary"), vmem_limit_bytes=VMEM_LIMIT),
        name="swa",
    )(zq, zkv, zkv, bkt, rel_bias, sinks)


def _post_kernel(final, x_ref, ro_ref, so_ref, p_ref, wo_ref, gm_ref, up_ref, dn_ref, pp_ref, pg_ref,
                 gate_ref, fg_ref, o_ref):
    W = RWKV_WIDTH
    x = x_ref[...]
    x = x + _bdot(ro_ref[...], wo_ref[0:W, :]) + _bdot(so_ref[...], wo_ref[W:2 * W, :])
    u = _rms(x, gm_ref[...]).astype(BF16)
    acc = jnp.zeros_like(x)
    for c in range(D_FF // FF_CHUNK):
        cs = slice(c * FF_CHUNK, (c + 1) * FF_CHUNK)
        hid = jnp.dot(u, up_ref[:, cs], preferred_element_type=F32)
        hid = jnp.square(jnp.maximum(hid, 0.0))
        acc = acc + jnp.dot(hid.astype(BF16), dn_ref[cs, :], preferred_element_type=F32)
    x = x + acc
    e = _rms(_bdot(p_ref[...], pp_ref[...]), pg_ref[...])
    x = x + e * _sigmoid(_bdot(x, gate_ref[...]))
    if final:
        x = _rms(x, fg_ref[...])
    o_ref[...] = x


def _post(final, x2, ro, so, p2, wo, gm, up, dn, pp, pg, gate, fg):
    T = x2.shape[0]
    tile = lambda n: pl.BlockSpec((TM_PROJ, n), lambda i: (i, 0))
    res = lambda a: pl.BlockSpec(a.shape, lambda i: (0, 0), pipeline_mode=pl.Buffered(1))
    weights = [wo, gm, up, dn, pp, pg, gate, fg]
    return pl.pallas_call(
        functools.partial(_post_kernel, final),
        grid=(T // TM_PROJ,),
        in_specs=[tile(D_MODEL), tile(RWKV_WIDTH), tile(SWA_WIDTH), tile(PLE_DIM)] + [res(a) for a in weights],
        out_specs=tile(D_MODEL),
        out_shape=jax.ShapeDtypeStruct((T, D_MODEL), F32),
        compiler_params=pltpu.CompilerParams(
            dimension_semantics=("arbitrary",), vmem_limit_bytes=VMEM_LIMIT),
        name="post_final" if final else "post",
    )(x2, ro, so, p2, *weights)


def _bucket_tile():
    max_exact = REL_BUCKETS // 2
    dist = (np.arange(BLOCK)[:, None] + BLOCK) - np.arange(2 * BLOCK)[None, :]
    n = np.maximum(dist, 0)
    nf = np.maximum(n, 1).astype(np.float32)
    scaled = (np.log(nf / np.float32(max_exact)) / np.float32(math.log(REL_MAX_DIST / max_exact))
              * np.float32(REL_BUCKETS - max_exact)).astype(np.float32)
    large = np.minimum(max_exact + scaled.astype(np.int32), REL_BUCKETS - 1)
    bucket = np.where(n < max_exact, n, large)
    valid = (dist >= 0) & (dist < WINDOW)
    return np.where(valid, bucket, -1).astype(np.int32)


def _scan_consts(tt):
    t = np.arange(tt)
    same = (t[:, None] // CHUNK) == (t[None, :] // CHUNK)
    tril = same & (t[:, None] >= t[None, :])
    hh = np.arange(RWKV_WIDTH) // HEAD_DIM
    seg = hh[:, None] == hh[None, :]
    return (jnp.asarray(seg, BF16), jnp.asarray(tril, BF16), jnp.asarray(same, BF16))


def kernel(x, p, norm_mix_g, w_in, mu_shift, w0, w_up, a0, a_up, g_up, vres_down, mu_vres, v0, vres_up,
           k_k, k_a, r_k, lnx_g, lnx_b, sinks, rel_bias, w_out, norm_mlp_g, w_ff_up, w_ff_down,
           ple_proj, ple_norm_g, ple_gate, final_norm_g):
    B, S, _ = x.shape
    depth = w_in.shape[0]
    T = B * S
    n_chunks = SCAN_CHUNKS_PER_STEP
    row = lambda a: a.reshape(1, -1).astype(F32)

    bkt = jnp.asarray(_bucket_tile())
    consts = _scan_consts(n_chunks * CHUNK)

    x2 = x.reshape(T, D_MODEL)
    v_first = None
    for i in range(depth):
        w_cols = w_in[i]
        widths = [RWKV_COLS, SWA_WIDTH, 2 * SWA_KV_WIDTH]
        if i > 0:
            pad = jnp.zeros((D_MODEL, VRES_PAD - MV_LORA), w_cols.dtype)
            w_cols = jnp.concatenate([w_cols, vres_down[i - 1], pad], axis=1)
            widths.append(VRES_PAD)
        zs = _inproj(x2, row(norm_mix_g[i]), w_cols.astype(BF16), widths)
        zr = zs[0].reshape(B, S, RWKV_COLS)
        zq = zs[1].reshape(B, S, SWA_WIDTH)
        zkv = zs[2].reshape(B, S, 2 * SWA_KV_WIDTH)

        prm = {
            "mu": row(mu_shift[i]), "w0": row(w0[i]), "w_up": w_up[i].astype(BF16), "a0": row(a0[i]),
            "a_up": a_up[i].astype(BF16), "g_up": g_up[i].astype(BF16), "k_k": row(k_k[i]),
            "k_a": row(k_a[i]), "r_k": row(r_k[i]), "lnx_g": row(lnx_g[i]), "lnx_b": row(lnx_b[i]),
        }
        if i == 0:
            rwkv_out, v_first = _rwkv(False, zr, None, None, prm, consts, n_chunks)
        else:
            prm["mu_vres"] = jnp.pad(row(mu_vres[i - 1]), ((0, 0), (0, VRES_PAD - MV_LORA)))
            prm["v0"] = row(v0[i - 1])
            prm["vres_up"] = jnp.pad(vres_up[i - 1], ((0, VRES_PAD - MV_LORA), (0, 0))).astype(BF16)
            zv = zs[3].reshape(B, S, VRES_PAD)
            (rwkv_out,) = _rwkv(True, zr, zv, v_first, prm, consts, n_chunks)

        swa_out = _swa(zq, zkv, bkt, rel_bias.astype(F32), sinks[i].astype(F32))

        x2 = _post(i == depth - 1, x2, rwkv_out.reshape(T, RWKV_WIDTH), swa_out.reshape(T, SWA_WIDTH),
                   p[i].reshape(T, PLE_DIM), w_out[i].astype(BF16), row(norm_mlp_g[i]),
                   w_ff_up[i].astype(BF16), w_ff_down[i].astype(BF16), ple_proj[i].astype(BF16),
                   row(ple_norm_g[i]), ple_gate[i].astype(BF16), row(final_norm_g))
    return x2.reshape(B, S, D_MODEL)
```

```python
import functools
import math

import jax
import jax.numpy as jnp
import numpy as np
from jax import lax
from jax.experimental import pallas as pl
from jax.experimental.pallas import tpu as pltpu

F32 = jnp.float32
BF16 = jnp.bfloat16

D_MODEL = 1024
PLE_DIM = 256
HEADS = 8
HEAD_DIM = 64
RWKV_WIDTH = HEADS * HEAD_DIM
DECAY_LORA = 64
AAA_LORA = 64
MV_LORA = 32
GATE_LORA = 128
LNX_EPS = 64e-5
SWA_Q_HEADS = 8
SWA_KV_HEADS = 2
SWA_GROUP = SWA_Q_HEADS // SWA_KV_HEADS
SWA_WIDTH = SWA_Q_HEADS * HEAD_DIM
SWA_KV_WIDTH = SWA_KV_HEADS * HEAD_DIM
WINDOW = 128
BLOCK = 128
ATTN_SCALE = 1.0 / math.sqrt(HEAD_DIM)
REL_BUCKETS = 32
REL_MAX_DIST = 128
D_FF = 4 * D_MODEL
NORM_EPS = 1e-6
RWKV_COLS = 3 * RWKV_WIDTH + DECAY_LORA + AAA_LORA + GATE_LORA
SWA_COLS = SWA_WIDTH + 2 * SWA_KV_WIDTH
IN_COLS = RWKV_COLS + SWA_COLS

LANES = 128
VRES_PAD = LANES
CHUNK = 64
SCAN_CHUNKS_PER_STEP = 4
MASK_VALUE = -1e30

TM_PROJ = 512
TQ_SWA = 512
FF_CHUNK = 1024
VMEM_LIMIT = 56 * 1024 * 1024


def _rms(x, g):
    ms = jnp.mean(x * x, axis=-1, keepdims=True)
    return (x * lax.rsqrt(ms + NORM_EPS)) * g


def _bdot(a, b):
    return jnp.dot(a.astype(BF16), b.astype(BF16), preferred_element_type=F32)


def _split2(x):
    hi = x.astype(BF16)
    lo = (x - hi.astype(F32)).astype(BF16)
    return hi, lo


def _dot01_left(m01, x):
    hi, lo = _split2(x)
    return jnp.dot(m01, hi, preferred_element_type=F32) + jnp.dot(m01, lo, preferred_element_type=F32)


def _head_sums(x, seg):
    half = seg.shape[0]
    outs = []
    for c in range(x.shape[1] // half):
        hi, lo = _split2(x[:, c * half:(c + 1) * half])
        outs.append(jnp.dot(hi, seg, preferred_element_type=F32) + jnp.dot(lo, seg, preferred_element_type=F32))
    return jnp.concatenate(outs, axis=1)


def _inproj_kernel(x_ref, g_ref, w_ref, *out_refs):
    h = _rms(x_ref[...], g_ref[...]).astype(BF16)
    off = 0
    for o_ref in out_refs:
        n = o_ref.shape[-1]
        o_ref[...] = jnp.dot(h, w_ref[:, off:off + n], preferred_element_type=F32)
        off += n


def _inproj(x2, g, w_bf16, widths):
    T = x2.shape[0]
    nc = w_bf16.shape[1]
    assert sum(widths) == nc
    return pl.pallas_call(
        _inproj_kernel,
        grid=(T // TM_PROJ,),
        in_specs=[
            pl.BlockSpec((TM_PROJ, D_MODEL), lambda i: (i, 0)),
            pl.BlockSpec((1, D_MODEL), lambda i: (0, 0)),
            pl.BlockSpec((D_MODEL, nc), lambda i: (0, 0)),
        ],
        out_specs=[pl.BlockSpec((TM_PROJ, n), lambda i: (i, 0)) for n in widths],
        out_shape=[jax.ShapeDtypeStruct((T, n), F32) for n in widths],
        compiler_params=pltpu.CompilerParams(
            dimension_semantics=("arbitrary",), vmem_limit_bytes=VMEM_LIMIT),
        name="inproj",
    )(x2, g, w_bf16)


def _sigmoid(x):
    return 1.0 / (1.0 + jnp.exp(-x))


def _softplus(x):
    return jnp.maximum(x, 0.0) + jnp.log1p(jnp.exp(-jnp.abs(x)))


def _shift(z, prev_row, mu):
    rolled = pltpu.roll(z, 1, 0)
    row = lax.broadcasted_iota(jnp.int32, z.shape, 0)
    prev = jnp.where(row == 0, prev_row, rolled)
    return z + (prev - z) * mu


def _rwkv_kernel(has_vres, n_chunks, *refs):
    if has_vres:
        (zr_ref, zv_ref, vf_ref, mu_ref, w0_ref, wup_ref, a0_ref, aup_ref, gup_ref, kk_ref, ka_ref,
         rk_ref, lg_ref, lb_ref, muv_ref, v0_ref, vup_ref, seg_ref, tril_ref,
         out_ref, h_ref, prev_ref, prevv_ref) = refs
    else:
        (zr_ref, mu_ref, w0_ref, wup_ref, a0_ref, aup_ref, gup_ref, kk_ref, ka_ref,
         rk_ref, lg_ref, lb_ref, seg_ref, tril_ref,
         out_ref, vf_out_ref, h_ref, prev_ref) = refs

    c_idx = pl.program_id(1)

    @pl.when(c_idx == 0)
    def _():
        h_ref[...] = jnp.zeros_like(h_ref)
        prev_ref[...] = jnp.zeros_like(prev_ref)
        if has_vres:
            prevv_ref[...] = jnp.zeros_like(prevv_ref)

    W = RWKV_WIDTH
    z = zr_ref[...]
    tt = z.shape[0]
    zs = _shift(z, prev_ref[...], mu_ref[...])
    prev_ref[...] = z[tt - 1:tt, :]
    r = zs[:, 0:W]
    k = zs[:, W:2 * W]
    v = zs[:, 2 * W:3 * W]
    o1 = 3 * W
    xw = zs[:, o1:o1 + DECAY_LORA]
    xa = zs[:, o1 + DECAY_LORA:o1 + DECAY_LORA + AAA_LORA]
    xg = zs[:, o1 + DECAY_LORA + AAA_LORA:]

    wraw = -_softplus(-(w0_ref[...] + _bdot(jnp.tanh(xw), wup_ref[...]))) - 0.5
    ld = -jnp.exp(wraw)
    a = _sigmoid(a0_ref[...] + _bdot(xa, aup_ref[...]))
    g = _bdot(_sigmoid(xg), gup_ref[...])
    if has_vres:
        zv = zv_ref[...]
        zvs = _shift(zv, prevv_ref[...], muv_ref[...])
        prevv_ref[...] = zv[tt - 1:tt, :]
        v = v + (vf_ref[...] - v) * _sigmoid(v0_ref[...] + _bdot(zvs, vup_ref[...]))
    else:
        vf_out_ref[...] = v

    seg = seg_ref[...]
    kk = k * kk_ref[...]
    nrm = jnp.sqrt(_head_sums(kk * kk, seg))
    kk = kk / jnp.maximum(nrm, 1e-12)
    kadj = k * (1.0 + (a - 1.0) * ka_ref[...])
    b = kk * a

    c = _dot01_left(tril_ref[...], ld)
    cl = jnp.concatenate(
        [jnp.broadcast_to(c[(ch + 1) * CHUNK - 1:(ch + 1) * CHUNK, :], (CHUNK, W)) for ch in range(n_chunks)],
        axis=0)
    qt = kk * jnp.exp(c - ld)
    rt = r * jnp.exp(c)
    en = jnp.exp(-c)
    kt = kadj * en
    bt = b * en
    el = jnp.exp(cl - c)
    kh = kadj * el
    bh = b * el
    pl_all = jnp.exp(cl)

    L = CHUNK
    P2 = 2 * HEAD_DIM
    lo = lax.broadcasted_iota(jnp.int32, (1, P2), 1) < HEAD_DIM
    ri = lax.broadcasted_iota(jnp.int32, (L, P2), 0)
    ci = lax.broadcasted_iota(jnp.int32, (L, P2), 1) % HEAD_DIM
    strict = ri > ci
    incl = ri >= ci
    eye2 = (ri == ci).astype(F32)
    units = [(ch, pr) for ch in range(n_chunks) for pr in range(HEADS // 2)]

    cat0 = lambda *xs: jnp.concatenate(xs, axis=0)
    cat1 = lambda *xs: jnp.concatenate(xs, axis=1)
    mm = lambda a_, b_: jnp.dot(a_, b_, preferred_element_type=F32)
    nt = lambda a_, b_: lax.dot_general(a_, b_, (((1,), (1,)), ((), ())), preferred_element_type=F32)
    tn = lambda a_, b_: lax.dot_general(a_, b_, (((0,), (0,)), ((), ())), preferred_element_type=F32)
    zero = jnp.zeros((), BF16)

    def bd(x_):
        x_ = x_.astype(BF16)
        return cat0(jnp.where(lo, x_, zero), jnp.where(lo, zero, x_))

    def diag_blocks(x_):
        sel = lo if x_.shape[1] == P2 else jnp.concatenate([lo] * (x_.shape[1] // P2), axis=1)
        return jnp.where(sel, x_[:HEAD_DIM], x_[HEAD_DIM:])

    def per_unit(arr, dtype=BF16):
        arr = arr.astype(dtype)
        return [arr[ch * L:(ch + 1) * L, pr * P2:(pr + 1) * P2] for ch, pr in units]

    qh, rh, kth, bth, vh = per_unit(qt), per_unit(rt), per_unit(kt), per_unit(bt), per_unit(v)
    khh, bhh = per_unit(kh), per_unit(bh)
    rf = per_unit(rt, F32)
    amat = [nt(cat0(q_, r_), cat0(bd(k_), bd(b_))) for q_, r_, k_, b_ in zip(qh, rh, kth, bth)]
    a_qk = [jnp.where(strict, m_[:L, :P2], 0.0).astype(BF16) for m_ in amat]
    a_qb = [jnp.where(strict, m_[:L, P2:], 0.0) for m_ in amat]
    a_rk = [jnp.where(incl, m_[L:, :P2], 0.0).astype(BF16) for m_ in amat]
    a_rb = [jnp.where(incl, m_[L:, P2:], 0.0).astype(BF16) for m_ in amat]
    tinv = [eye2 - m_ for m_ in a_qb]
    pw = [m_.astype(BF16) for m_ in a_qb]
    pw = [mm(p_, bd(p_)).astype(BF16) for p_ in pw]
    for _ in range(int(math.log2(L)) - 2):
        both = [mm(cat0(t_.astype(BF16), p_), bd(p_)) for t_, p_ in zip(tinv, pw)]
        tinv = [t_ + b_[:L] for t_, b_ in zip(tinv, both)]
        pw = [b_[L:].astype(BF16) for b_ in both]
    tinv = [t_ + mm(t_.astype(BF16), bd(p_)) for t_, p_ in zip(tinv, pw)]
    av = [mm(cat0(ak_, ar_), bd(v_)) for ak_, ar_, v_ in zip(a_qk, a_rk, vh)]
    x = [mm(t_.astype(BF16), cat1(bd(q_), bd(av_[:L]))).astype(BF16)
         for t_, q_, av_ in zip(tinv, qh, av)]
    ry = [cat1(r_, av_[L:]) - mm(ab_, cat1(bd(x_[:, :P2]), bd(x_[:, P2:])))
          for r_, av_, ab_, x_ in zip(rf, av, a_rb, x)]
    bx = [diag_blocks(tn(b_, x_)) for b_, x_ in zip(bhh, x)]
    kv = [diag_blocks(tn(k_, v_)) for k_, v_ in zip(khh, vh)]
    mmat = [(eye2 * pl_all[ch * L:ch * L + 1, pr * P2:(pr + 1) * P2] - bx_[:, :P2]).astype(BF16)
            for (ch, pr), bx_ in zip(units, bx)]
    nmat = [kv_ - bx_[:, P2:] for kv_, bx_ in zip(kv, bx)]

    n_pairs = HEADS // 2
    h_cur = [h_ref[:, pr * P2:(pr + 1) * P2] for pr in range(n_pairs)]
    y_rows = []
    for ch in range(n_chunks):
        y_pairs = []
        for pr in range(n_pairs):
            u = ch * n_pairs + pr
            both = mm(cat0(ry[u][:, :P2].astype(BF16), mmat[u]), bd(h_cur[pr]))
            y_pairs.append(both[:L] + ry[u][:, P2:])
            h_cur[pr] = both[L:] + nmat[u]
        y_rows.append(cat1(*y_pairs))
    h_ref[...] = cat1(*h_cur)
    y = y_rows[0] if n_chunks == 1 else cat0(*y_rows)

    inv_n = 1.0 / HEAD_DIM
    ym = _head_sums(y, seg) * inv_n
    yc = y - ym
    yv = _head_sums(yc * yc, seg) * inv_n
    y = yc * lax.rsqrt(yv + LNX_EPS)
    y = y * lg_ref[...] + lb_ref[...]
    y = y + _head_sums(r * kadj * rk_ref[...], seg) * v
    out_ref[...] = y * g


def _rwkv(has_vres, zr, zv, v_first, prm, consts, n_chunks):
    B, S, _ = zr.shape
    tt = n_chunks * CHUNK
    W = RWKV_WIDTH
    tile = lambda n: pl.BlockSpec((None, tt, n), lambda b, c: (b, c, 0))
    full = lambda a: pl.BlockSpec(a.shape, lambda b, c: (0,) * a.ndim)
    names = ["mu", "w0", "w_up", "a0", "a_up", "g_up", "k_k", "k_a", "r_k", "lnx_g", "lnx_b"]
    if has_vres:
        names += ["mu_vres", "v0", "vres_up"]
    params = [prm[n] for n in names] + list(consts)
    if has_vres:
        acts = [zr, zv, v_first]
        act_specs = [tile(RWKV_COLS), tile(VRES_PAD), tile(W)]
        out_shape = [jax.ShapeDtypeStruct((B, S, W), F32)]
        out_specs = [tile(W)]
        scratch = [pltpu.VMEM((HEAD_DIM, W), F32), pltpu.VMEM((1, RWKV_COLS), F32),
                   pltpu.VMEM((1, VRES_PAD), F32)]
    else:
        acts = [zr]
        act_specs = [tile(RWKV_COLS)]
        out_shape = [jax.ShapeDtypeStruct((B, S, W), F32)] * 2
        out_specs = [tile(W), tile(W)]
        scratch = [pltpu.VMEM((HEAD_DIM, W), F32), pltpu.VMEM((1, RWKV_COLS), F32)]
    return pl.pallas_call(
        functools.partial(_rwkv_kernel, has_vres, n_chunks),
        grid=(B, S // tt),
        in_specs=act_specs + [full(a) for a in params],
        out_specs=out_specs,
        out_shape=out_shape,
        scratch_shapes=scratch,
        compiler_params=pltpu.CompilerParams(
            dimension_semantics=("arbitrary", "arbitrary"), vmem_limit_bytes=VMEM_LIMIT),
        name="rwkv_vres" if has_vres else "rwkv",
    )(*acts, *params)


def _swa_kernel(n_blk, q_ref, kv_ref, kvp_ref, bkt_ref, relb_ref, sink_ref, o_ref, bias_ref):
    b_idx = pl.program_id(0)
    c_idx = pl.program_id(1)
    G2 = 2 * BLOCK

    @pl.when((b_idx == 0) & (c_idx == 0))
    def _build_bias():
        bkt = bkt_ref[...]
        for h in range(SWA_Q_HEADS):
            acc = jnp.full(bkt.shape, MASK_VALUE, F32)
            for k in range(REL_BUCKETS):
                acc = jnp.where(bkt == k, relb_ref[k, h], acc)
            pair, half = divmod(h, 2)
            bias_ref[pair * BLOCK:(pair + 1) * BLOCK, half * G2:(half + 1) * G2] = acc

    lo = lax.broadcasted_iota(jnp.int32, (1, LANES), 1) < HEAD_DIM
    ones_lo = jnp.broadcast_to(lo.astype(BF16), (G2, LANES))
    ones_hi = jnp.broadcast_to((~lo).astype(BF16), (G2, LANES))
    top = lax.broadcasted_iota(jnp.int32, (2 * BLOCK, 1), 0) < BLOCK
    prev_col = (lax.broadcasted_iota(jnp.int32, (2 * BLOCK, 2 * G2), 1) % G2) < BLOCK
    units = [(j, g) for j in range(n_blk) for g in range(SWA_KV_HEADS)]

    kk, vv = {}, {}
    for j in range(n_blk):
        if j == 0:
            kv = jnp.concatenate([kvp_ref[...], kv_ref[0:BLOCK, :]], axis=0)
        else:
            kv = kv_ref[(j - 1) * BLOCK:(j + 1) * BLOCK, :]
        kcol, vcol = kv[:, :LANES], kv[:, LANES:]
        for arr, dst, extra in ((kcol, kk, None), (vcol, vv, (ones_lo, ones_hi))):
            g0a = jnp.where(lo, arr, 0.0)
            g1b = jnp.where(lo, 0.0, arr)
            g0b = pltpu.roll(g0a, HEAD_DIM, 1)
            g1a = pltpu.roll(g1b, HEAD_DIM, 1)
            for g, (xa, xb) in enumerate(((g0a, g0b), (g1a, g1b))):
                xa, xb = xa.astype(BF16), xb.astype(BF16)
                if extra is None:
                    dst[j, g] = jnp.concatenate([xa, xb], axis=0)
                else:
                    dst[j, g] = jnp.concatenate(
                        [jnp.concatenate([xa, extra[0]], axis=1),
                         jnp.concatenate([xb, extra[1]], axis=1)], axis=0)

    def q_pairs(j, g):
        q2 = (q_ref[j * BLOCK:(j + 1) * BLOCK, g * 2 * LANES:(g + 1) * 2 * LANES] * ATTN_SCALE).astype(BF16)
        return jnp.concatenate([q2[:, :LANES], q2[:, LANES:]], axis=0)

    logits = [lax.dot_general(q_pairs(j, g), kk[j, g], (((1,), (1,)), ((), ())), preferred_element_type=F32)
              + bias_ref[g * 2 * BLOCK:(g + 1) * 2 * BLOCK, :] for j, g in units]
    logits = [jnp.where(prev_col & (c_idx == 0), MASK_VALUE, l_) if j == 0 else l_
              for (j, g), l_ in zip(units, logits)]
    sinks = [[jnp.where(top, sink_ref[4 * g + half], sink_ref[4 * g + 2 + half]) for half in range(2)]
             for j, g in units]
    ms = [[jnp.maximum(jnp.max(l_[:, half * G2:(half + 1) * G2], axis=-1, keepdims=True), s_[half])
           for half in range(2)] for l_, s_ in zip(logits, sinks)]
    es = [jnp.concatenate([jnp.exp(l_[:, half * G2:(half + 1) * G2] - m_[half]) for half in range(2)],
                          axis=1).astype(BF16) for l_, m_ in zip(logits, ms)]
    res = [jnp.dot(e_, vv[u], preferred_element_type=F32) for e_, u in zip(es, units)]
    for (j, g), r_, m_, s_ in zip(units, res, ms, sinks):
        den = r_[:, LANES:] + jnp.where(lo, jnp.exp(s_[0] - m_[0]), jnp.exp(s_[1] - m_[1]))
        out = r_[:, :LANES] / den
        rows = slice(j * BLOCK, (j + 1) * BLOCK)
        o_ref[rows, (2 * g) * LANES:(2 * g + 1) * LANES] = out[:BLOCK]
        o_ref[rows, (2 * g + 1) * LANES:(2 * g + 2) * LANES] = out[BLOCK:]


def _swa(zq, zkv, bkt, rel_bias, sinks):
    B, S, _ = zq.shape
    n_blk = TQ_SWA // BLOCK
    smem = lambda a: pl.BlockSpec(a.shape, lambda b, c: (0,) * a.ndim, memory_space=pltpu.SMEM)
    return pl.pallas_call(
        functools.partial(_swa_kernel, n_blk),
        grid=(B, S // TQ_SWA),
        in_specs=[
            pl.BlockSpec((None, TQ_SWA, SWA_WIDTH), lambda b, c: (b, c, 0)),
            pl.BlockSpec((None, TQ_SWA, 2 * SWA_KV_WIDTH), lambda b, c: (b, c, 0)),
            pl.BlockSpec((None, BLOCK, 2 * SWA_KV_WIDTH),
                         lambda b, c: (b, jnp.maximum(c * n_blk - 1, 0), 0)),
            pl.BlockSpec(bkt.shape, lambda b, c: (0, 0)),
            smem(rel_bias),
            smem(sinks),
        ],
        out_specs=pl.BlockSpec((None, TQ_SWA, SWA_WIDTH), lambda b, c: (b, c, 0)),
        out_shape=jax.ShapeDtypeStruct((B, S, SWA_WIDTH), F32),
        scratch_shapes=[pltpu.VMEM((SWA_Q_HEADS // 2 * BLOCK, 2 * 2 * BLOCK), F32)],
        compiler_params=pltpu.CompilerParams(
            dimension_semantics=("arbitrary", "arbitrary"), vmem_limit_bytes=VMEM_LIMIT),
        name="swa",
    )(zq, zkv, zkv, bkt, rel_bias, sinks)


def _post_kernel(final, x_ref, ro_ref, so_ref, p_ref, wo_ref, gm_ref, up_ref, dn_ref, pp_ref, pg_ref,
                 gate_ref, fg_ref, o_ref):
    W = RWKV_WIDTH
    x = x_ref[...]
    x = x + _bdot(ro_ref[...], wo_ref[0:W, :]) + _bdot(so_ref[...], wo_ref[W:2 * W, :])
    u = _rms(x, gm_ref[...]).astype(BF16)
    acc = jnp.zeros_like(x)
    for c in range(D_FF // FF_CHUNK):
        cs = slice(c * FF_CHUNK, (c + 1) * FF_CHUNK)
        hid = jnp.dot(u, up_ref[:, cs], preferred_element_type=F32)
        hid = jnp.square(jnp.maximum(hid, 0.0))
        acc = acc + jnp.dot(hid.astype(BF16), dn_ref[cs, :], preferred_element_type=F32)
    x = x + acc
    e = _rms(_bdot(p_ref[...], pp_ref[...]), pg_ref[...])
    x = x + e * _sigmoid(_bdot(x, gate_ref[...]))
    if final:
        x = _rms(x, fg_ref[...])
    o_ref[...] = x


def _post(final, x2, ro, so, p2, wo, gm, up, dn, pp, pg, gate, fg):
    T = x2.shape[0]
    tile = lambda n: pl.BlockSpec((TM_PROJ, n), lambda i: (i, 0))
    res = lambda a: pl.BlockSpec(a.shape, lambda i: (0, 0), pipeline_mode=pl.Buffered(1))
    weights = [wo, gm, up, dn, pp, pg, gate, fg]
    return pl.pallas_call(
        functools.partial(_post_kernel, final),
        grid=(T // TM_PROJ,),
        in_specs=[tile(D_MODEL), tile(RWKV_WIDTH), tile(SWA_WIDTH), tile(PLE_DIM)] + [res(a) for a in weights],
        out_specs=tile(D_MODEL),
        out_shape=jax.ShapeDtypeStruct((T, D_MODEL), F32),
        compiler_params=pltpu.CompilerParams(
            dimension_semantics=("arbitrary",), vmem_limit_bytes=VMEM_LIMIT),
        name="post_final" if final else "post",
    )(x2, ro, so, p2, *weights)


def _bucket_tile():
    max_exact = REL_BUCKETS // 2
    dist = (np.arange(BLOCK)[:, None] + BLOCK) - np.arange(2 * BLOCK)[None, :]
    n = np.maximum(dist, 0)
    nf = np.maximum(n, 1).astype(np.float32)
    scaled = (np.log(nf / np.float32(max_exact)) / np.float32(math.log(REL_MAX_DIST / max_exact))
              * np.float32(REL_BUCKETS - max_exact)).astype(np.float32)
    large = np.minimum(max_exact + scaled.astype(np.int32), REL_BUCKETS - 1)
    bucket = np.where(n < max_exact, n, large)
    valid = (dist >= 0) & (dist < WINDOW)
    return np.where(valid, bucket, -1).astype(np.int32)


def _scan_consts(tt):
    t = np.arange(tt)
    same = (t[:, None] // CHUNK) == (t[None, :] // CHUNK)
    tril = same & (t[:, None] >= t[None, :])
    hh = np.arange(4 * HEAD_DIM) // HEAD_DIM
    seg = hh[:, None] == hh[None, :]
    return (jnp.asarray(seg, BF16), jnp.asarray(tril, BF16))


def kernel(x, p, norm_mix_g, w_in, mu_shift, w0, w_up, a0, a_up, g_up, vres_down, mu_vres, v0, vres_up,
           k_k, k_a, r_k, lnx_g, lnx_b, sinks, rel_bias, w_out, norm_mlp_g, w_ff_up, w_ff_down,
           ple_proj, ple_norm_g, ple_gate, final_norm_g):
    B, S, _ = x.shape
    depth = w_in.shape[0]
    T = B * S
    n_chunks = SCAN_CHUNKS_PER_STEP
    row = lambda a: a.reshape(1, -1).astype(F32)

    bkt = jnp.asarray(_bucket_tile())
    consts = _scan_consts(n_chunks * CHUNK)

    x2 = x.reshape(T, D_MODEL)
    v_first = None
    for i in range(depth):
        w_cols = w_in[i]
        widths = [RWKV_COLS, SWA_WIDTH, 2 * SWA_KV_WIDTH]
        if i > 0:
            pad = jnp.zeros((D_MODEL, VRES_PAD - MV_LORA), w_cols.dtype)
            w_cols = jnp.concatenate([w_cols, vres_down[i - 1], pad], axis=1)
            widths.append(VRES_PAD)
        zs = _inproj(x2, row(norm_mix_g[i]), w_cols.astype(BF16), widths)
        zr = zs[0].reshape(B, S, RWKV_COLS)
        zq = zs[1].reshape(B, S, SWA_WIDTH)
        zkv = zs[2].reshape(B, S, 2 * SWA_KV_WIDTH)

        prm = {
            "mu": row(mu_shift[i]), "w0": row(w0[i]), "w_up": w_up[i].astype(BF16), "a0": row(a0[i]),
            "a_up": a_up[i].astype(BF16), "g_up": g_up[i].astype(BF16), "k_k": row(k_k[i]),
            "k_a": row(k_a[i]), "r_k": row(r_k[i]), "lnx_g": row(lnx_g[i]), "lnx_b": row(lnx_b[i]),
        }
        if i == 0:
            rwkv_out, v_first = _rwkv(False, zr, None, None, prm, consts, n_chunks)
        else:
            prm["mu_vres"] = jnp.pad(row(mu_vres[i - 1]), ((0, 0), (0, VRES_PAD - MV_LORA)))
            prm["v0"] = row(v0[i - 1])
            prm["vres_up"] = jnp.pad(vres_up[i - 1], ((0, VRES_PAD - MV_LORA), (0, 0))).astype(BF16)
            zv = zs[3].reshape(B, S, VRES_PAD)
            (rwkv_out,) = _rwkv(True, zr, zv, v_first, prm, consts, n_chunks)

        swa_out = _swa(zq, zkv, bkt, rel_bias.astype(F32), sinks[i].astype(F32))

        x2 = _post(i == depth - 1, x2, rwkv_out.reshape(T, RWKV_WIDTH), swa_out.reshape(T, SWA_WIDTH),
                   p[i].reshape(T, PLE_DIM), w_out[i].astype(BF16), row(norm_mlp_g[i]),
                   w_ff_up[i].astype(BF16), w_ff_down[i].astype(BF16), ple_proj[i].astype(BF16),
                   row(ple_norm_g[i]), ple_gate[i].astype(BF16), row(final_norm_g))
    return x2.reshape(B, S, D_MODEL)
```

```python
import functools
import math

import jax
import jax.numpy as jnp
import numpy as np
from jax import lax
from jax.experimental import pallas as pl
from jax.experimental.pallas import tpu as pltpu

F32 = jnp.float32
BF16 = jnp.bfloat16

D_MODEL = 1024
PLE_DIM = 256
HEADS = 8
HEAD_DIM = 64
RWKV_WIDTH = HEADS * HEAD_DIM
DECAY_LORA = 64
AAA_LORA = 64
MV_LORA = 32
GATE_LORA = 128
LNX_EPS = 64e-5
SWA_Q_HEADS = 8
SWA_KV_HEADS = 2
SWA_GROUP = SWA_Q_HEADS // SWA_KV_HEADS
SWA_WIDTH = SWA_Q_HEADS * HEAD_DIM
SWA_KV_WIDTH = SWA_KV_HEADS * HEAD_DIM
WINDOW = 128
BLOCK = 128
ATTN_SCALE = 1.0 / math.sqrt(HEAD_DIM)
REL_BUCKETS = 32
REL_MAX_DIST = 128
D_FF = 4 * D_MODEL
NORM_EPS = 1e-6
RWKV_COLS = 3 * RWKV_WIDTH + DECAY_LORA + AAA_LORA + GATE_LORA
SWA_COLS = SWA_WIDTH + 2 * SWA_KV_WIDTH
IN_COLS = RWKV_COLS + SWA_COLS

LANES = 128
VRES_PAD = LANES
CHUNK = 64
SCAN_CHUNKS_PER_STEP = 4
MASK_VALUE = -1e30

TM_PROJ = 512
TQ_SWA = 512
FF_CHUNK = 1024
VMEM_LIMIT = 56 * 1024 * 1024


def _rms(x, g):
    ms = jnp.mean(x * x, axis=-1, keepdims=True)
    return (x * lax.rsqrt(ms + NORM_EPS)) * g


def _bdot(a, b):
    return jnp.dot(a.astype(BF16), b.astype(BF16), preferred_element_type=F32)


def _split2(x):
    hi = x.astype(BF16)
    lo = (x - hi.astype(F32)).astype(BF16)
    return hi, lo


def _dot01_left(m01, x):
    hi, lo = _split2(x)
    return jnp.dot(m01, hi, preferred_element_type=F32) + jnp.dot(m01, lo, preferred_element_type=F32)


def _head_sums(x, seg):
    half = seg.shape[0]
    outs = []
    for c in range(x.shape[1] // half):
        hi, lo = _split2(x[:, c * half:(c + 1) * half])
        outs.append(jnp.dot(hi, seg, preferred_element_type=F32) + jnp.dot(lo, seg, preferred_element_type=F32))
    return jnp.concatenate(outs, axis=1)


def _inproj_kernel(x_ref, g_ref, w_ref, *out_refs):
    h = _rms(x_ref[...], g_ref[...]).astype(BF16)
    off = 0
    for o_ref in out_refs:
        n = o_ref.shape[-1]
        o_ref[...] = jnp.dot(h, w_ref[:, off:off + n], preferred_element_type=F32)
        off += n


def _inproj(x2, g, w_bf16, widths):
    T = x2.shape[0]
    nc = w_bf16.shape[1]
    assert sum(widths) == nc
    return pl.pallas_call(
        _inproj_kernel,
        grid=(T // TM_PROJ,),
        in_specs=[
            pl.BlockSpec((TM_PROJ, D_MODEL), lambda i: (i, 0)),
            pl.BlockSpec((1, D_MODEL), lambda i: (0, 0)),
            pl.BlockSpec((D_MODEL, nc), lambda i: (0, 0)),
        ],
        out_specs=[pl.BlockSpec((TM_PROJ, n), lambda i: (i, 0)) for n in widths],
        out_shape=[jax.ShapeDtypeStruct((T, n), F32) for n in widths],
        compiler_params=pltpu.CompilerParams(
            dimension_semantics=("arbitrary",), vmem_limit_bytes=VMEM_LIMIT),
        name="inproj",
    )(x2, g, w_bf16)


def _sigmoid(x):
    return 0.5 * jnp.tanh(0.5 * x) + 0.5


def _shift(z, prev_row, mu):
    rolled = pltpu.roll(z, 1, 0)
    row = lax.broadcasted_iota(jnp.int32, z.shape, 0)
    prev = jnp.where(row == 0, prev_row, rolled)
    return z + (prev - z) * mu


def _rwkv_kernel(has_vres, n_chunks, *refs):
    if has_vres:
        (zr_ref, zv_ref, vf_ref, mu_ref, w0_ref, wup_ref, a0_ref, aup_ref, gup_ref, kk_ref, ka_ref,
         rk_ref, lg_ref, lb_ref, muv_ref, v0_ref, vup_ref, seg_ref, tril_ref,
         out_ref, h_ref, prev_ref, prevv_ref) = refs
    else:
        (zr_ref, mu_ref, w0_ref, wup_ref, a0_ref, aup_ref, gup_ref, kk_ref, ka_ref,
         rk_ref, lg_ref, lb_ref, seg_ref, tril_ref,
         out_ref, vf_out_ref, h_ref, prev_ref) = refs

    c_idx = pl.program_id(1)

    @pl.when(c_idx == 0)
    def _():
        h_ref[...] = jnp.zeros_like(h_ref)
        prev_ref[...] = jnp.zeros_like(prev_ref)
        if has_vres:
            prevv_ref[...] = jnp.zeros_like(prevv_ref)

    W = RWKV_WIDTH
    z = zr_ref[...]
    tt = z.shape[0]
    zs = _shift(z, prev_ref[...], mu_ref[...])
    prev_ref[...] = z[tt - 1:tt, :]
    r = zs[:, 0:W]
    k = zs[:, W:2 * W]
    v = zs[:, 2 * W:3 * W]
    o1 = 3 * W
    xw = zs[:, o1:o1 + DECAY_LORA]
    xa = zs[:, o1 + DECAY_LORA:o1 + DECAY_LORA + AAA_LORA]
    xg = zs[:, o1 + DECAY_LORA + AAA_LORA:]

    ld = -math.exp(-0.5) * _sigmoid(w0_ref[...] + _bdot(jnp.tanh(xw), wup_ref[...]))
    a = _sigmoid(a0_ref[...] + _bdot(xa, aup_ref[...]))
    g = _bdot(_sigmoid(xg), gup_ref[...])
    if has_vres:
        zv = zv_ref[...]
        zvs = _shift(zv, prevv_ref[...], muv_ref[...])
        prevv_ref[...] = zv[tt - 1:tt, :]
        v = v + (vf_ref[...] - v) * _sigmoid(v0_ref[...] + _bdot(zvs, vup_ref[...]))
    else:
        vf_out_ref[...] = v

    seg = seg_ref[...]
    kk = k * kk_ref[...]
    kk = kk * jnp.minimum(lax.rsqrt(_head_sums(kk * kk, seg)), 1e12)
    kadj = k * (1.0 + (a - 1.0) * ka_ref[...])
    b = kk * a

    c = _dot01_left(tril_ref[...], ld)
    cl = jnp.concatenate(
        [jnp.broadcast_to(c[(ch + 1) * CHUNK - 1:(ch + 1) * CHUNK, :], (CHUNK, W)) for ch in range(n_chunks)],
        axis=0)
    qt = kk * jnp.exp(c - ld)
    rt = r * jnp.exp(c)
    en = jnp.exp(-c)
    kt = kadj * en
    bt = b * en
    el = jnp.exp(cl - c)
    kh = kadj * el
    bh = b * el
    pl_all = jnp.exp(cl)

    L = CHUNK
    P2 = 2 * HEAD_DIM
    lo = lax.broadcasted_iota(jnp.int32, (1, P2), 1) < HEAD_DIM
    ri = lax.broadcasted_iota(jnp.int32, (L, P2), 0)
    ci = lax.broadcasted_iota(jnp.int32, (L, P2), 1) % HEAD_DIM
    strict = ri > ci
    incl = ri >= ci
    eye2 = (ri == ci).astype(F32)
    units = [(ch, pr) for ch in range(n_chunks) for pr in range(HEADS // 2)]

    cat0 = lambda *xs: jnp.concatenate(xs, axis=0)
    cat1 = lambda *xs: jnp.concatenate(xs, axis=1)
    mm = lambda a_, b_: jnp.dot(a_, b_, preferred_element_type=F32)
    nt = lambda a_, b_: lax.dot_general(a_, b_, (((1,), (1,)), ((), ())), preferred_element_type=F32)
    tn = lambda a_, b_: lax.dot_general(a_, b_, (((0,), (0,)), ((), ())), preferred_element_type=F32)
    zero = jnp.zeros((), BF16)

    def bd(x_):
        x_ = x_.astype(BF16)
        return cat0(jnp.where(lo, x_, zero), jnp.where(lo, zero, x_))

    def diag_blocks(x_):
        sel = lo if x_.shape[1] == P2 else jnp.concatenate([lo] * (x_.shape[1] // P2), axis=1)
        return jnp.where(sel, x_[:HEAD_DIM], x_[HEAD_DIM:])

    def per_unit(arr, dtype=BF16):
        arr = arr.astype(dtype)
        return [arr[ch * L:(ch + 1) * L, pr * P2:(pr + 1) * P2] for ch, pr in units]

    qh, rh, kth, bth, vh = per_unit(qt), per_unit(rt), per_unit(kt), per_unit(bt), per_unit(v)
    khh, bhh = per_unit(kh), per_unit(bh)
    rf = per_unit(rt, F32)
    amat = [nt(cat0(q_, r_), cat0(bd(k_), bd(b_))) for q_, r_, k_, b_ in zip(qh, rh, kth, bth)]
    a_qk = [jnp.where(strict, m_[:L, :P2], 0.0).astype(BF16) for m_ in amat]
    a_qb = [jnp.where(strict, m_[:L, P2:], 0.0) for m_ in amat]
    a_rk = [jnp.where(incl, m_[L:, :P2], 0.0).astype(BF16) for m_ in amat]
    a_rb = [jnp.where(incl, m_[L:, P2:], 0.0).astype(BF16) for m_ in amat]
    tinv = [eye2 - m_ for m_ in a_qb]
    pw = [m_.astype(BF16) for m_ in a_qb]
    pw = [mm(p_, bd(p_)).astype(BF16) for p_ in pw]
    for _ in range(int(math.log2(L)) - 2):
        both = [mm(cat0(t_.astype(BF16), p_), bd(p_)) for t_, p_ in zip(tinv, pw)]
        tinv = [t_ + b_[:L] for t_, b_ in zip(tinv, both)]
        pw = [b_[L:].astype(BF16) for b_ in both]
    tinv = [t_ + mm(t_.astype(BF16), bd(p_)) for t_, p_ in zip(tinv, pw)]
    av = [mm(cat0(ak_, ar_), bd(v_)) for ak_, ar_, v_ in zip(a_qk, a_rk, vh)]
    x = [mm(t_.astype(BF16), cat1(bd(q_), bd(av_[:L]))).astype(BF16)
         for t_, q_, av_ in zip(tinv, qh, av)]
    ry = [cat1(r_, av_[L:]) - mm(ab_, cat1(bd(x_[:, :P2]), bd(x_[:, P2:])))
          for r_, av_, ab_, x_ in zip(rf, av, a_rb, x)]
    bx = [diag_blocks(tn(b_, x_)) for b_, x_ in zip(bhh, x)]
    kv = [diag_blocks(tn(k_, v_)) for k_, v_ in zip(khh, vh)]
    mmat = [(eye2 * pl_all[ch * L:ch * L + 1, pr * P2:(pr + 1) * P2] - bx_[:, :P2]).astype(BF16)
            for (ch, pr), bx_ in zip(units, bx)]
    nmat = [kv_ - bx_[:, P2:] for kv_, bx_ in zip(kv, bx)]

    n_pairs = HEADS // 2
    h_cur = [h_ref[:, pr * P2:(pr + 1) * P2] for pr in range(n_pairs)]
    y_rows = []
    for ch in range(n_chunks):
        y_pairs = []
        for pr in range(n_pairs):
            u = ch * n_pairs + pr
            both = mm(cat0(ry[u][:, :P2].astype(BF16), mmat[u]), bd(h_cur[pr]))
            y_pairs.append(both[:L] + ry[u][:, P2:])
            h_cur[pr] = both[L:] + nmat[u]
        y_rows.append(cat1(*y_pairs))
    h_ref[...] = cat1(*h_cur)
    y = y_rows[0] if n_chunks == 1 else cat0(*y_rows)

    inv_n = 1.0 / HEAD_DIM
    ym = _head_sums(y, seg) * inv_n
    yc = y - ym
    yv = _head_sums(yc * yc, seg) * inv_n
    y = yc * lax.rsqrt(yv + LNX_EPS)
    y = y * lg_ref[...] + lb_ref[...]
    y = y + _head_sums(r * kadj * rk_ref[...], seg) * v
    out_ref[...] = y * g


def _rwkv(has_vres, zr, zv, v_first, prm, consts, n_chunks):
    B, S, _ = zr.shape
    tt = n_chunks * CHUNK
    W = RWKV_WIDTH
    tile = lambda n: pl.BlockSpec((None, tt, n), lambda b, c: (b, c, 0))
    full = lambda a: pl.BlockSpec(a.shape, lambda b, c: (0,) * a.ndim)
    names = ["mu", "w0", "w_up", "a0", "a_up", "g_up", "k_k", "k_a", "r_k", "lnx_g", "lnx_b"]
    if has_vres:
        names += ["mu_vres", "v0", "vres_up"]
    params = [prm[n] for n in names] + list(consts)
    if has_vres:
        acts = [zr, zv, v_first]
        act_specs = [tile(RWKV_COLS), tile(VRES_PAD), tile(W)]
        out_shape = [jax.ShapeDtypeStruct((B, S, W), F32)]
        out_specs = [tile(W)]
        scratch = [pltpu.VMEM((HEAD_DIM, W), F32), pltpu.VMEM((1, RWKV_COLS), F32),
                   pltpu.VMEM((1, VRES_PAD), F32)]
    else:
        acts = [zr]
        act_specs = [tile(RWKV_COLS)]
        out_shape = [jax.ShapeDtypeStruct((B, S, W), F32)] * 2
        out_specs = [tile(W), tile(W)]
        scratch = [pltpu.VMEM((HEAD_DIM, W), F32), pltpu.VMEM((1, RWKV_COLS), F32)]
    return pl.pallas_call(
        functools.partial(_rwkv_kernel, has_vres, n_chunks),
        grid=(B, S // tt),
        in_specs=act_specs + [full(a) for a in params],
        out_specs=out_specs,
        out_shape=out_shape,
        scratch_shapes=scratch,
        compiler_params=pltpu.CompilerParams(
            dimension_semantics=("arbitrary", "arbitrary"), vmem_limit_bytes=VMEM_LIMIT),
        name="rwkv_vres" if has_vres else "rwkv",
    )(*acts, *params)


def _swa_kernel(n_blk, q_ref, kv_ref, kvp_ref, bkt_ref, relb_ref, sink_ref, o_ref, bias_ref):
    b_idx = pl.program_id(0)
    c_idx = pl.program_id(1)
    G2 = 2 * BLOCK

    @pl.when((b_idx == 0) & (c_idx == 0))
    def _build_bias():
        bkt = bkt_ref[...]
        for h in range(SWA_Q_HEADS):
            acc = jnp.full(bkt.shape, MASK_VALUE, F32)
            for k in range(REL_BUCKETS):
                acc = jnp.where(bkt == k, relb_ref[k, h], acc)
            pair, half = divmod(h, 2)
            bias_ref[pair * BLOCK:(pair + 1) * BLOCK, half * G2:(half + 1) * G2] = acc

    lo = lax.broadcasted_iota(jnp.int32, (1, LANES), 1) < HEAD_DIM
    ones_lo = jnp.broadcast_to(lo.astype(BF16), (G2, LANES))
    ones_hi = jnp.broadcast_to((~lo).astype(BF16), (G2, LANES))
    top = lax.broadcasted_iota(jnp.int32, (2 * BLOCK, 1), 0) < BLOCK
    prev_col = (lax.broadcasted_iota(jnp.int32, (2 * BLOCK, 2 * G2), 1) % G2) < BLOCK
    units = [(j, g) for j in range(n_blk) for g in range(SWA_KV_HEADS)]

    kk, vv = {}, {}
    for j in range(n_blk):
        if j == 0:
            kv = jnp.concatenate([kvp_ref[...], kv_ref[0:BLOCK, :]], axis=0)
        else:
            kv = kv_ref[(j - 1) * BLOCK:(j + 1) * BLOCK, :]
        kcol, vcol = kv[:, :LANES], kv[:, LANES:]
        for arr, dst, extra in ((kcol, kk, None), (vcol, vv, (ones_lo, ones_hi))):
            g0a = jnp.where(lo, arr, 0.0)
            g1b = jnp.where(lo, 0.0, arr)
            g0b = pltpu.roll(g0a, HEAD_DIM, 1)
            g1a = pltpu.roll(g1b, HEAD_DIM, 1)
            for g, (xa, xb) in enumerate(((g0a, g0b), (g1a, g1b))):
                xa, xb = xa.astype(BF16), xb.astype(BF16)
                if extra is None:
                    dst[j, g] = jnp.concatenate([xa, xb], axis=0)
                else:
                    dst[j, g] = jnp.concatenate(
                        [jnp.concatenate([xa, extra[0]], axis=1),
                         jnp.concatenate([xb, extra[1]], axis=1)], axis=0)

    def q_pairs(j, g):
        q2 = (q_ref[j * BLOCK:(j + 1) * BLOCK, g * 2 * LANES:(g + 1) * 2 * LANES] * ATTN_SCALE).astype(BF16)
        return jnp.concatenate([q2[:, :LANES], q2[:, LANES:]], axis=0)

    logits = [lax.dot_general(q_pairs(j, g), kk[j, g], (((1,), (1,)), ((), ())), preferred_element_type=F32)
              + bias_ref[g * 2 * BLOCK:(g + 1) * 2 * BLOCK, :] for j, g in units]
    logits = [jnp.where(prev_col & (c_idx == 0), MASK_VALUE, l_) if j == 0 else l_
              for (j, g), l_ in zip(units, logits)]
    sinks = [[jnp.where(top, sink_ref[4 * g + half], sink_ref[4 * g + 2 + half]) for half in range(2)]
             for j, g in units]
    ms = [[jnp.maximum(jnp.max(l_[:, half * G2:(half + 1) * G2], axis=-1, keepdims=True), s_[half])
           for half in range(2)] for l_, s_ in zip(logits, sinks)]
    es = [jnp.concatenate([jnp.exp(l_[:, half * G2:(half + 1) * G2] - m_[half]) for half in range(2)],
                          axis=1).astype(BF16) for l_, m_ in zip(logits, ms)]
    res = [jnp.dot(e_, vv[u], preferred_element_type=F32) for e_, u in zip(es, units)]
    for (j, g), r_, m_, s_ in zip(units, res, ms, sinks):
        den = r_[:, LANES:] + jnp.where(lo, jnp.exp(s_[0] - m_[0]), jnp.exp(s_[1] - m_[1]))
        out = r_[:, :LANES] / den
        rows = slice(j * BLOCK, (j + 1) * BLOCK)
        o_ref[rows, (2 * g) * LANES:(2 * g + 1) * LANES] = out[:BLOCK]
        o_ref[rows, (2 * g + 1) * LANES:(2 * g + 2) * LANES] = out[BLOCK:]


def _swa(zq, zkv, bkt, rel_bias, sinks):
    B, S, _ = zq.shape
    n_blk = TQ_SWA // BLOCK
    smem = lambda a: pl.BlockSpec(a.shape, lambda b, c: (0,) * a.ndim, memory_space=pltpu.SMEM)
    return pl.pallas_call(
        functools.partial(_swa_kernel, n_blk),
        grid=(B, S // TQ_SWA),
        in_specs=[
            pl.BlockSpec((None, TQ_SWA, SWA_WIDTH), lambda b, c: (b, c, 0)),
            pl.BlockSpec((None, TQ_SWA, 2 * SWA_KV_WIDTH), lambda b, c: (b, c, 0)),
            pl.BlockSpec((None, BLOCK, 2 * SWA_KV_WIDTH),
                         lambda b, c: (b, jnp.maximum(c * n_blk - 1, 0), 0)),
            pl.BlockSpec(bkt.shape, lambda b, c: (0, 0)),
            smem(rel_bias),
            smem(sinks),
        ],
        out_specs=pl.BlockSpec((None, TQ_SWA, SWA_WIDTH), lambda b, c: (b, c, 0)),
        out_shape=jax.ShapeDtypeStruct((B, S, SWA_WIDTH), F32),
        scratch_shapes=[pltpu.VMEM((SWA_Q_HEADS // 2 * BLOCK, 2 * 2 * BLOCK), F32)],
        compiler_params=pltpu.CompilerParams(
            dimension_semantics=("arbitrary", "arbitrary"), vmem_limit_bytes=VMEM_LIMIT),
        name="swa",
    )(zq, zkv, zkv, bkt, rel_bias, sinks)


def _post_kernel(final, x_ref, ro_ref, so_ref, p_ref, wo_ref, gm_ref, up_ref, dn_ref, pp_ref, pg_ref,
                 gate_ref, fg_ref, o_ref):
    W = RWKV_WIDTH
    x = x_ref[...]
    x = x + _bdot(ro_ref[...], wo_ref[0:W, :]) + _bdot(so_ref[...], wo_ref[W:2 * W, :])
    u = _rms(x, gm_ref[...]).astype(BF16)
    acc = jnp.zeros_like(x)
    for c in range(D_FF // FF_CHUNK):
        cs = slice(c * FF_CHUNK, (c + 1) * FF_CHUNK)
        hid = jnp.dot(u, up_ref[:, cs], preferred_element_type=F32)
        hid = jnp.square(jnp.maximum(hid, 0.0))
        acc = acc + jnp.dot(hid.astype(BF16), dn_ref[cs, :], preferred_element_type=F32)
    x = x + acc
    e = _rms(_bdot(p_ref[...], pp_ref[...]), pg_ref[...])
    x = x + e * _sigmoid(_bdot(x, gate_ref[...]))
    if final:
        x = _rms(x, fg_ref[...])
    o_ref[...] = x


def _post(final, x2, ro, so, p2, wo, gm, up, dn, pp, pg, gate, fg):
    T = x2.shape[0]
    tile = lambda n: pl.BlockSpec((TM_PROJ, n), lambda i: (i, 0))
    res = lambda a: pl.BlockSpec(a.shape, lambda i: (0, 0), pipeline_mode=pl.Buffered(1))
    weights = [wo, gm, up, dn, pp, pg, gate, fg]
    return pl.pallas_call(
        functools.partial(_post_kernel, final),
        grid=(T // TM_PROJ,),
        in_specs=[tile(D_MODEL), tile(RWKV_WIDTH), tile(SWA_WIDTH), tile(PLE_DIM)] + [res(a) for a in weights],
        out_specs=tile(D_MODEL),
        out_shape=jax.ShapeDtypeStruct((T, D_MODEL), F32),
        compiler_params=pltpu.CompilerParams(
            dimension_semantics=("arbitrary",), vmem_limit_bytes=VMEM_LIMIT),
        name="post_final" if final else "post",
    )(x2, ro, so, p2, *weights)


def _bucket_tile():
    max_exact = REL_BUCKETS // 2
    dist = (np.arange(BLOCK)[:, None] + BLOCK) - np.arange(2 * BLOCK)[None, :]
    n = np.maximum(dist, 0)
    nf = np.maximum(n, 1).astype(np.float32)
    scaled = (np.log(nf / np.float32(max_exact)) / np.float32(math.log(REL_MAX_DIST / max_exact))
              * np.float32(REL_BUCKETS - max_exact)).astype(np.float32)
    large = np.minimum(max_exact + scaled.astype(np.int32), REL_BUCKETS - 1)
    bucket = np.where(n < max_exact, n, large)
    valid = (dist >= 0) & (dist < WINDOW)
    return np.where(valid, bucket, -1).astype(np.int32)


def _scan_consts(tt):
    t = np.arange(tt)
    same = (t[:, None] // CHUNK) == (t[None, :] // CHUNK)
    tril = same & (t[:, None] >= t[None, :])
    hh = np.arange(4 * HEAD_DIM) // HEAD_DIM
    seg = hh[:, None] == hh[None, :]
    return (jnp.asarray(seg, BF16), jnp.asarray(tril, BF16))


def kernel(x, p, norm_mix_g, w_in, mu_shift, w0, w_up, a0, a_up, g_up, vres_down, mu_vres, v0, vres_up,
           k_k, k_a, r_k, lnx_g, lnx_b, sinks, rel_bias, w_out, norm_mlp_g, w_ff_up, w_ff_down,
           ple_proj, ple_norm_g, ple_gate, final_norm_g):
    B, S, _ = x.shape
    depth = w_in.shape[0]
    T = B * S
    n_chunks = SCAN_CHUNKS_PER_STEP
    row = lambda a: a.reshape(1, -1).astype(F32)

    bkt = jnp.asarray(_bucket_tile())
    consts = _scan_consts(n_chunks * CHUNK)

    x2 = x.reshape(T, D_MODEL)
    v_first = None
    for i in range(depth):
        w_cols = w_in[i]
        widths = [RWKV_COLS, SWA_WIDTH, 2 * SWA_KV_WIDTH]
        if i > 0:
            pad = jnp.zeros((D_MODEL, VRES_PAD - MV_LORA), w_cols.dtype)
            w_cols = jnp.concatenate([w_cols, vres_down[i - 1], pad], axis=1)
            widths.append(VRES_PAD)
        zs = _inproj(x2, row(norm_mix_g[i]), w_cols.astype(BF16), widths)
        zr = zs[0].reshape(B, S, RWKV_COLS)
        zq = zs[1].reshape(B, S, SWA_WIDTH)
        zkv = zs[2].reshape(B, S, 2 * SWA_KV_WIDTH)

        prm = {
            "mu": row(mu_shift[i]), "w0": row(w0[i]), "w_up": w_up[i].astype(BF16), "a0": row(a0[i]),
            "a_up": a_up[i].astype(BF16), "g_up": g_up[i].astype(BF16), "k_k": row(k_k[i]),
            "k_a": row(k_a[i]), "r_k": row(r_k[i]), "lnx_g": row(lnx_g[i]), "lnx_b": row(lnx_b[i]),
        }
        if i == 0:
            rwkv_out, v_first = _rwkv(False, zr, None, None, prm, consts, n_chunks)
        else:
            prm["mu_vres"] = jnp.pad(row(mu_vres[i - 1]), ((0, 0), (0, VRES_PAD - MV_LORA)))
            prm["v0"] = row(v0[i - 1])
            prm["vres_up"] = jnp.pad(vres_up[i - 1], ((0, VRES_PAD - MV_LORA), (0, 0))).astype(BF16)
            zv = zs[3].reshape(B, S, VRES_PAD)
            (rwkv_out,) = _rwkv(True, zr, zv, v_first, prm, consts, n_chunks)

        swa_out = _swa(zq, zkv, bkt, rel_bias.astype(F32), sinks[i].astype(F32))

        x2 = _post(i == depth - 1, x2, rwkv_out.reshape(T, RWKV_WIDTH), swa_out.reshape(T, SWA_WIDTH),
                   p[i].reshape(T, PLE_DIM), w_out[i].astype(BF16), row(norm_mlp_g[i]),
                   w_ff_up[i].astype(BF16), w_ff_down[i].astype(BF16), ple_proj[i].astype(BF16),
                   row(ple_norm_g[i]), ple_gate[i].astype(BF16), row(final_norm_g))
    return x2.reshape(B, S, D_MODEL)
```

```python
import functools
import math

import jax
import jax.numpy as jnp
import numpy as np
from jax import lax
from jax.experimental import pallas as pl
from jax.experimental.pallas import tpu as pltpu

F32 = jnp.float32
BF16 = jnp.bfloat16

D_MODEL = 1024
PLE_DIM = 256
HEADS = 8
HEAD_DIM = 64
RWKV_WIDTH = HEADS * HEAD_DIM
DECAY_LORA = 64
AAA_LORA = 64
MV_LORA = 32
GATE_LORA = 128
LNX_EPS = 64e-5
SWA_Q_HEADS = 8
SWA_KV_HEADS = 2
SWA_GROUP = SWA_Q_HEADS // SWA_KV_HEADS
SWA_WIDTH = SWA_Q_HEADS * HEAD_DIM
SWA_KV_WIDTH = SWA_KV_HEADS * HEAD_DIM
WINDOW = 128
BLOCK = 128
ATTN_SCALE = 1.0 / math.sqrt(HEAD_DIM)
REL_BUCKETS = 32
REL_MAX_DIST = 128
D_FF = 4 * D_MODEL
NORM_EPS = 1e-6
RWKV_COLS = 3 * RWKV_WIDTH + DECAY_LORA + AAA_LORA + GATE_LORA
SWA_COLS = SWA_WIDTH + 2 * SWA_KV_WIDTH
IN_COLS = RWKV_COLS + SWA_COLS

LANES = 128
VRES_PAD = LANES
CHUNK = 64
SCAN_CHUNKS_PER_STEP = 4
MASK_VALUE = -1e30

TM_PROJ = 512
TQ_SWA = 512
FF_CHUNK = 1024
VMEM_LIMIT = 56 * 1024 * 1024


def _rms(x, g):
    ms = jnp.mean(x * x, axis=-1, keepdims=True)
    return (x * lax.rsqrt(ms + NORM_EPS)) * g


def _bdot(a, b):
    return jnp.dot(a.astype(BF16), b.astype(BF16), preferred_element_type=F32)


def _split2(x):
    hi = x.astype(BF16)
    lo = (x - hi.astype(F32)).astype(BF16)
    return hi, lo


def _dot01_left(m01x2, x):
    hi, lo = _split2(x)
    return jnp.dot(m01x2, jnp.concatenate([hi, lo], axis=0), preferred_element_type=F32)


def _head_sums(x, segx2):
    half = segx2.shape[1]
    outs = []
    for c in range(x.shape[1] // half):
        hi, lo = _split2(x[:, c * half:(c + 1) * half])
        outs.append(jnp.dot(jnp.concatenate([hi, lo], axis=1), segx2, preferred_element_type=F32))
    return jnp.concatenate(outs, axis=1)


def _inproj_kernel(x_ref, g_ref, w_ref, *out_refs):
    h = _rms(x_ref[...], g_ref[...]).astype(BF16)
    off = 0
    for o_ref in out_refs:
        n = o_ref.shape[-1]
        o_ref[...] = jnp.dot(h, w_ref[:, off:off + n], preferred_element_type=F32)
        off += n


def _inproj(x2, g, w_bf16, widths):
    T = x2.shape[0]
    nc = w_bf16.shape[1]
    assert sum(widths) == nc
    return pl.pallas_call(
        _inproj_kernel,
        grid=(T // TM_PROJ,),
        in_specs=[
            pl.BlockSpec((TM_PROJ, D_MODEL), lambda i: (i, 0)),
            pl.BlockSpec((1, D_MODEL), lambda i: (0, 0)),
            pl.BlockSpec((D_MODEL, nc), lambda i: (0, 0)),
        ],
        out_specs=[pl.BlockSpec((TM_PROJ, n), lambda i: (i, 0)) for n in widths],
        out_shape=[jax.ShapeDtypeStruct((T, n), F32) for n in widths],
        compiler_params=pltpu.CompilerParams(
            dimension_semantics=("arbitrary",), vmem_limit_bytes=VMEM_LIMIT),
        name="inproj",
    )(x2, g, w_bf16)


def _sigmoid(x):
    return 0.5 * jnp.tanh(0.5 * x) + 0.5


def _shift(z, prev_row, mu):
    rolled = pltpu.roll(z, 1, 0)
    row = lax.broadcasted_iota(jnp.int32, z.shape, 0)
    prev = jnp.where(row == 0, prev_row, rolled)
    return z + (prev - z) * mu


def _rwkv_kernel(has_vres, n_chunks, tiles_per_seq, *refs):
    if has_vres:
        (zr_ref, zv_ref, vf_ref, mu_ref, w0_ref, wup_ref, a0_ref, aup_ref, gup_ref, kk_ref, ka_ref,
         rk_ref, lg_ref, lb_ref, muv_ref, v0_ref, vup_ref, seg_ref, tril_ref,
         out_ref, h_ref, prev_ref, ops0_ref, ops1_ref, aux0_ref, aux1_ref, pl0_ref, pl1_ref, prevv_ref) = refs
    else:
        (zr_ref, mu_ref, w0_ref, wup_ref, a0_ref, aup_ref, gup_ref, kk_ref, ka_ref,
         rk_ref, lg_ref, lb_ref, seg_ref, tril_ref,
         out_ref, vf_out_ref, h_ref, prev_ref, ops0_ref, ops1_ref, aux0_ref, aux1_ref, pl0_ref, pl1_ref) = refs

    i = pl.program_id(0)
    n_tiles = pl.num_programs(0) - 1
    first_p = lax.rem(jnp.minimum(i, n_tiles - 1), tiles_per_seq) == 0
    first_s = lax.rem(i + tiles_per_seq - 1, tiles_per_seq) == 0

    @pl.when(i == 0)
    def _():
        h_ref[...] = jnp.zeros_like(h_ref)
        prev_ref[...] = jnp.zeros_like(prev_ref)
        ops1_ref[...] = jnp.zeros_like(ops1_ref)
        aux1_ref[...] = jnp.zeros_like(aux1_ref)
        pl1_ref[...] = jnp.zeros_like(pl1_ref)
        if has_vres:
            prevv_ref[...] = jnp.zeros_like(prevv_ref)

    W = RWKV_WIDTH
    L = CHUNK
    P2 = 2 * HEAD_DIM
    tt = n_chunks * L
    Q_, R_, KT_, BT_, KH_, BH_, V_ = range(7)
    AV_, AG_, AB_ = range(3)

    def prologue(ops_ref, aux_ref, pl_ref):
        z = zr_ref[...]
        before = jnp.where(i == n_tiles, prev_ref[1:2, :], prev_ref[0:1, :])
        zs = _shift(z, jnp.where(first_p, 0.0, before), mu_ref[...])
        prev_ref[1:2, :] = prev_ref[0:1, :]
        prev_ref[0:1, :] = z[tt - 1:tt, :]
        yield
        r = zs[:, 0:W]
        k = zs[:, W:2 * W]
        v = zs[:, 2 * W:3 * W]
        o1 = 3 * W
        xw = zs[:, o1:o1 + DECAY_LORA]
        xa = zs[:, o1 + DECAY_LORA:o1 + DECAY_LORA + AAA_LORA]
        xg = zs[:, o1 + DECAY_LORA + AAA_LORA:]
        ld = -math.exp(-0.5) * _sigmoid(w0_ref[...] + _bdot(jnp.tanh(xw), wup_ref[...]))
        yield
        a = _sigmoid(a0_ref[...] + _bdot(xa, aup_ref[...]))
        yield
        aux_ref[AG_] = _bdot(_sigmoid(xg), gup_ref[...])
        yield
        if has_vres:
            zv = zv_ref[...]
            zvs = _shift(zv, jnp.where(first_p, 0.0, prevv_ref[...]), muv_ref[...])
            prevv_ref[...] = zv[tt - 1:tt, :]
            v = v + (vf_ref[...] - v) * _sigmoid(v0_ref[...] + _bdot(zvs, vup_ref[...]))
        else:
            vf_out_ref[...] = v
        aux_ref[AV_] = v
        ops_ref[V_] = v.astype(BF16)
        yield
        seg = seg_ref[...]
        kk = k * kk_ref[...]
        kk = kk * jnp.minimum(lax.rsqrt(_head_sums(kk * kk, seg)), 1e12)
        yield
        kadj = k * (1.0 + (a - 1.0) * ka_ref[...])
        b = kk * a
        aux_ref[AB_] = _head_sums(r * kadj * rk_ref[...], seg)
        yield
        c = _dot01_left(tril_ref[...], ld)
        cl = jnp.concatenate(
            [jnp.broadcast_to(c[(ch + 1) * L - 1:(ch + 1) * L, :], (L, W)) for ch in range(n_chunks)], axis=0)
        yield
        ops_ref[Q_] = (kk * jnp.exp(c - ld)).astype(BF16)
        yield
        ops_ref[R_] = (r * jnp.exp(c)).astype(BF16)
        yield
        en = jnp.exp(-c)
        ops_ref[KT_] = (kadj * en).astype(BF16)
        ops_ref[BT_] = (b * en).astype(BF16)
        yield
        el = jnp.exp(cl - c)
        ops_ref[KH_] = (kadj * el).astype(BF16)
        ops_ref[BH_] = (b * el).astype(BF16)
        yield
        for ch in range(n_chunks):
            pl_ref[ch:ch + 1, :] = jnp.exp(c[(ch + 1) * L - 1:(ch + 1) * L, :])
        yield

    lo = lax.broadcasted_iota(jnp.int32, (1, P2), 1) < HEAD_DIM
    ri = lax.broadcasted_iota(jnp.int32, (L, P2), 0)
    ci = lax.broadcasted_iota(jnp.int32, (L, P2), 1) % HEAD_DIM
    strict = ri > ci
    incl = ri >= ci
    eye2 = (ri == ci).astype(F32)
    n_pairs = HEADS // 2
    units = [(ch, pr) for ch in range(n_chunks) for pr in range(n_pairs)]

    cat0 = lambda *xs: jnp.concatenate(xs, axis=0)
    cat1 = lambda *xs: jnp.concatenate(xs, axis=1)
    mm = lambda a_, b_: jnp.dot(a_, b_, preferred_element_type=F32)
    nt = lambda a_, b_: lax.dot_general(a_, b_, (((1,), (1,)), ((), ())), preferred_element_type=F32)
    tn = lambda a_, b_: lax.dot_general(a_, b_, (((0,), (0,)), ((), ())), preferred_element_type=F32)
    zero = jnp.zeros((), BF16)

    def bd(x_):
        x_ = x_.astype(BF16)
        return cat0(jnp.where(lo, x_, zero), jnp.where(lo, zero, x_))

    def diag_blocks(x_):
        sel = lo if x_.shape[1] == P2 else jnp.concatenate([lo] * (x_.shape[1] // P2), axis=1)
        return jnp.where(sel, x_[:HEAD_DIM], x_[HEAD_DIM:])

    def scan(ops_ref, aux_ref, pl_ref):
        def operand(row):
            return [ops_ref[row, ch * L:(ch + 1) * L, pr * P2:(pr + 1) * P2] for ch, pr in units]

        qh, rh, kth, bth = operand(Q_), operand(R_), operand(KT_), operand(BT_)
        amat = [nt(cat0(q_, r_), cat0(bd(k_), bd(b_))) for q_, r_, k_, b_ in zip(qh, rh, kth, bth)]
        yield
        a_qk = [jnp.where(strict, m_[:L, :P2], 0.0).astype(BF16) for m_ in amat]
        a_qb = [jnp.where(strict, m_[:L, P2:], 0.0) for m_ in amat]
        a_rk = [jnp.where(incl, m_[L:, :P2], 0.0).astype(BF16) for m_ in amat]
        a_rb = [jnp.where(incl, m_[L:, P2:], 0.0).astype(BF16) for m_ in amat]
        tinv = [eye2 - m_ for m_ in a_qb]
        pw = [m_.astype(BF16) for m_ in a_qb]
        pw = [mm(p_, bd(p_)).astype(BF16) for p_ in pw]
        yield
        for _ in range(int(math.log2(L)) - 2):
            both = [mm(cat0(t_.astype(BF16), p_), bd(p_)) for t_, p_ in zip(tinv, pw)]
            tinv = [t_ + b_[:L] for t_, b_ in zip(tinv, both)]
            pw = [b_[L:].astype(BF16) for b_ in both]
            yield
        tinv = [t_ + mm(t_.astype(BF16), bd(p_)) for t_, p_ in zip(tinv, pw)]
        vh = operand(V_)
        av = [mm(cat0(ak_, ar_), bd(v_)) for ak_, ar_, v_ in zip(a_qk, a_rk, vh)]
        yield
        x = [mm(t_.astype(BF16), cat1(bd(q_), bd(av_[:L]))).astype(BF16)
             for t_, q_, av_ in zip(tinv, qh, av)]
        yield
        ry = [cat1(r_.astype(F32), av_[L:]) - mm(ab_, cat1(bd(x_[:, :P2]), bd(x_[:, P2:])))
              for r_, av_, ab_, x_ in zip(rh, av, a_rb, x)]
        yield
        bx = [diag_blocks(tn(b_, x_)) for b_, x_ in zip(operand(BH_), x)]
        yield
        kv = [diag_blocks(tn(k_, v_)) for k_, v_ in zip(operand(KH_), vh)]
        mmat = [(eye2 * pl_ref[ch:ch + 1, pr * P2:(pr + 1) * P2] - bx_[:, :P2]).astype(BF16)
                for (ch, pr), bx_ in zip(units, bx)]
        nmat = [kv_ - bx_[:, P2:] for kv_, bx_ in zip(kv, bx)]
        yield
        h_cur = [jnp.where(first_s, 0.0, h_ref[:, pr * P2:(pr + 1) * P2]) for pr in range(n_pairs)]
        y_rows = []
        for ch in range(n_chunks):
            y_pairs = []
            for pr in range(n_pairs):
                u = ch * n_pairs + pr
                both = mm(cat0(ry[u][:, :P2].astype(BF16), mmat[u]), bd(h_cur[pr]))
                y_pairs.append(both[:L] + ry[u][:, P2:])
                h_cur[pr] = both[L:] + nmat[u]
            y_rows.append(cat1(*y_pairs))
            yield
        h_ref[...] = cat1(*h_cur)
        y = y_rows[0] if n_chunks == 1 else cat0(*y_rows)
        seg = seg_ref[...]
        inv_n = 1.0 / HEAD_DIM
        ym = _head_sums(y, seg) * inv_n
        yield
        yc = y - ym
        yv = _head_sums(yc * yc, seg) * inv_n
        yield
        y = yc * lax.rsqrt(yv + LNX_EPS)
        y = y * lg_ref[...] + lb_ref[...]
        y = y + aux_ref[AB_] * aux_ref[AV_]
        out_ref[...] = y * aux_ref[AG_]
        yield

    def step(fill, use):
        streams = [scan(*use), prologue(*fill)]
        while streams:
            for s_ in list(streams):
                try:
                    next(s_)
                except StopIteration:
                    streams.remove(s_)

    set0 = (ops0_ref, aux0_ref, pl0_ref)
    set1 = (ops1_ref, aux1_ref, pl1_ref)
    parity = lax.rem(i, 2)
    pl.when(parity == 0)(lambda: step(set0, set1))
    pl.when(parity == 1)(lambda: step(set1, set0))


def _rwkv(has_vres, zr, zv, v_first, prm, consts, n_chunks, tiles_per_seq):
    T = zr.shape[0]
    tt = n_chunks * CHUNK
    n_tiles = T // tt
    W = RWKV_WIDTH
    cur = lambda n: pl.BlockSpec((tt, n), lambda i: (jnp.minimum(i, n_tiles - 1), 0))
    prv = lambda n: pl.BlockSpec((tt, n), lambda i: (jnp.maximum(i - 1, 0), 0))
    full = lambda a: pl.BlockSpec(a.shape, lambda i: (0,) * a.ndim)
    names = ["mu", "w0", "w_up", "a0", "a_up", "g_up", "k_k", "k_a", "r_k", "lnx_g", "lnx_b"]
    if has_vres:
        names += ["mu_vres", "v0", "vres_up"]
    params = [prm[n] for n in names] + list(consts)
    scratch = [pltpu.VMEM((HEAD_DIM, W), F32), pltpu.VMEM((2, RWKV_COLS), F32),
               pltpu.VMEM((7, tt, W), BF16), pltpu.VMEM((7, tt, W), BF16),
               pltpu.VMEM((3, tt, W), F32), pltpu.VMEM((3, tt, W), F32),
               pltpu.VMEM((max(n_chunks, 8), W), F32), pltpu.VMEM((max(n_chunks, 8), W), F32)]
    if has_vres:
        acts = [zr, zv, v_first]
        act_specs = [cur(RWKV_COLS), cur(VRES_PAD), cur(W)]
        out_shape = [jax.ShapeDtypeStruct((T, W), F32)]
        out_specs = [prv(W)]
        scratch.append(pltpu.VMEM((1, VRES_PAD), F32))
    else:
        acts = [zr]
        act_specs = [cur(RWKV_COLS)]
        out_shape = [jax.ShapeDtypeStruct((T, W), F32)] * 2
        out_specs = [prv(W), cur(W)]
    return pl.pallas_call(
        functools.partial(_rwkv_kernel, has_vres, n_chunks, tiles_per_seq),
        grid=(n_tiles + 1,),
        in_specs=act_specs + [full(a) for a in params],
        out_specs=out_specs,
        out_shape=out_shape,
        scratch_shapes=scratch,
        compiler_params=pltpu.CompilerParams(
            dimension_semantics=("arbitrary",), vmem_limit_bytes=VMEM_LIMIT),
        name="rwkv_vres" if has_vres else "rwkv",
    )(*acts, *params)


def _swa_kernel(n_blk, q_ref, kv_ref, kvp_ref, bkt_ref, relb_ref, sink_ref, o_ref, bias_ref):
    b_idx = pl.program_id(0)
    c_idx = pl.program_id(1)
    G2 = 2 * BLOCK

    @pl.when((b_idx == 0) & (c_idx == 0))
    def _build_bias():
        bkt = bkt_ref[...]
        for h in range(SWA_Q_HEADS):
            acc = jnp.full(bkt.shape, MASK_VALUE, F32)
            for k in range(REL_BUCKETS):
                acc = jnp.where(bkt == k, relb_ref[k, h], acc)
            pair, half = divmod(h, 2)
            bias_ref[pair * BLOCK:(pair + 1) * BLOCK, half * G2:(half + 1) * G2] = acc

    lo = lax.broadcasted_iota(jnp.int32, (1, LANES), 1) < HEAD_DIM
    ones_lo = jnp.broadcast_to(lo.astype(BF16), (G2, LANES))
    ones_hi = jnp.broadcast_to((~lo).astype(BF16), (G2, LANES))
    top = lax.broadcasted_iota(jnp.int32, (2 * BLOCK, 1), 0) < BLOCK
    prev_col = (lax.broadcasted_iota(jnp.int32, (2 * BLOCK, 2 * G2), 1) % G2) < BLOCK
    units = [(j, g) for j in range(n_blk) for g in range(SWA_KV_HEADS)]

    kk, vv = {}, {}
    for j in range(n_blk):
        if j == 0:
            kv = jnp.concatenate([kvp_ref[...], kv_ref[0:BLOCK, :]], axis=0)
        else:
            kv = kv_ref[(j - 1) * BLOCK:(j + 1) * BLOCK, :]
        kcol, vcol = kv[:, :LANES], kv[:, LANES:]
        for arr, dst, extra in ((kcol, kk, None), (vcol, vv, (ones_lo, ones_hi))):
            g0a = jnp.where(lo, arr, 0.0)
            g1b = jnp.where(lo, 0.0, arr)
            g0b = pltpu.roll(g0a, HEAD_DIM, 1)
            g1a = pltpu.roll(g1b, HEAD_DIM, 1)
            for g, (xa, xb) in enumerate(((g0a, g0b), (g1a, g1b))):
                xa, xb = xa.astype(BF16), xb.astype(BF16)
                if extra is None:
                    dst[j, g] = jnp.concatenate([xa, xb], axis=0)
                else:
                    dst[j, g] = jnp.concatenate(
                        [jnp.concatenate([xa, extra[0]], axis=1),
                         jnp.concatenate([xb, extra[1]], axis=1)], axis=0)

    def q_pairs(j, g):
        q2 = (q_ref[j * BLOCK:(j + 1) * BLOCK, g * 2 * LANES:(g + 1) * 2 * LANES] * ATTN_SCALE).astype(BF16)
        return jnp.concatenate([q2[:, :LANES], q2[:, LANES:]], axis=0)

    logits = [lax.dot_general(q_pairs(j, g), kk[j, g], (((1,), (1,)), ((), ())), preferred_element_type=F32)
              + bias_ref[g * 2 * BLOCK:(g + 1) * 2 * BLOCK, :] for j, g in units]
    logits = [jnp.where(prev_col & (c_idx == 0), MASK_VALUE, l_) if j == 0 else l_
              for (j, g), l_ in zip(units, logits)]
    sinks = [[jnp.where(top, sink_ref[4 * g + half], sink_ref[4 * g + 2 + half]) for half in range(2)]
             for j, g in units]
    ms = [[jnp.maximum(jnp.max(l_[:, half * G2:(half + 1) * G2], axis=-1, keepdims=True), s_[half])
           for half in range(2)] for l_, s_ in zip(logits, sinks)]
    es = [jnp.concatenate([jnp.exp(l_[:, half * G2:(half + 1) * G2] - m_[half]) for half in range(2)],
                          axis=1).astype(BF16) for l_, m_ in zip(logits, ms)]
    res = [jnp.dot(e_, vv[u], preferred_element_type=F32) for e_, u in zip(es, units)]
    for (j, g), r_, m_, s_ in zip(units, res, ms, sinks):
        den = r_[:, LANES:] + jnp.where(lo, jnp.exp(s_[0] - m_[0]), jnp.exp(s_[1] - m_[1]))
        out = r_[:, :LANES] / den
        rows = slice(j * BLOCK, (j + 1) * BLOCK)
        o_ref[rows, (2 * g) * LANES:(2 * g + 1) * LANES] = out[:BLOCK]
        o_ref[rows, (2 * g + 1) * LANES:(2 * g + 2) * LANES] = out[BLOCK:]


def _swa(zq, zkv, bkt, rel_bias, sinks):
    B, S, _ = zq.shape
    n_blk = TQ_SWA // BLOCK
    smem = lambda a: pl.BlockSpec(a.shape, lambda b, c: (0,) * a.ndim, memory_space=pltpu.SMEM)
    return pl.pallas_call(
        functools.partial(_swa_kernel, n_blk),
        grid=(B, S // TQ_SWA),
        in_specs=[
            pl.BlockSpec((None, TQ_SWA, SWA_WIDTH), lambda b, c: (b, c, 0)),
            pl.BlockSpec((None, TQ_SWA, 2 * SWA_KV_WIDTH), lambda b, c: (b, c, 0)),
            pl.BlockSpec((None, BLOCK, 2 * SWA_KV_WIDTH),
                         lambda b, c: (b, jnp.maximum(c * n_blk - 1, 0), 0)),
            pl.BlockSpec(bkt.shape, lambda b, c: (0, 0)),
            smem(rel_bias),
            smem(sinks),
        ],
        out_specs=pl.BlockSpec((None, TQ_SWA, SWA_WIDTH), lambda b, c: (b, c, 0)),
        out_shape=jax.ShapeDtypeStruct((B, S, SWA_WIDTH), F32),
        scratch_shapes=[pltpu.VMEM((SWA_Q_HEADS // 2 * BLOCK, 2 * 2 * BLOCK), F32)],
        compiler_params=pltpu.CompilerParams(
            dimension_semantics=("arbitrary", "arbitrary"), vmem_limit_bytes=VMEM_LIMIT),
        name="swa",
    )(zq, zkv, zkv, bkt, rel_bias, sinks)


def _post_kernel(final, x_ref, ro_ref, so_ref, p_ref, wo_ref, gm_ref, up_ref, dn_ref, pp_ref, pg_ref,
                 gate_ref, fg_ref, o_ref):
    W = RWKV_WIDTH
    x = x_ref[...]
    x = x + _bdot(ro_ref[...], wo_ref[0:W, :]) + _bdot(so_ref[...], wo_ref[W:2 * W, :])
    u = _rms(x, gm_ref[...]).astype(BF16)
    acc = jnp.zeros_like(x)
    for c in range(D_FF // FF_CHUNK):
        cs = slice(c * FF_CHUNK, (c + 1) * FF_CHUNK)
        hid = jnp.dot(u, up_ref[:, cs], preferred_element_type=F32)
        hid = jnp.square(jnp.maximum(hid, 0.0))
        acc = acc + jnp.dot(hid.astype(BF16), dn_ref[cs, :], preferred_element_type=F32)
    x = x + acc
    e = _rms(_bdot(p_ref[...], pp_ref[...]), pg_ref[...])
    x = x + e * _sigmoid(_bdot(x, gate_ref[...]))
    if final:
        x = _rms(x, fg_ref[...])
    o_ref[...] = x


def _post(final, x2, ro, so, p2, wo, gm, up, dn, pp, pg, gate, fg):
    T = x2.shape[0]
    tile = lambda n: pl.BlockSpec((TM_PROJ, n), lambda i: (i, 0))
    res = lambda a: pl.BlockSpec(a.shape, lambda i: (0, 0), pipeline_mode=pl.Buffered(1))
    weights = [wo, gm, up, dn, pp, pg, gate, fg]
    return pl.pallas_call(
        functools.partial(_post_kernel, final),
        grid=(T // TM_PROJ,),
        in_specs=[tile(D_MODEL), tile(RWKV_WIDTH), tile(SWA_WIDTH), tile(PLE_DIM)] + [res(a) for a in weights],
        out_specs=tile(D_MODEL),
        out_shape=jax.ShapeDtypeStruct((T, D_MODEL), F32),
        compiler_params=pltpu.CompilerParams(
            dimension_semantics=("arbitrary",), vmem_limit_bytes=VMEM_LIMIT),
        name="post_final" if final else "post",
    )(x2, ro, so, p2, *weights)


def _bucket_tile():
    max_exact = REL_BUCKETS // 2
    dist = (np.arange(BLOCK)[:, None] + BLOCK) - np.arange(2 * BLOCK)[None, :]
    n = np.maximum(dist, 0)
    nf = np.maximum(n, 1).astype(np.float32)
    scaled = (np.log(nf / np.float32(max_exact)) / np.float32(math.log(REL_MAX_DIST / max_exact))
              * np.float32(REL_BUCKETS - max_exact)).astype(np.float32)
    large = np.minimum(max_exact + scaled.astype(np.int32), REL_BUCKETS - 1)
    bucket = np.where(n < max_exact, n, large)
    valid = (dist >= 0) & (dist < WINDOW)
    return np.where(valid, bucket, -1).astype(np.int32)


def _scan_consts(tt):
    t = np.arange(tt)
    same = (t[:, None] // CHUNK) == (t[None, :] // CHUNK)
    tril = same & (t[:, None] >= t[None, :])
    hh = np.arange(4 * HEAD_DIM) // HEAD_DIM
    seg = hh[:, None] == hh[None, :]
    return (jnp.asarray(np.concatenate([seg, seg], axis=0), BF16),
            jnp.asarray(np.concatenate([tril, tril], axis=1), BF16))


def kernel(x, p, norm_mix_g, w_in, mu_shift, w0, w_up, a0, a_up, g_up, vres_down, mu_vres, v0, vres_up,
           k_k, k_a, r_k, lnx_g, lnx_b, sinks, rel_bias, w_out, norm_mlp_g, w_ff_up, w_ff_down,
           ple_proj, ple_norm_g, ple_gate, final_norm_g):
    B, S, _ = x.shape
    depth = w_in.shape[0]
    T = B * S
    n_chunks = SCAN_CHUNKS_PER_STEP
    tiles_per_seq = S // (n_chunks * CHUNK)
    row = lambda a: a.reshape(1, -1).astype(F32)

    bkt = jnp.asarray(_bucket_tile())
    consts = _scan_consts(n_chunks * CHUNK)

    x2 = x.reshape(T, D_MODEL)
    v_first = None
    for i in range(depth):
        w_cols = w_in[i]
        widths = [RWKV_COLS, SWA_WIDTH, 2 * SWA_KV_WIDTH]
        if i > 0:
            pad = jnp.zeros((D_MODEL, VRES_PAD - MV_LORA), w_cols.dtype)
            w_cols = jnp.concatenate([w_cols, vres_down[i - 1], pad], axis=1)
            widths.append(VRES_PAD)
        zs = _inproj(x2, row(norm_mix_g[i]), w_cols.astype(BF16), widths)
        zr = zs[0]
        zq = zs[1].reshape(B, S, SWA_WIDTH)
        zkv = zs[2].reshape(B, S, 2 * SWA_KV_WIDTH)

        prm = {
            "mu": row(mu_shift[i]), "w0": row(w0[i]), "w_up": w_up[i].astype(BF16), "a0": row(a0[i]),
            "a_up": a_up[i].astype(BF16), "g_up": g_up[i].astype(BF16), "k_k": row(k_k[i]),
            "k_a": row(k_a[i]), "r_k": row(r_k[i]), "lnx_g": row(lnx_g[i]), "lnx_b": row(lnx_b[i]),
        }
        if i == 0:
            rwkv_out, v_first = _rwkv(False, zr, None, None, prm, consts, n_chunks, tiles_per_seq)
        else:
            prm["mu_vres"] = jnp.pad(row(mu_vres[i - 1]), ((0, 0), (0, VRES_PAD - MV_LORA)))
            prm["v0"] = row(v0[i - 1])
            prm["vres_up"] = jnp.pad(vres_up[i - 1], ((0, VRES_PAD - MV_LORA), (0, 0))).astype(BF16)
            (rwkv_out,) = _rwkv(True, zr, zs[3], v_first, prm, consts, n_chunks, tiles_per_seq)

        swa_out = _swa(zq, zkv, bkt, rel_bias.astype(F32), sinks[i].astype(F32))

        x2 = _post(i == depth - 1, x2, rwkv_out, swa_out.reshape(T, SWA_WIDTH),
                   p[i].reshape(T, PLE_DIM), w_out[i].astype(BF16), row(norm_mlp_g[i]),
                   w_ff_up[i].astype(BF16), w_ff_down[i].astype(BF16), ple_proj[i].astype(BF16),
                   row(ple_norm_g[i]), ple_gate[i].astype(BF16), row(final_norm_g))
    return x2.reshape(B, S, D_MODEL)
```

```python
import functools
import math

import jax
import jax.numpy as jnp
import numpy as np
from jax import lax
from jax.experimental import pallas as pl
from jax.experimental.pallas import tpu as pltpu

F32 = jnp.float32
BF16 = jnp.bfloat16

D_MODEL = 1024
PLE_DIM = 256
HEADS = 8
HEAD_DIM = 64
RWKV_WIDTH = HEADS * HEAD_DIM
DECAY_LORA = 64
AAA_LORA = 64
MV_LORA = 32
GATE_LORA = 128
LNX_EPS = 64e-5
SWA_Q_HEADS = 8
SWA_KV_HEADS = 2
SWA_GROUP = SWA_Q_HEADS // SWA_KV_HEADS
SWA_WIDTH = SWA_Q_HEADS * HEAD_DIM
SWA_KV_WIDTH = SWA_KV_HEADS * HEAD_DIM
WINDOW = 128
BLOCK = 128
ATTN_SCALE = 1.0 / math.sqrt(HEAD_DIM)
REL_BUCKETS = 32
REL_MAX_DIST = 128
D_FF = 4 * D_MODEL
NORM_EPS = 1e-6
RWKV_COLS = 3 * RWKV_WIDTH + DECAY_LORA + AAA_LORA + GATE_LORA
SWA_COLS = SWA_WIDTH + 2 * SWA_KV_WIDTH
IN_COLS = RWKV_COLS + SWA_COLS

LANES = 128
VRES_PAD = LANES
CHUNK = 64
SCAN_CHUNKS_PER_STEP = 4
MASK_VALUE = -1e30

TM_PROJ = 512
TQ_SWA = 512
FF_CHUNK = 1024
VMEM_LIMIT = 56 * 1024 * 1024


def _rms(x, g):
    ms = jnp.mean(x * x, axis=-1, keepdims=True)
    return (x * lax.rsqrt(ms + NORM_EPS)) * g


def _bdot(a, b):
    return jnp.dot(a.astype(BF16), b.astype(BF16), preferred_element_type=F32)


def _split2(x):
    hi = x.astype(BF16)
    lo = (x - hi.astype(F32)).astype(BF16)
    return hi, lo


def _dot01_left(m01x2, x):
    hi, lo = _split2(x)
    return jnp.dot(m01x2, jnp.concatenate([hi, lo], axis=0), preferred_element_type=F32)


def _head_sums(x, segx2):
    half = segx2.shape[1]
    outs = []
    for c in range(x.shape[1] // half):
        hi, lo = _split2(x[:, c * half:(c + 1) * half])
        outs.append(jnp.dot(jnp.concatenate([hi, lo], axis=1), segx2, preferred_element_type=F32))
    return jnp.concatenate(outs, axis=1)


def _inproj_kernel(x_ref, g_ref, w_ref, *out_refs):
    h = _rms(x_ref[...], g_ref[...]).astype(BF16)
    off = 0
    for o_ref in out_refs:
        n = o_ref.shape[-1]
        o_ref[...] = jnp.dot(h, w_ref[:, off:off + n], preferred_element_type=F32)
        off += n


def _inproj(x2, g, w_bf16, widths):
    T = x2.shape[0]
    nc = w_bf16.shape[1]
    assert sum(widths) == nc
    return pl.pallas_call(
        _inproj_kernel,
        grid=(T // TM_PROJ,),
        in_specs=[
            pl.BlockSpec((TM_PROJ, D_MODEL), lambda i: (i, 0)),
            pl.BlockSpec((1, D_MODEL), lambda i: (0, 0)),
            pl.BlockSpec((D_MODEL, nc), lambda i: (0, 0)),
        ],
        out_specs=[pl.BlockSpec((TM_PROJ, n), lambda i: (i, 0)) for n in widths],
        out_shape=[jax.ShapeDtypeStruct((T, n), F32) for n in widths],
        compiler_params=pltpu.CompilerParams(
            dimension_semantics=("arbitrary",), vmem_limit_bytes=VMEM_LIMIT),
        name="inproj",
    )(x2, g, w_bf16)


def _sigmoid(x):
    return 0.5 * jnp.tanh(0.5 * x) + 0.5


def _shift(z, prev_row, mu):
    rolled = pltpu.roll(z, 1, 0)
    row = lax.broadcasted_iota(jnp.int32, z.shape, 0)
    prev = jnp.where(row == 0, prev_row, rolled)
    return z + (prev - z) * mu


def _rwkv_kernel(has_vres, n_chunks, *refs):
    if has_vres:
        (zr_ref, zv_ref, vf_ref, mu_ref, w0_ref, wup_ref, a0_ref, aup_ref, gup_ref, kk_ref, ka_ref,
         rk_ref, lg_ref, lb_ref, muv_ref, v0_ref, vup_ref, seg_ref, tril_ref,
         out_ref, h_ref, prev_ref, prevv_ref) = refs
    else:
        (zr_ref, mu_ref, w0_ref, wup_ref, a0_ref, aup_ref, gup_ref, kk_ref, ka_ref,
         rk_ref, lg_ref, lb_ref, seg_ref, tril_ref,
         out_ref, vf_out_ref, h_ref, prev_ref) = refs

    c_idx = pl.program_id(1)

    @pl.when(c_idx == 0)
    def _():
        h_ref[...] = jnp.zeros_like(h_ref)
        prev_ref[...] = jnp.zeros_like(prev_ref)
        if has_vres:
            prevv_ref[...] = jnp.zeros_like(prevv_ref)

    W = RWKV_WIDTH
    z = zr_ref[...]
    tt = z.shape[0]
    zs = _shift(z, prev_ref[...], mu_ref[...])
    prev_ref[...] = z[tt - 1:tt, :]
    r = zs[:, 0:W]
    k = zs[:, W:2 * W]
    v = zs[:, 2 * W:3 * W]
    o1 = 3 * W
    xw = zs[:, o1:o1 + DECAY_LORA]
    xa = zs[:, o1 + DECAY_LORA:o1 + DECAY_LORA + AAA_LORA]
    xg = zs[:, o1 + DECAY_LORA + AAA_LORA:]

    ld = -math.exp(-0.5) * _sigmoid(w0_ref[...] + _bdot(jnp.tanh(xw), wup_ref[...]))
    a = _sigmoid(a0_ref[...] + _bdot(xa, aup_ref[...]))
    g = _bdot(_sigmoid(xg), gup_ref[...])
    if has_vres:
        zv = zv_ref[...]
        zvs = _shift(zv, prevv_ref[...], muv_ref[...])
        prevv_ref[...] = zv[tt - 1:tt, :]
        v = v + (vf_ref[...] - v) * _sigmoid(v0_ref[...] + _bdot(zvs, vup_ref[...]))
    else:
        vf_out_ref[...] = v

    seg = seg_ref[...]
    kk = k * kk_ref[...]
    kk = kk * jnp.minimum(lax.rsqrt(_head_sums(kk * kk, seg)), 1e12)
    kadj = k * (1.0 + (a - 1.0) * ka_ref[...])
    b = kk * a

    c = _dot01_left(tril_ref[...], ld)
    cl = jnp.concatenate(
        [jnp.broadcast_to(c[(ch + 1) * CHUNK - 1:(ch + 1) * CHUNK, :], (CHUNK, W)) for ch in range(n_chunks)],
        axis=0)
    qt = kk * jnp.exp(c - ld)
    rt = r * jnp.exp(c)
    en = jnp.exp(-c)
    kt = kadj * en
    bt = b * en
    el = jnp.exp(cl - c)
    kh = kadj * el
    bh = b * el
    pl_all = jnp.exp(cl)

    L = CHUNK
    P2 = 2 * HEAD_DIM
    lo = lax.broadcasted_iota(jnp.int32, (1, P2), 1) < HEAD_DIM
    ri = lax.broadcasted_iota(jnp.int32, (L, P2), 0)
    ci = lax.broadcasted_iota(jnp.int32, (L, P2), 1) % HEAD_DIM
    strict = ri > ci
    incl = ri >= ci
    eye2 = (ri == ci).astype(F32)
    units = [(ch, pr) for ch in range(n_chunks) for pr in range(HEADS // 2)]

    cat0 = lambda *xs: jnp.concatenate(xs, axis=0)
    cat1 = lambda *xs: jnp.concatenate(xs, axis=1)
    mm = lambda a_, b_: jnp.dot(a_, b_, preferred_element_type=F32)
    nt = lambda a_, b_: lax.dot_general(a_, b_, (((1,), (1,)), ((), ())), preferred_element_type=F32)
    tn = lambda a_, b_: lax.dot_general(a_, b_, (((0,), (0,)), ((), ())), preferred_element_type=F32)
    zero = jnp.zeros((), BF16)

    def bd(x_):
        x_ = x_.astype(BF16)
        return cat0(jnp.where(lo, x_, zero), jnp.where(lo, zero, x_))

    def diag_blocks(x_):
        sel = lo if x_.shape[1] == P2 else jnp.concatenate([lo] * (x_.shape[1] // P2), axis=1)
        return jnp.where(sel, x_[:HEAD_DIM], x_[HEAD_DIM:])

    def per_unit(arr, dtype=BF16):
        arr = arr.astype(dtype)
        return [arr[ch * L:(ch + 1) * L, pr * P2:(pr + 1) * P2] for ch, pr in units]

    qh, rh, kth, bth, vh = per_unit(qt), per_unit(rt), per_unit(kt), per_unit(bt), per_unit(v)
    khh, bhh = per_unit(kh), per_unit(bh)
    rf = per_unit(rt, F32)
    amat = [nt(cat0(q_, r_), cat0(bd(k_), bd(b_))) for q_, r_, k_, b_ in zip(qh, rh, kth, bth)]
    a_qk = [jnp.where(strict, m_[:L, :P2], 0.0).astype(BF16) for m_ in amat]
    a_qb = [jnp.where(strict, m_[:L, P2:], 0.0) for m_ in amat]
    a_rk = [jnp.where(incl, m_[L:, :P2], 0.0).astype(BF16) for m_ in amat]
    a_rb = [jnp.where(incl, m_[L:, P2:], 0.0).astype(BF16) for m_ in amat]
    tinv = [eye2 - m_ for m_ in a_qb]
    pw = [m_.astype(BF16) for m_ in a_qb]
    pw = [mm(p_, bd(p_)).astype(BF16) for p_ in pw]
    for _ in range(int(math.log2(L)) - 2):
        both = [mm(cat0(t_.astype(BF16), p_), bd(p_)) for t_, p_ in zip(tinv, pw)]
        tinv = [t_ + b_[:L] for t_, b_ in zip(tinv, both)]
        pw = [b_[L:].astype(BF16) for b_ in both]
    tinv = [t_ + mm(t_.astype(BF16), bd(p_)) for t_, p_ in zip(tinv, pw)]
    av = [mm(cat0(ak_, ar_), bd(v_)) for ak_, ar_, v_ in zip(a_qk, a_rk, vh)]
    x = [mm(t_.astype(BF16), cat1(bd(q_), bd(av_[:L]))).astype(BF16)
         for t_, q_, av_ in zip(tinv, qh, av)]
    ry = [cat1(r_, av_[L:]) - mm(ab_, cat1(bd(x_[:, :P2]), bd(x_[:, P2:])))
          for r_, av_, ab_, x_ in zip(rf, av, a_rb, x)]
    bx = [diag_blocks(tn(b_, x_)) for b_, x_ in zip(bhh, x)]
    kv = [diag_blocks(tn(k_, v_)) for k_, v_ in zip(khh, vh)]
    mmat = [(eye2 * pl_all[ch * L:ch * L + 1, pr * P2:(pr + 1) * P2] - bx_[:, :P2]).astype(BF16)
            for (ch, pr), bx_ in zip(units, bx)]
    nmat = [kv_ - bx_[:, P2:] for kv_, bx_ in zip(kv, bx)]

    n_pairs = HEADS // 2
    h_cur = [h_ref[:, pr * P2:(pr + 1) * P2] for pr in range(n_pairs)]
    y_rows = []
    for ch in range(n_chunks):
        y_pairs = []
        for pr in range(n_pairs):
            u = ch * n_pairs + pr
            both = mm(cat0(ry[u][:, :P2].astype(BF16), mmat[u]), bd(h_cur[pr]))
            y_pairs.append(both[:L] + ry[u][:, P2:])
            h_cur[pr] = both[L:] + nmat[u]
        y_rows.append(cat1(*y_pairs))
    h_ref[...] = cat1(*h_cur)
    y = y_rows[0] if n_chunks == 1 else cat0(*y_rows)

    inv_n = 1.0 / HEAD_DIM
    ym = _head_sums(y, seg) * inv_n
    yc = y - ym
    yv = _head_sums(yc * yc, seg) * inv_n
    y = yc * lax.rsqrt(yv + LNX_EPS)
    y = y * lg_ref[...] + lb_ref[...]
    y = y + _head_sums(r * kadj * rk_ref[...], seg) * v
    out_ref[...] = y * g


def _rwkv(has_vres, zr, zv, v_first, prm, consts, n_chunks):
    B, S, _ = zr.shape
    tt = n_chunks * CHUNK
    W = RWKV_WIDTH
    tile = lambda n: pl.BlockSpec((None, tt, n), lambda b, c: (b, c, 0))
    full = lambda a: pl.BlockSpec(a.shape, lambda b, c: (0,) * a.ndim)
    names = ["mu", "w0", "w_up", "a0", "a_up", "g_up", "k_k", "k_a", "r_k", "lnx_g", "lnx_b"]
    if has_vres:
        names += ["mu_vres", "v0", "vres_up"]
    params = [prm[n] for n in names] + list(consts)
    if has_vres:
        acts = [zr, zv, v_first]
        act_specs = [tile(RWKV_COLS), tile(VRES_PAD), tile(W)]
        out_shape = [jax.ShapeDtypeStruct((B, S, W), F32)]
        out_specs = [tile(W)]
        scratch = [pltpu.VMEM((HEAD_DIM, W), F32), pltpu.VMEM((1, RWKV_COLS), F32),
                   pltpu.VMEM((1, VRES_PAD), F32)]
    else:
        acts = [zr]
        act_specs = [tile(RWKV_COLS)]
        out_shape = [jax.ShapeDtypeStruct((B, S, W), F32)] * 2
        out_specs = [tile(W), tile(W)]
        scratch = [pltpu.VMEM((HEAD_DIM, W), F32), pltpu.VMEM((1, RWKV_COLS), F32)]
    return pl.pallas_call(
        functools.partial(_rwkv_kernel, has_vres, n_chunks),
        grid=(B, S // tt),
        in_specs=act_specs + [full(a) for a in params],
        out_specs=out_specs,
        out_shape=out_shape,
        scratch_shapes=scratch,
        compiler_params=pltpu.CompilerParams(
            dimension_semantics=("arbitrary", "arbitrary"), vmem_limit_bytes=VMEM_LIMIT),
        name="rwkv_vres" if has_vres else "rwkv",
    )(*acts, *params)


def _swa_kernel(n_blk, q_ref, kv_ref, kvp_ref, bkt_ref, relb_ref, sink_ref, o_ref, bias_ref):
    b_idx = pl.program_id(0)
    c_idx = pl.program_id(1)
    G2 = 2 * BLOCK

    @pl.when((b_idx == 0) & (c_idx == 0))
    def _build_bias():
        bkt = bkt_ref[...]
        for h in range(SWA_Q_HEADS):
            acc = jnp.full(bkt.shape, MASK_VALUE, F32)
            for k in range(REL_BUCKETS):
                acc = jnp.where(bkt == k, relb_ref[k, h], acc)
            pair, half = divmod(h, 2)
            bias_ref[pair * BLOCK:(pair + 1) * BLOCK, half * G2:(half + 1) * G2] = acc

    lo = lax.broadcasted_iota(jnp.int32, (1, LANES), 1) < HEAD_DIM
    ones_lo = jnp.broadcast_to(lo.astype(BF16), (G2, LANES))
    ones_hi = jnp.broadcast_to((~lo).astype(BF16), (G2, LANES))
    top = lax.broadcasted_iota(jnp.int32, (2 * BLOCK, 1), 0) < BLOCK
    prev_col = (lax.broadcasted_iota(jnp.int32, (2 * BLOCK, 2 * G2), 1) % G2) < BLOCK
    units = [(j, g) for j in range(n_blk) for g in range(SWA_KV_HEADS)]

    kk, vv = {}, {}
    for j in range(n_blk):
        if j == 0:
            kv = jnp.concatenate([kvp_ref[...], kv_ref[0:BLOCK, :]], axis=0)
        else:
            kv = kv_ref[(j - 1) * BLOCK:(j + 1) * BLOCK, :]
        kcol, vcol = kv[:, :LANES], kv[:, LANES:]
        for arr, dst, extra in ((kcol, kk, None), (vcol, vv, (ones_lo, ones_hi))):
            g0a = jnp.where(lo, arr, 0.0)
            g1b = jnp.where(lo, 0.0, arr)
            g0b = pltpu.roll(g0a, HEAD_DIM, 1)
            g1a = pltpu.roll(g1b, HEAD_DIM, 1)
            for g, (xa, xb) in enumerate(((g0a, g0b), (g1a, g1b))):
                xa, xb = xa.astype(BF16), xb.astype(BF16)
                if extra is None:
                    dst[j, g] = jnp.concatenate([xa, xb], axis=0)
                else:
                    dst[j, g] = jnp.concatenate(
                        [jnp.concatenate([xa, extra[0]], axis=1),
                         jnp.concatenate([xb, extra[1]], axis=1)], axis=0)

    def q_pairs(j, g):
        q2 = (q_ref[j * BLOCK:(j + 1) * BLOCK, g * 2 * LANES:(g + 1) * 2 * LANES] * ATTN_SCALE).astype(BF16)
        return jnp.concatenate([q2[:, :LANES], q2[:, LANES:]], axis=0)

    logits = [lax.dot_general(q_pairs(j, g), kk[j, g], (((1,), (1,)), ((), ())), preferred_element_type=F32)
              + bias_ref[g * 2 * BLOCK:(g + 1) * 2 * BLOCK, :] for j, g in units]
    logits = [jnp.where(prev_col & (c_idx == 0), MASK_VALUE, l_) if j == 0 else l_
              for (j, g), l_ in zip(units, logits)]
    sinks = [[jnp.where(top, sink_ref[4 * g + half], sink_ref[4 * g + 2 + half]) for half in range(2)]
             for j, g in units]
    ms = [[jnp.maximum(jnp.max(l_[:, half * G2:(half + 1) * G2], axis=-1, keepdims=True), s_[half])
           for half in range(2)] for l_, s_ in zip(logits, sinks)]
    es = [jnp.concatenate([jnp.exp(l_[:, half * G2:(half + 1) * G2] - m_[half]) for half in range(2)],
                          axis=1).astype(BF16) for l_, m_ in zip(logits, ms)]
    res = [jnp.dot(e_, vv[u], preferred_element_type=F32) for e_, u in zip(es, units)]
    for (j, g), r_, m_, s_ in zip(units, res, ms, sinks):
        den = r_[:, LANES:] + jnp.where(lo, jnp.exp(s_[0] - m_[0]), jnp.exp(s_[1] - m_[1]))
        out = r_[:, :LANES] / den
        rows = slice(j * BLOCK, (j + 1) * BLOCK)
        o_ref[rows, (2 * g) * LANES:(2 * g + 1) * LANES] = out[:BLOCK]
        o_ref[rows, (2 * g + 1) * LANES:(2 * g + 2) * LANES] = out[BLOCK:]


def _swa(zq, zkv, bkt, rel_bias, sinks):
    B, S, _ = zq.shape
    n_blk = TQ_SWA // BLOCK
    smem = lambda a: pl.BlockSpec(a.shape, lambda b, c: (0,) * a.ndim, memory_space=pltpu.SMEM)
    return pl.pallas_call(
        functools.partial(_swa_kernel, n_blk),
        grid=(B, S // TQ_SWA),
        in_specs=[
            pl.BlockSpec((None, TQ_SWA, SWA_WIDTH), lambda b, c: (b, c, 0)),
            pl.BlockSpec((None, TQ_SWA, 2 * SWA_KV_WIDTH), lambda b, c: (b, c, 0)),
            pl.BlockSpec((None, BLOCK, 2 * SWA_KV_WIDTH),
                         lambda b, c: (b, jnp.maximum(c * n_blk - 1, 0), 0)),
            pl.BlockSpec(bkt.shape, lambda b, c: (0, 0)),
            smem(rel_bias),
            smem(sinks),
        ],
        out_specs=pl.BlockSpec((None, TQ_SWA, SWA_WIDTH), lambda b, c: (b, c, 0)),
        out_shape=jax.ShapeDtypeStruct((B, S, SWA_WIDTH), F32),
        scratch_shapes=[pltpu.VMEM((SWA_Q_HEADS // 2 * BLOCK, 2 * 2 * BLOCK), F32)],
        compiler_params=pltpu.CompilerParams(
            dimension_semantics=("arbitrary", "arbitrary"), vmem_limit_bytes=VMEM_LIMIT),
        name="swa",
    )(zq, zkv, zkv, bkt, rel_bias, sinks)


def _post_kernel(final, x_ref, ro_ref, so_ref, p_ref, wo_ref, gm_ref, up_ref, dn_ref, pp_ref, pg_ref,
                 gate_ref, fg_ref, o_ref):
    W = RWKV_WIDTH
    x = x_ref[...]
    x = x + _bdot(ro_ref[...], wo_ref[0:W, :]) + _bdot(so_ref[...], wo_ref[W:2 * W, :])
    u = _rms(x, gm_ref[...]).astype(BF16)
    acc = jnp.zeros_like(x)
    for c in range(D_FF // FF_CHUNK):
        cs = slice(c * FF_CHUNK, (c + 1) * FF_CHUNK)
        hid = jnp.dot(u, up_ref[:, cs], preferred_element_type=F32)
        hid = jnp.square(jnp.maximum(hid, 0.0))
        acc = acc + jnp.dot(hid.astype(BF16), dn_ref[cs, :], preferred_element_type=F32)
    x = x + acc
    e = _rms(_bdot(p_ref[...], pp_ref[...]), pg_ref[...])
    x = x + e * _sigmoid(_bdot(x, gate_ref[...]))
    if final:
        x = _rms(x, fg_ref[...])
    o_ref[...] = x


def _post(final, x2, ro, so, p2, wo, gm, up, dn, pp, pg, gate, fg):
    T = x2.shape[0]
    tile = lambda n: pl.BlockSpec((TM_PROJ, n), lambda i: (i, 0))
    res = lambda a: pl.BlockSpec(a.shape, lambda i: (0, 0), pipeline_mode=pl.Buffered(1))
    weights = [wo, gm, up, dn, pp, pg, gate, fg]
    return pl.pallas_call(
        functools.partial(_post_kernel, final),
        grid=(T // TM_PROJ,),
        in_specs=[tile(D_MODEL), tile(RWKV_WIDTH), tile(SWA_WIDTH), tile(PLE_DIM)] + [res(a) for a in weights],
        out_specs=tile(D_MODEL),
        out_shape=jax.ShapeDtypeStruct((T, D_MODEL), F32),
        compiler_params=pltpu.CompilerParams(
            dimension_semantics=("arbitrary",), vmem_limit_bytes=VMEM_LIMIT),
        name="post_final" if final else "post",
    )(x2, ro, so, p2, *weights)


def _bucket_tile():
    max_exact = REL_BUCKETS // 2
    dist = (np.arange(BLOCK)[:, None] + BLOCK) - np.arange(2 * BLOCK)[None, :]
    n = np.maximum(dist, 0)
    nf = np.maximum(n, 1).astype(np.float32)
    scaled = (np.log(nf / np.float32(max_exact)) / np.float32(math.log(REL_MAX_DIST / max_exact))
              * np.float32(REL_BUCKETS - max_exact)).astype(np.float32)
    large = np.minimum(max_exact + scaled.astype(np.int32), REL_BUCKETS - 1)
    bucket = np.where(n < max_exact, n, large)
    valid = (dist >= 0) & (dist < WINDOW)
    return np.where(valid, bucket, -1).astype(np.int32)


def _scan_consts(tt):
    t = np.arange(tt)
    same = (t[:, None] // CHUNK) == (t[None, :] // CHUNK)
    tril = same & (t[:, None] >= t[None, :])
    hh = np.arange(4 * HEAD_DIM) // HEAD_DIM
    seg = hh[:, None] == hh[None, :]
    return (jnp.asarray(np.concatenate([seg, seg], axis=0), BF16),
            jnp.asarray(np.concatenate([tril, tril], axis=1), BF16))


def kernel(x, p, norm_mix_g, w_in, mu_shift, w0, w_up, a0, a_up, g_up, vres_down, mu_vres, v0, vres_up,
           k_k, k_a, r_k, lnx_g, lnx_b, sinks, rel_bias, w_out, norm_mlp_g, w_ff_up, w_ff_down,
           ple_proj, ple_norm_g, ple_gate, final_norm_g):
    B, S, _ = x.shape
    depth = w_in.shape[0]
    T = B * S
    n_chunks = SCAN_CHUNKS_PER_STEP
    row = lambda a: a.reshape(1, -1).astype(F32)

    bkt = jnp.asarray(_bucket_tile())
    consts = _scan_consts(n_chunks * CHUNK)

    x2 = x.reshape(T, D_MODEL)
    v_first = None
    for i in range(depth):
        w_cols = w_in[i]
        widths = [RWKV_COLS, SWA_WIDTH, 2 * SWA_KV_WIDTH]
        if i > 0:
            pad = jnp.zeros((D_MODEL, VRES_PAD - MV_LORA), w_cols.dtype)
            w_cols = jnp.concatenate([w_cols, vres_down[i - 1], pad], axis=1)
            widths.append(VRES_PAD)
        zs = _inproj(x2, row(norm_mix_g[i]), w_cols.astype(BF16), widths)
        zr = zs[0].reshape(B, S, RWKV_COLS)
        zq = zs[1].reshape(B, S, SWA_WIDTH)
        zkv = zs[2].reshape(B, S, 2 * SWA_KV_WIDTH)

        prm = {
            "mu": row(mu_shift[i]), "w0": row(w0[i]), "w_up": w_up[i].astype(BF16), "a0": row(a0[i]),
            "a_up": a_up[i].astype(BF16), "g_up": g_up[i].astype(BF16), "k_k": row(k_k[i]),
            "k_a": row(k_a[i]), "r_k": row(r_k[i]), "lnx_g": row(lnx_g[i]), "lnx_b": row(lnx_b[i]),
        }
        if i == 0:
            rwkv_out, v_first = _rwkv(False, zr, None, None, prm, consts, n_chunks)
        else:
            prm["mu_vres"] = jnp.pad(row(mu_vres[i - 1]), ((0, 0), (0, VRES_PAD - MV_LORA)))
            prm["v0"] = row(v0[i - 1])
            prm["vres_up"] = jnp.pad(vres_up[i - 1], ((0, VRES_PAD - MV_LORA), (0, 0))).astype(BF16)
            zv = zs[3].reshape(B, S, VRES_PAD)
            (rwkv_out,) = _rwkv(True, zr, zv, v_first, prm, consts, n_chunks)

        swa_out = _swa(zq, zkv, bkt, rel_bias.astype(F32), sinks[i].astype(F32))

        x2 = _post(i == depth - 1, x2, rwkv_out.reshape(T, RWKV_WIDTH), swa_out.reshape(T, SWA_WIDTH),
                   p[i].reshape(T, PLE_DIM), w_out[i].astype(BF16), row(norm_mlp_g[i]),
                   w_ff_up[i].astype(BF16), w_ff_down[i].astype(BF16), ple_proj[i].astype(BF16),
                   row(ple_norm_g[i]), ple_gate[i].astype(BF16), row(final_norm_g))
    return x2.reshape(B, S, D_MODEL)
```

```python
import functools
import math

import jax
import jax.numpy as jnp
import numpy as np
from jax import lax
from jax.experimental import pallas as pl
from jax.experimental.pallas import tpu as pltpu

F32 = jnp.float32
BF16 = jnp.bfloat16

D_MODEL = 1024
PLE_DIM = 256
HEADS = 8
HEAD_DIM = 64
RWKV_WIDTH = HEADS * HEAD_DIM
DECAY_LORA = 64
AAA_LORA = 64
MV_LORA = 32
GATE_LORA = 128
LNX_EPS = 64e-5
SWA_Q_HEADS = 8
SWA_KV_HEADS = 2
SWA_GROUP = SWA_Q_HEADS // SWA_KV_HEADS
SWA_WIDTH = SWA_Q_HEADS * HEAD_DIM
SWA_KV_WIDTH = SWA_KV_HEADS * HEAD_DIM
WINDOW = 128
BLOCK = 128
ATTN_SCALE = 1.0 / math.sqrt(HEAD_DIM)
REL_BUCKETS = 32
REL_MAX_DIST = 128
D_FF = 4 * D_MODEL
NORM_EPS = 1e-6
RWKV_COLS = 3 * RWKV_WIDTH + DECAY_LORA + AAA_LORA + GATE_LORA
SWA_COLS = SWA_WIDTH + 2 * SWA_KV_WIDTH
IN_COLS = RWKV_COLS + SWA_COLS

LANES = 128
VRES_PAD = LANES
CHUNK = 64
SCAN_CHUNKS_PER_STEP = 4
MASK_VALUE = -1e30

TM_PROJ = 512
TQ_SWA = 512
FF_CHUNK = 1024
VMEM_LIMIT = 56 * 1024 * 1024


def _rms(x, g):
    ms = jnp.mean(x * x, axis=-1, keepdims=True)
    return (x * lax.rsqrt(ms + NORM_EPS)) * g


def _bdot(a, b):
    return jnp.dot(a.astype(BF16), b.astype(BF16), preferred_element_type=F32)


def _split2(x):
    hi = x.astype(BF16)
    lo = (x - hi.astype(F32)).astype(BF16)
    return hi, lo


def _dot01_left(m01x2, x):
    hi, lo = _split2(x)
    return jnp.dot(m01x2, jnp.concatenate([hi, lo], axis=0), preferred_element_type=F32)


def _head_sums(x, seg):
    half = seg.shape[0]
    return jnp.concatenate(
        [jnp.dot(x[:, c * half:(c + 1) * half].astype(BF16), seg, preferred_element_type=F32)
         for c in range(x.shape[1] // half)], axis=1)


def _inproj_kernel(x_ref, g_ref, w_ref, *out_refs):
    h = _rms(x_ref[...], g_ref[...]).astype(BF16)
    off = 0
    for o_ref in out_refs:
        n = o_ref.shape[-1]
        o_ref[...] = jnp.dot(h, w_ref[:, off:off + n], preferred_element_type=F32)
        off += n


def _inproj(x2, g, w_bf16, widths):
    T = x2.shape[0]
    nc = w_bf16.shape[1]
    assert sum(widths) == nc
    return pl.pallas_call(
        _inproj_kernel,
        grid=(T // TM_PROJ,),
        in_specs=[
            pl.BlockSpec((TM_PROJ, D_MODEL), lambda i: (i, 0)),
            pl.BlockSpec((1, D_MODEL), lambda i: (0, 0)),
            pl.BlockSpec((D_MODEL, nc), lambda i: (0, 0)),
        ],
        out_specs=[pl.BlockSpec((TM_PROJ, n), lambda i: (i, 0)) for n in widths],
        out_shape=[jax.ShapeDtypeStruct((T, n), F32) for n in widths],
        compiler_params=pltpu.CompilerParams(
            dimension_semantics=("arbitrary",), vmem_limit_bytes=VMEM_LIMIT),
        name="inproj",
    )(x2, g, w_bf16)


def _sigmoid(x):
    return 0.5 * jnp.tanh(0.5 * x) + 0.5


def _shift(z, prev_row, mu):
    rolled = pltpu.roll(z, 1, 0)
    row = lax.broadcasted_iota(jnp.int32, z.shape, 0)
    prev = jnp.where(row == 0, prev_row, rolled)
    return z + (prev - z) * mu


def _rwkv_kernel(has_vres, n_chunks, *refs):
    if has_vres:
        (zr_ref, zv_ref, vf_ref, mu_ref, w0_ref, wup_ref, a0_ref, aup_ref, gup_ref, kk_ref, ka_ref,
         rk_ref, lg_ref, lb_ref, muv_ref, v0_ref, vup_ref, seg_ref, tril_ref,
         out_ref, h_ref, prev_ref, prevv_ref) = refs
    else:
        (zr_ref, mu_ref, w0_ref, wup_ref, a0_ref, aup_ref, gup_ref, kk_ref, ka_ref,
         rk_ref, lg_ref, lb_ref, seg_ref, tril_ref,
         out_ref, vf_out_ref, h_ref, prev_ref) = refs

    c_idx = pl.program_id(1)

    @pl.when(c_idx == 0)
    def _():
        h_ref[...] = jnp.zeros_like(h_ref)
        prev_ref[...] = jnp.zeros_like(prev_ref)
        if has_vres:
            prevv_ref[...] = jnp.zeros_like(prevv_ref)

    W = RWKV_WIDTH
    z = zr_ref[...]
    tt = z.shape[0]
    zs = _shift(z, prev_ref[...], mu_ref[...])
    prev_ref[...] = z[tt - 1:tt, :]
    r = zs[:, 0:W]
    k = zs[:, W:2 * W]
    v = zs[:, 2 * W:3 * W]
    o1 = 3 * W
    xw = zs[:, o1:o1 + DECAY_LORA]
    xa = zs[:, o1 + DECAY_LORA:o1 + DECAY_LORA + AAA_LORA]
    xg = zs[:, o1 + DECAY_LORA + AAA_LORA:]

    ld = -math.exp(-0.5) * _sigmoid(w0_ref[...] + _bdot(jnp.tanh(xw), wup_ref[...]))
    a = _sigmoid(a0_ref[...] + _bdot(xa, aup_ref[...]))
    g = _bdot(_sigmoid(xg), gup_ref[...])
    if has_vres:
        zv = zv_ref[...]
        zvs = _shift(zv, prevv_ref[...], muv_ref[...])
        prevv_ref[...] = zv[tt - 1:tt, :]
        v = v + (vf_ref[...] - v) * _sigmoid(v0_ref[...] + _bdot(zvs, vup_ref[...]))
    else:
        vf_out_ref[...] = v

    seg = seg_ref[...]
    kk = k * kk_ref[...]
    kk = kk * jnp.minimum(lax.rsqrt(_head_sums(kk * kk, seg)), 1e12)
    kadj = k * (1.0 + (a - 1.0) * ka_ref[...])
    b = kk * a

    c = _dot01_left(tril_ref[...], ld)
    cl = jnp.concatenate(
        [jnp.broadcast_to(c[(ch + 1) * CHUNK - 1:(ch + 1) * CHUNK, :], (CHUNK, W)) for ch in range(n_chunks)],
        axis=0)
    qt = kk * jnp.exp(c - ld)
    rt = r * jnp.exp(c)
    en = jnp.exp(-c)
    kt = kadj * en
    bt = b * en
    el = jnp.exp(cl - c)
    kh = kadj * el
    bh = b * el
    pl_all = jnp.exp(cl)

    L = CHUNK
    P2 = 2 * HEAD_DIM
    lo = lax.broadcasted_iota(jnp.int32, (1, P2), 1) < HEAD_DIM
    ri = lax.broadcasted_iota(jnp.int32, (L, P2), 0)
    ci = lax.broadcasted_iota(jnp.int32, (L, P2), 1) % HEAD_DIM
    strict = ri > ci
    incl = ri >= ci
    eye2 = (ri == ci).astype(F32)
    units = [(ch, pr) for ch in range(n_chunks) for pr in range(HEADS // 2)]

    cat0 = lambda *xs: jnp.concatenate(xs, axis=0)
    cat1 = lambda *xs: jnp.concatenate(xs, axis=1)
    mm = lambda a_, b_: jnp.dot(a_, b_, preferred_element_type=F32)
    nt = lambda a_, b_: lax.dot_general(a_, b_, (((1,), (1,)), ((), ())), preferred_element_type=F32)
    tn = lambda a_, b_: lax.dot_general(a_, b_, (((0,), (0,)), ((), ())), preferred_element_type=F32)
    zero = jnp.zeros((), BF16)

    def bd(x_):
        x_ = x_.astype(BF16)
        return cat0(jnp.where(lo, x_, zero), jnp.where(lo, zero, x_))

    def diag_blocks(x_):
        sel = lo if x_.shape[1] == P2 else jnp.concatenate([lo] * (x_.shape[1] // P2), axis=1)
        return jnp.where(sel, x_[:HEAD_DIM], x_[HEAD_DIM:])

    def per_unit(arr, dtype=BF16):
        arr = arr.astype(dtype)
        return [arr[ch * L:(ch + 1) * L, pr * P2:(pr + 1) * P2] for ch, pr in units]

    qh, rh, kth, bth, vh = per_unit(qt), per_unit(rt), per_unit(kt), per_unit(bt), per_unit(v)
    khh, bhh = per_unit(kh), per_unit(bh)
    rf = per_unit(rt, F32)
    amat = [nt(cat0(q_, r_), cat0(bd(k_), bd(b_))) for q_, r_, k_, b_ in zip(qh, rh, kth, bth)]
    a_qk = [jnp.where(strict, m_[:L, :P2], 0.0).astype(BF16) for m_ in amat]
    a_qb = [jnp.where(strict, m_[:L, P2:], 0.0) for m_ in amat]
    a_rk = [jnp.where(incl, m_[L:, :P2], 0.0).astype(BF16) for m_ in amat]
    a_rb = [jnp.where(incl, m_[L:, P2:], 0.0).astype(BF16) for m_ in amat]
    tinv = [eye2 - m_ for m_ in a_qb]
    pw = [m_.astype(BF16) for m_ in a_qb]
    pw = [mm(p_, bd(p_)).astype(BF16) for p_ in pw]
    for _ in range(int(math.log2(L)) - 2):
        both = [mm(cat0(t_.astype(BF16), p_), bd(p_)) for t_, p_ in zip(tinv, pw)]
        tinv = [t_ + b_[:L] for t_, b_ in zip(tinv, both)]
        pw = [b_[L:].astype(BF16) for b_ in both]
    tinv = [t_ + mm(t_.astype(BF16), bd(p_)) for t_, p_ in zip(tinv, pw)]
    av = [mm(cat0(ak_, ar_), bd(v_)) for ak_, ar_, v_ in zip(a_qk, a_rk, vh)]
    x = [mm(t_.astype(BF16), cat1(bd(q_), bd(av_[:L]))).astype(BF16)
         for t_, q_, av_ in zip(tinv, qh, av)]
    ry = [cat1(r_, av_[L:]) - mm(ab_, cat1(bd(x_[:, :P2]), bd(x_[:, P2:])))
          for r_, av_, ab_, x_ in zip(rf, av, a_rb, x)]
    bx = [diag_blocks(tn(b_, x_)) for b_, x_ in zip(bhh, x)]
    kv = [diag_blocks(tn(k_, v_)) for k_, v_ in zip(khh, vh)]
    mmat = [(eye2 * pl_all[ch * L:ch * L + 1, pr * P2:(pr + 1) * P2] - bx_[:, :P2]).astype(BF16)
            for (ch, pr), bx_ in zip(units, bx)]
    nmat = [kv_ - bx_[:, P2:] for kv_, bx_ in zip(kv, bx)]

    n_pairs = HEADS // 2
    assert n_chunks % 2 == 0
    comp = {}
    for ch in range(0, n_chunks, 2):
        for pr in range(n_pairs):
            u0, u1 = ch * n_pairs + pr, (ch + 1) * n_pairs + pr
            both = mm(mmat[u1], cat1(bd(mmat[u0]), bd(nmat[u0])))
            comp[ch, pr] = (both[:, :P2].astype(BF16), both[:, P2:] + nmat[u1])
    h_cur = [h_ref[:, pr * P2:(pr + 1) * P2] for pr in range(n_pairs)]
    y_rows = [None] * n_chunks
    for ch in range(0, n_chunks, 2):
        y0, y1, h_mid = [], [], []
        for pr in range(n_pairs):
            u0 = ch * n_pairs + pr
            m2, n2 = comp[ch, pr]
            both = mm(cat0(ry[u0][:, :P2].astype(BF16), mmat[u0], m2), bd(h_cur[pr]))
            y0.append(both[:L] + ry[u0][:, P2:])
            h_mid.append(both[L:2 * L] + nmat[u0])
            h_cur[pr] = both[2 * L:] + n2
        for pr in range(n_pairs):
            u1 = (ch + 1) * n_pairs + pr
            y1.append(mm(ry[u1][:, :P2].astype(BF16), bd(h_mid[pr])) + ry[u1][:, P2:])
        y_rows[ch], y_rows[ch + 1] = cat1(*y0), cat1(*y1)
    h_ref[...] = cat1(*h_cur)
    y = cat0(*y_rows)

    inv_n = 1.0 / HEAD_DIM
    ym = _head_sums(y, seg) * inv_n
    yc = y - ym
    yv = _head_sums(yc * yc, seg) * inv_n
    y = yc * lax.rsqrt(yv + LNX_EPS)
    y = y * lg_ref[...] + lb_ref[...]
    y = y + _head_sums(r * kadj * rk_ref[...], seg) * v
    out_ref[...] = y * g


def _rwkv(has_vres, zr, zv, v_first, prm, consts, n_chunks):
    B, S, _ = zr.shape
    tt = n_chunks * CHUNK
    W = RWKV_WIDTH
    tile = lambda n: pl.BlockSpec((None, tt, n), lambda b, c: (b, c, 0))
    full = lambda a: pl.BlockSpec(a.shape, lambda b, c: (0,) * a.ndim)
    names = ["mu", "w0", "w_up", "a0", "a_up", "g_up", "k_k", "k_a", "r_k", "lnx_g", "lnx_b"]
    if has_vres:
        names += ["mu_vres", "v0", "vres_up"]
    params = [prm[n] for n in names] + list(consts)
    if has_vres:
        acts = [zr, zv, v_first]
        act_specs = [tile(RWKV_COLS), tile(VRES_PAD), tile(W)]
        out_shape = [jax.ShapeDtypeStruct((B, S, W), F32)]
        out_specs = [tile(W)]
        scratch = [pltpu.VMEM((HEAD_DIM, W), F32), pltpu.VMEM((1, RWKV_COLS), F32),
                   pltpu.VMEM((1, VRES_PAD), F32)]
    else:
        acts = [zr]
        act_specs = [tile(RWKV_COLS)]
        out_shape = [jax.ShapeDtypeStruct((B, S, W), F32)] * 2
        out_specs = [tile(W), tile(W)]
        scratch = [pltpu.VMEM((HEAD_DIM, W), F32), pltpu.VMEM((1, RWKV_COLS), F32)]
    return pl.pallas_call(
        functools.partial(_rwkv_kernel, has_vres, n_chunks),
        grid=(B, S // tt),
        in_specs=act_specs + [full(a) for a in params],
        out_specs=out_specs,
        out_shape=out_shape,
        scratch_shapes=scratch,
        compiler_params=pltpu.CompilerParams(
            dimension_semantics=("arbitrary", "arbitrary"), vmem_limit_bytes=VMEM_LIMIT),
        name="rwkv_vres" if has_vres else "rwkv",
    )(*acts, *params)


def _swa_kernel(n_blk, q_ref, kv_ref, kvp_ref, bkt_ref, relb_ref, sink_ref, o_ref, bias_ref):
    b_idx = pl.program_id(0)
    c_idx = pl.program_id(1)
    G2 = 2 * BLOCK

    @pl.when((b_idx == 0) & (c_idx == 0))
    def _build_bias():
        bkt = bkt_ref[...]
        for h in range(SWA_Q_HEADS):
            acc = jnp.full(bkt.shape, MASK_VALUE, F32)
            for k in range(REL_BUCKETS):
                acc = jnp.where(bkt == k, relb_ref[k, h], acc)
            pair, half = divmod(h, 2)
            bias_ref[pair * BLOCK:(pair + 1) * BLOCK, half * G2:(half + 1) * G2] = acc

    lo = lax.broadcasted_iota(jnp.int32, (1, LANES), 1) < HEAD_DIM
    ones_lo = jnp.broadcast_to(lo.astype(BF16), (G2, LANES))
    ones_hi = jnp.broadcast_to((~lo).astype(BF16), (G2, LANES))
    top = lax.broadcasted_iota(jnp.int32, (2 * BLOCK, 1), 0) < BLOCK
    prev_col = (lax.broadcasted_iota(jnp.int32, (2 * BLOCK, 2 * G2), 1) % G2) < BLOCK
    units = [(j, g) for j in range(n_blk) for g in range(SWA_KV_HEADS)]

    kk, vv = {}, {}
    for j in range(n_blk):
        if j == 0:
            kv = jnp.concatenate([kvp_ref[...], kv_ref[0:BLOCK, :]], axis=0)
        else:
            kv = kv_ref[(j - 1) * BLOCK:(j + 1) * BLOCK, :]
        kcol, vcol = kv[:, :LANES], kv[:, LANES:]
        for arr, dst, extra in ((kcol, kk, None), (vcol, vv, (ones_lo, ones_hi))):
            g0a = jnp.where(lo, arr, 0.0)
            g1b = jnp.where(lo, 0.0, arr)
            g0b = pltpu.roll(g0a, HEAD_DIM, 1)
            g1a = pltpu.roll(g1b, HEAD_DIM, 1)
            for g, (xa, xb) in enumerate(((g0a, g0b), (g1a, g1b))):
                xa, xb = xa.astype(BF16), xb.astype(BF16)
                if extra is None:
                    dst[j, g] = jnp.concatenate([xa, xb], axis=0)
                else:
                    dst[j, g] = jnp.concatenate(
                        [jnp.concatenate([xa, extra[0]], axis=1),
                         jnp.concatenate([xb, extra[1]], axis=1)], axis=0)

    def q_pairs(j, g):
        q2 = (q_ref[j * BLOCK:(j + 1) * BLOCK, g * 2 * LANES:(g + 1) * 2 * LANES] * ATTN_SCALE).astype(BF16)
        return jnp.concatenate([q2[:, :LANES], q2[:, LANES:]], axis=0)

    logits = [lax.dot_general(q_pairs(j, g), kk[j, g], (((1,), (1,)), ((), ())), preferred_element_type=F32)
              + bias_ref[g * 2 * BLOCK:(g + 1) * 2 * BLOCK, :] for j, g in units]
    logits = [jnp.where(prev_col & (c_idx == 0), MASK_VALUE, l_) if j == 0 else l_
              for (j, g), l_ in zip(units, logits)]
    sinks = [[jnp.where(top, sink_ref[4 * g + half], sink_ref[4 * g + 2 + half]) for half in range(2)]
             for j, g in units]
    ms = [[jnp.maximum(jnp.max(l_[:, half * G2:(half + 1) * G2], axis=-1, keepdims=True), s_[half])
           for half in range(2)] for l_, s_ in zip(logits, sinks)]
    es = [jnp.concatenate([jnp.exp(l_[:, half * G2:(half + 1) * G2] - m_[half]) for half in range(2)],
                          axis=1).astype(BF16) for l_, m_ in zip(logits, ms)]
    res = [jnp.dot(e_, vv[u], preferred_element_type=F32) for e_, u in zip(es, units)]
    for (j, g), r_, m_, s_ in zip(units, res, ms, sinks):
        den = r_[:, LANES:] + jnp.where(lo, jnp.exp(s_[0] - m_[0]), jnp.exp(s_[1] - m_[1]))
        out = r_[:, :LANES] / den
        rows = slice(j * BLOCK, (j + 1) * BLOCK)
        o_ref[rows, (2 * g) * LANES:(2 * g + 1) * LANES] = out[:BLOCK]
        o_ref[rows, (2 * g + 1) * LANES:(2 * g + 2) * LANES] = out[BLOCK:]


def _swa(zq, zkv, bkt, rel_bias, sinks):
    B, S, _ = zq.shape
    n_blk = TQ_SWA // BLOCK
    smem = lambda a: pl.BlockSpec(a.shape, lambda b, c: (0,) * a.ndim, memory_space=pltpu.SMEM)
    return pl.pallas_call(
        functools.partial(_swa_kernel, n_blk),
        grid=(B, S // TQ_SWA),
        in_specs=[
            pl.BlockSpec((None, TQ_SWA, SWA_WIDTH), lambda b, c: (b, c, 0)),
            pl.BlockSpec((None, TQ_SWA, 2 * SWA_KV_WIDTH), lambda b, c: (b, c, 0)),
            pl.BlockSpec((None, BLOCK, 2 * SWA_KV_WIDTH),
                         lambda b, c: (b, jnp.maximum(c * n_blk - 1, 0), 0)),
            pl.BlockSpec(bkt.shape, lambda b, c: (0, 0)),
            smem(rel_bias),
            smem(sinks),
        ],
        out_specs=pl.BlockSpec((None, TQ_SWA, SWA_WIDTH), lambda b, c: (b, c, 0)),
        out_shape=jax.ShapeDtypeStruct((B, S, SWA_WIDTH), F32),
        scratch_shapes=[pltpu.VMEM((SWA_Q_HEADS // 2 * BLOCK, 2 * 2 * BLOCK), F32)],
        compiler_params=pltpu.CompilerParams(
            dimension_semantics=("arbitrary", "arbitrary"), vmem_limit_bytes=VMEM_LIMIT),
        name="swa",
    )(zq, zkv, zkv, bkt, rel_bias, sinks)


def _post_kernel(final, x_ref, ro_ref, so_ref, p_ref, wo_ref, gm_ref, up_ref, dn_ref, pp_ref, pg_ref,
                 gate_ref, fg_ref, o_ref):
    W = RWKV_WIDTH
    x = x_ref[...]
    x = x + _bdot(ro_ref[...], wo_ref[0:W, :]) + _bdot(so_ref[...], wo_ref[W:2 * W, :])
    u = _rms(x, gm_ref[...]).astype(BF16)
    acc = jnp.zeros_like(x)
    for c in range(D_FF // FF_CHUNK):
        cs = slice(c * FF_CHUNK, (c + 1) * FF_CHUNK)
        hid = jnp.dot(u, up_ref[:, cs], preferred_element_type=F32)
        hid = jnp.square(jnp.maximum(hid, 0.0))
        acc = acc + jnp.dot(hid.astype(BF16), dn_ref[cs, :], preferred_element_type=F32)
    x = x + acc
    e = _rms(_bdot(p_ref[...], pp_ref[...]), pg_ref[...])
    x = x + e * _sigmoid(_bdot(x, gate_ref[...]))
    if final:
        x = _rms(x, fg_ref[...])
    o_ref[...] = x


def _post(final, x2, ro, so, p2, wo, gm, up, dn, pp, pg, gate, fg):
    T = x2.shape[0]
    tile = lambda n: pl.BlockSpec((TM_PROJ, n), lambda i: (i, 0))
    res = lambda a: pl.BlockSpec(a.shape, lambda i: (0, 0), pipeline_mode=pl.Buffered(1))
    weights = [wo, gm, up, dn, pp, pg, gate, fg]
    return pl.pallas_call(
        functools.partial(_post_kernel, final),
        grid=(T // TM_PROJ,),
        in_specs=[tile(D_MODEL), tile(RWKV_WIDTH), tile(SWA_WIDTH), tile(PLE_DIM)] + [res(a) for a in weights],
        out_specs=tile(D_MODEL),
        out_shape=jax.ShapeDtypeStruct((T, D_MODEL), F32),
        compiler_params=pltpu.CompilerParams(
            dimension_semantics=("arbitrary",), vmem_limit_bytes=VMEM_LIMIT),
        name="post_final" if final else "post",
    )(x2, ro, so, p2, *weights)


def _bucket_tile():
    max_exact = REL_BUCKETS // 2
    dist = (np.arange(BLOCK)[:, None] + BLOCK) - np.arange(2 * BLOCK)[None, :]
    n = np.maximum(dist, 0)
    nf = np.maximum(n, 1).astype(np.float32)
    scaled = (np.log(nf / np.float32(max_exact)) / np.float32(math.log(REL_MAX_DIST / max_exact))
              * np.float32(REL_BUCKETS - max_exact)).astype(np.float32)
    large = np.minimum(max_exact + scaled.astype(np.int32), REL_BUCKETS - 1)
    bucket = np.where(n < max_exact, n, large)
    valid = (dist >= 0) & (dist < WINDOW)
    return np.where(valid, bucket, -1).astype(np.int32)


def _scan_consts(tt):
    t = np.arange(tt)
    same = (t[:, None] // CHUNK) == (t[None, :] // CHUNK)
    tril = same & (t[:, None] >= t[None, :])
    hh = np.arange(4 * HEAD_DIM) // HEAD_DIM
    seg = hh[:, None] == hh[None, :]
    return (jnp.asarray(seg, BF16), jnp.asarray(np.concatenate([tril, tril], axis=1), BF16))


def kernel(x, p, norm_mix_g, w_in, mu_shift, w0, w_up, a0, a_up, g_up, vres_down, mu_vres, v0, vres_up,
           k_k, k_a, r_k, lnx_g, lnx_b, sinks, rel_bias, w_out, norm_mlp_g, w_ff_up, w_ff_down,
           ple_proj, ple_norm_g, ple_gate, final_norm_g):
    B, S, _ = x.shape
    depth = w_in.shape[0]
    T = B * S
    n_chunks = SCAN_CHUNKS_PER_STEP
    row = lambda a: a.reshape(1, -1).astype(F32)

    bkt = jnp.asarray(_bucket_tile())
    consts = _scan_consts(n_chunks * CHUNK)

    x2 = x.reshape(T, D_MODEL)
    v_first = None
    for i in range(depth):
        w_cols = w_in[i]
        widths = [RWKV_COLS, SWA_WIDTH, 2 * SWA_KV_WIDTH]
        if i > 0:
            pad = jnp.zeros((D_MODEL, VRES_PAD - MV_LORA), w_cols.dtype)
            w_cols = jnp.concatenate([w_cols, vres_down[i - 1], pad], axis=1)
            widths.append(VRES_PAD)
        zs = _inproj(x2, row(norm_mix_g[i]), w_cols.astype(BF16), widths)
        zr = zs[0].reshape(B, S, RWKV_COLS)
        zq = zs[1].reshape(B, S, SWA_WIDTH)
        zkv = zs[2].reshape(B, S, 2 * SWA_KV_WIDTH)

        prm = {
            "mu": row(mu_shift[i]), "w0": row(w0[i]), "w_up": w_up[i].astype(BF16), "a0": row(a0[i]),
            "a_up": a_up[i].astype(BF16), "g_up": g_up[i].astype(BF16), "k_k": row(k_k[i]),
            "k_a": row(k_a[i]), "r_k": row(r_k[i]), "lnx_g": row(lnx_g[i]), "lnx_b": row(lnx_b[i]),
        }
        if i == 0:
            rwkv_out, v_first = _rwkv(False, zr, None, None, prm, consts, n_chunks)
        else:
            prm["mu_vres"] = jnp.pad(row(mu_vres[i - 1]), ((0, 0), (0, VRES_PAD - MV_LORA)))
            prm["v0"] = row(v0[i - 1])
            prm["vres_up"] = jnp.pad(vres_up[i - 1], ((0, VRES_PAD - MV_LORA), (0, 0))).astype(BF16)
            zv = zs[3].reshape(B, S, VRES_PAD)
            (rwkv_out,) = _rwkv(True, zr, zv, v_first, prm, consts, n_chunks)

        swa_out = _swa(zq, zkv, bkt, rel_bias.astype(F32), sinks[i].astype(F32))

        x2 = _post(i == depth - 1, x2, rwkv_out.reshape(T, RWKV_WIDTH), swa_out.reshape(T, SWA_WIDTH),
                   p[i].reshape(T, PLE_DIM), w_out[i].astype(BF16), row(norm_mlp_g[i]),
                   w_ff_up[i].astype(BF16), w_ff_down[i].astype(BF16), ple_proj[i].astype(BF16),
                   row(ple_norm_g[i]), ple_gate[i].astype(BF16), row(final_norm_g))
    return x2.reshape(B, S, D_MODEL)
```

```python
import functools
import math

import jax
import jax.numpy as jnp
import numpy as np
from jax import lax
from jax.experimental import pallas as pl
from jax.experimental.pallas import tpu as pltpu

F32 = jnp.float32
BF16 = jnp.bfloat16

D_MODEL = 1024
PLE_DIM = 256
HEADS = 8
HEAD_DIM = 64
RWKV_WIDTH = HEADS * HEAD_DIM
DECAY_LORA = 64
AAA_LORA = 64
MV_LORA = 32
GATE_LORA = 128
LNX_EPS = 64e-5
SWA_Q_HEADS = 8
SWA_KV_HEADS = 2
SWA_GROUP = SWA_Q_HEADS // SWA_KV_HEADS
SWA_WIDTH = SWA_Q_HEADS * HEAD_DIM
SWA_KV_WIDTH = SWA_KV_HEADS * HEAD_DIM
WINDOW = 128
BLOCK = 128
ATTN_SCALE = 1.0 / math.sqrt(HEAD_DIM)
REL_BUCKETS = 32
REL_MAX_DIST = 128
D_FF = 4 * D_MODEL
NORM_EPS = 1e-6
RWKV_COLS = 3 * RWKV_WIDTH + DECAY_LORA + AAA_LORA + GATE_LORA
SWA_COLS = SWA_WIDTH + 2 * SWA_KV_WIDTH
IN_COLS = RWKV_COLS + SWA_COLS

LANES = 128
VRES_PAD = LANES
CHUNK = 64
SCAN_CHUNKS_PER_STEP = 4
MASK_VALUE = -1e30

TM_PROJ = 512
TQ_SWA = 2048
FF_CHUNK = 1024
VMEM_LIMIT = 56 * 1024 * 1024


def _rms(x, g):
    ms = jnp.mean(x * x, axis=-1, keepdims=True)
    return (x * lax.rsqrt(ms + NORM_EPS)) * g


def _bdot(a, b):
    return jnp.dot(a.astype(BF16), b.astype(BF16), preferred_element_type=F32)


def _split2(x):
    hi = x.astype(BF16)
    lo = (x - hi.astype(F32)).astype(BF16)
    return hi, lo


def _dot01_left(m01x2, x):
    hi, lo = _split2(x)
    return jnp.dot(m01x2, jnp.concatenate([hi, lo], axis=0), preferred_element_type=F32)


def _head_sums(x, seg):
    half = seg.shape[0]
    return jnp.concatenate(
        [jnp.dot(x[:, c * half:(c + 1) * half].astype(BF16), seg, preferred_element_type=F32)
         for c in range(x.shape[1] // half)], axis=1)


def _inproj_kernel(has_vres, x_ref, g_ref, w_ref, *refs):
    h = _rms(x_ref[...], g_ref[...]).astype(BF16)
    out_refs = refs[1:] if has_vres else refs
    off = 0
    for o_ref in out_refs[:3]:
        n = o_ref.shape[-1]
        o_ref[...] = jnp.dot(h, w_ref[:, off:off + n], preferred_element_type=F32)
        off += n
    if has_vres:
        out_refs[3][...] = jnp.dot(h, refs[0][...], preferred_element_type=F32)


def _inproj(layer, x2, g, w_all, w_vres):
    T = x2.shape[0]
    widths = [RWKV_COLS, SWA_WIDTH, 2 * SWA_KV_WIDTH]
    in_specs = [
        pl.BlockSpec((TM_PROJ, D_MODEL), lambda i: (i, 0)),
        pl.BlockSpec((1, D_MODEL), lambda i: (0, 0)),
        pl.BlockSpec((None, D_MODEL, IN_COLS), lambda i: (layer, 0, 0)),
    ]
    args = [x2, g, w_all]
    if w_vres is not None:
        widths.append(VRES_PAD)
        in_specs.append(pl.BlockSpec(w_vres.shape, lambda i: (0, 0)))
        args.append(w_vres)
    return pl.pallas_call(
        functools.partial(_inproj_kernel, w_vres is not None),
        grid=(T // TM_PROJ,),
        in_specs=in_specs,
        out_specs=[pl.BlockSpec((TM_PROJ, n), lambda i: (i, 0)) for n in widths],
        out_shape=[jax.ShapeDtypeStruct((T, n), F32) for n in widths],
        compiler_params=pltpu.CompilerParams(
            dimension_semantics=("arbitrary",), vmem_limit_bytes=VMEM_LIMIT),
        name="inproj",
    )(*args)


def _sigmoid(x):
    return 0.5 * jnp.tanh(0.5 * x) + 0.5


def _shift(z, prev_row, mu):
    rolled = pltpu.roll(z, 1, 0)
    row = lax.broadcasted_iota(jnp.int32, z.shape, 0)
    prev = jnp.where(row == 0, prev_row, rolled)
    return z + (prev - z) * mu


def _rwkv_kernel(has_vres, n_chunks, *refs):
    if has_vres:
        (zr_ref, zv_ref, vf_ref, mu_ref, w0_ref, wup_ref, a0_ref, aup_ref, gup_ref, kk_ref, ka_ref,
         rk_ref, lg_ref, lb_ref, muv_ref, v0_ref, vup_ref, seg_ref, tril_ref,
         out_ref, h_ref, prev_ref, prevv_ref) = refs
    else:
        (zr_ref, mu_ref, w0_ref, wup_ref, a0_ref, aup_ref, gup_ref, kk_ref, ka_ref,
         rk_ref, lg_ref, lb_ref, seg_ref, tril_ref,
         out_ref, vf_out_ref, h_ref, prev_ref) = refs

    c_idx = pl.program_id(1)

    @pl.when(c_idx == 0)
    def _():
        h_ref[...] = jnp.zeros_like(h_ref)
        prev_ref[...] = jnp.zeros_like(prev_ref)
        if has_vres:
            prevv_ref[...] = jnp.zeros_like(prevv_ref)

    W = RWKV_WIDTH
    z = zr_ref[...]
    tt = z.shape[0]
    zs = _shift(z, prev_ref[...], mu_ref[...])
    prev_ref[...] = z[tt - 1:tt, :]
    r = zs[:, 0:W]
    k = zs[:, W:2 * W]
    v = zs[:, 2 * W:3 * W]
    o1 = 3 * W
    xw = zs[:, o1:o1 + DECAY_LORA]
    xa = zs[:, o1 + DECAY_LORA:o1 + DECAY_LORA + AAA_LORA]
    xg = zs[:, o1 + DECAY_LORA + AAA_LORA:]

    ld = -math.exp(-0.5) * _sigmoid(w0_ref[...] + _bdot(jnp.tanh(xw), wup_ref[...]))
    a = _sigmoid(a0_ref[...] + _bdot(xa, aup_ref[...]))
    g = _bdot(_sigmoid(xg), gup_ref[...])
    if has_vres:
        zv = zv_ref[...]
        zvs = _shift(zv, prevv_ref[...], muv_ref[...])
        prevv_ref[...] = zv[tt - 1:tt, :]
        v = v + (vf_ref[...] - v) * _sigmoid(v0_ref[...] + _bdot(zvs, vup_ref[...]))
    else:
        vf_out_ref[...] = v

    seg = seg_ref[...]
    kk = k * kk_ref[...]
    kk = kk * jnp.minimum(lax.rsqrt(_head_sums(kk * kk, seg)), 1e12)
    kadj = k * (1.0 + (a - 1.0) * ka_ref[...])
    b = kk * a

    c = _dot01_left(tril_ref[...], ld)
    cl = jnp.concatenate(
        [jnp.broadcast_to(c[(ch + 1) * CHUNK - 1:(ch + 1) * CHUNK, :], (CHUNK, W)) for ch in range(n_chunks)],
        axis=0)
    qt = kk * jnp.exp(c - ld)
    rt = r * jnp.exp(c)
    en = jnp.exp(-c)
    kt = kadj * en
    bt = b * en
    el = jnp.exp(cl - c)
    kh = kadj * el
    bh = b * el
    pl_all = jnp.exp(cl)

    L = CHUNK
    P2 = 2 * HEAD_DIM
    lo = lax.broadcasted_iota(jnp.int32, (1, P2), 1) < HEAD_DIM
    ri = lax.broadcasted_iota(jnp.int32, (L, P2), 0)
    ci = lax.broadcasted_iota(jnp.int32, (L, P2), 1) % HEAD_DIM
    strict = ri > ci
    incl = ri >= ci
    eye2 = (ri == ci).astype(F32)
    units = [(ch, pr) for ch in range(n_chunks) for pr in range(HEADS // 2)]

    cat0 = lambda *xs: jnp.concatenate(xs, axis=0)
    cat1 = lambda *xs: jnp.concatenate(xs, axis=1)
    mm = lambda a_, b_: jnp.dot(a_, b_, preferred_element_type=F32)
    nt = lambda a_, b_: lax.dot_general(a_, b_, (((1,), (1,)), ((), ())), preferred_element_type=F32)
    tn = lambda a_, b_: lax.dot_general(a_, b_, (((0,), (0,)), ((), ())), preferred_element_type=F32)
    zero = jnp.zeros((), BF16)

    def bd(x_):
        x_ = x_.astype(BF16)
        return cat0(jnp.where(lo, x_, zero), jnp.where(lo, zero, x_))

    def diag_blocks(x_):
        sel = lo if x_.shape[1] == P2 else jnp.concatenate([lo] * (x_.shape[1] // P2), axis=1)
        return jnp.where(sel, x_[:HEAD_DIM], x_[HEAD_DIM:])

    def per_unit(arr, dtype=BF16):
        arr = arr.astype(dtype)
        return [arr[ch * L:(ch + 1) * L, pr * P2:(pr + 1) * P2] for ch, pr in units]

    qh, rh, kth, bth, vh = per_unit(qt), per_unit(rt), per_unit(kt), per_unit(bt), per_unit(v)
    khh, bhh = per_unit(kh), per_unit(bh)
    rf = per_unit(rt, F32)
    amat = [nt(cat0(q_, r_), cat0(bd(k_), bd(b_))) for q_, r_, k_, b_ in zip(qh, rh, kth, bth)]
    a_qk = [jnp.where(strict, m_[:L, :P2], 0.0).astype(BF16) for m_ in amat]
    a_qb = [jnp.where(strict, m_[:L, P2:], 0.0) for m_ in amat]
    a_rk = [jnp.where(incl, m_[L:, :P2], 0.0).astype(BF16) for m_ in amat]
    a_rb = [jnp.where(incl, m_[L:, P2:], 0.0).astype(BF16) for m_ in amat]
    tinv = [eye2 - m_ for m_ in a_qb]
    pw = [m_.astype(BF16) for m_ in a_qb]
    pw = [mm(p_, bd(p_)).astype(BF16) for p_ in pw]
    for _ in range(int(math.log2(L)) - 2):
        both = [mm(cat0(t_.astype(BF16), p_), bd(p_)) for t_, p_ in zip(tinv, pw)]
        tinv = [t_ + b_[:L] for t_, b_ in zip(tinv, both)]
        pw = [b_[L:].astype(BF16) for b_ in both]
    tinv = [t_ + mm(t_.astype(BF16), bd(p_)) for t_, p_ in zip(tinv, pw)]
    av = [mm(cat0(ak_, ar_), bd(v_)) for ak_, ar_, v_ in zip(a_qk, a_rk, vh)]
    x = [mm(t_.astype(BF16), cat1(bd(q_), bd(av_[:L]))).astype(BF16)
         for t_, q_, av_ in zip(tinv, qh, av)]
    ry = [cat1(r_, av_[L:]) - mm(ab_, cat1(bd(x_[:, :P2]), bd(x_[:, P2:])))
          for r_, av_, ab_, x_ in zip(rf, av, a_rb, x)]
    bx = [diag_blocks(tn(b_, x_)) for b_, x_ in zip(bhh, x)]
    kv = [diag_blocks(tn(k_, v_)) for k_, v_ in zip(khh, vh)]
    mmat = [(eye2 * pl_all[ch * L:ch * L + 1, pr * P2:(pr + 1) * P2] - bx_[:, :P2]).astype(BF16)
            for (ch, pr), bx_ in zip(units, bx)]
    nmat = [kv_ - bx_[:, P2:] for kv_, bx_ in zip(kv, bx)]

    n_pairs = HEADS // 2
    assert n_chunks % 2 == 0
    comp = {}
    for ch in range(0, n_chunks, 2):
        for pr in range(n_pairs):
            u0, u1 = ch * n_pairs + pr, (ch + 1) * n_pairs + pr
            both = mm(mmat[u1], cat1(bd(mmat[u0]), bd(nmat[u0])))
            comp[ch, pr] = (both[:, :P2].astype(BF16), both[:, P2:] + nmat[u1])
    h_cur = [h_ref[:, pr * P2:(pr + 1) * P2] for pr in range(n_pairs)]
    y_rows = [None] * n_chunks
    for ch in range(0, n_chunks, 2):
        y0, y1, h_mid = [], [], []
        for pr in range(n_pairs):
            u0 = ch * n_pairs + pr
            m2, n2 = comp[ch, pr]
            both = mm(cat0(ry[u0][:, :P2].astype(BF16), mmat[u0], m2), bd(h_cur[pr]))
            y0.append(both[:L] + ry[u0][:, P2:])
            h_mid.append(both[L:2 * L] + nmat[u0])
            h_cur[pr] = both[2 * L:] + n2
        for pr in range(n_pairs):
            u1 = (ch + 1) * n_pairs + pr
            y1.append(mm(ry[u1][:, :P2].astype(BF16), bd(h_mid[pr])) + ry[u1][:, P2:])
        y_rows[ch], y_rows[ch + 1] = cat1(*y0), cat1(*y1)
    h_ref[...] = cat1(*h_cur)
    y = cat0(*y_rows)

    inv_n = 1.0 / HEAD_DIM
    ym = _head_sums(y, seg) * inv_n
    yc = y - ym
    yv = _head_sums(yc * yc, seg) * inv_n
    y = yc * lax.rsqrt(yv + LNX_EPS)
    y = y * lg_ref[...] + lb_ref[...]
    y = y + _head_sums(r * kadj * rk_ref[...], seg) * v
    out_ref[...] = y * g


def _rwkv(has_vres, zr, zv, v_first, prm, consts, n_chunks):
    B, S, _ = zr.shape
    tt = n_chunks * CHUNK
    W = RWKV_WIDTH
    tile = lambda n: pl.BlockSpec((None, tt, n), lambda b, c: (b, c, 0))
    full = lambda a: pl.BlockSpec(a.shape, lambda b, c: (0,) * a.ndim)
    names = ["mu", "w0", "w_up", "a0", "a_up", "g_up", "k_k", "k_a", "r_k", "lnx_g", "lnx_b"]
    if has_vres:
        names += ["mu_vres", "v0", "vres_up"]
    params = [prm[n] for n in names] + list(consts)
    if has_vres:
        acts = [zr, zv, v_first]
        act_specs = [tile(RWKV_COLS), tile(VRES_PAD), tile(W)]
        out_shape = [jax.ShapeDtypeStruct((B, S, W), F32)]
        out_specs = [tile(W)]
        scratch = [pltpu.VMEM((HEAD_DIM, W), F32), pltpu.VMEM((1, RWKV_COLS), F32),
                   pltpu.VMEM((1, VRES_PAD), F32)]
    else:
        acts = [zr]
        act_specs = [tile(RWKV_COLS)]
        out_shape = [jax.ShapeDtypeStruct((B, S, W), F32)] * 2
        out_specs = [tile(W), tile(W)]
        scratch = [pltpu.VMEM((HEAD_DIM, W), F32), pltpu.VMEM((1, RWKV_COLS), F32)]
    return pl.pallas_call(
        functools.partial(_rwkv_kernel, has_vres, n_chunks),
        grid=(B, S // tt),
        in_specs=act_specs + [full(a) for a in params],
        out_specs=out_specs,
        out_shape=out_shape,
        scratch_shapes=scratch,
        compiler_params=pltpu.CompilerParams(
            dimension_semantics=("arbitrary", "arbitrary"), vmem_limit_bytes=VMEM_LIMIT),
        name="rwkv_vres" if has_vres else "rwkv",
    )(*acts, *params)


def _swa_kernel(n_blk, q_ref, kv_ref, kvp_ref, bkt_ref, relb_ref, sink_ref, o_ref, bias_ref):
    b_idx = pl.program_id(0)
    c_idx = pl.program_id(1)
    G2 = 2 * BLOCK

    @pl.when((b_idx == 0) & (c_idx == 0))
    def _build_bias():
        bkt = bkt_ref[...]
        for h in range(SWA_Q_HEADS):
            acc = jnp.full(bkt.shape, MASK_VALUE, F32)
            for k in range(REL_BUCKETS):
                acc = jnp.where(bkt == k, relb_ref[k, h], acc)
            pair, half = divmod(h, 2)
            bias_ref[pair * BLOCK:(pair + 1) * BLOCK, half * G2:(half + 1) * G2] = acc

    lo = lax.broadcasted_iota(jnp.int32, (1, LANES), 1) < HEAD_DIM
    ones_lo = jnp.broadcast_to(lo.astype(BF16), (G2, LANES))
    ones_hi = jnp.broadcast_to((~lo).astype(BF16), (G2, LANES))
    top = lax.broadcasted_iota(jnp.int32, (2 * BLOCK, 1), 0) < BLOCK
    prev_col = (lax.broadcasted_iota(jnp.int32, (2 * BLOCK, 2 * G2), 1) % G2) < BLOCK
    units = [(j, g) for j in range(n_blk) for g in range(SWA_KV_HEADS)]

    kk, vv = {}, {}
    for j in range(n_blk):
        if j == 0:
            kv = jnp.concatenate([kvp_ref[...], kv_ref[0:BLOCK, :]], axis=0)
        else:
            kv = kv_ref[(j - 1) * BLOCK:(j + 1) * BLOCK, :]
        kcol, vcol = kv[:, :LANES], kv[:, LANES:]
        for arr, dst, extra in ((kcol, kk, None), (vcol, vv, (ones_lo, ones_hi))):
            g0a = jnp.where(lo, arr, 0.0)
            g1b = jnp.where(lo, 0.0, arr)
            g0b = pltpu.roll(g0a, HEAD_DIM, 1)
            g1a = pltpu.roll(g1b, HEAD_DIM, 1)
            for g, (xa, xb) in enumerate(((g0a, g0b), (g1a, g1b))):
                xa, xb = xa.astype(BF16), xb.astype(BF16)
                if extra is None:
                    dst[j, g] = jnp.concatenate([xa, xb], axis=0)
                else:
                    dst[j, g] = jnp.concatenate(
                        [jnp.concatenate([xa, extra[0]], axis=1),
                         jnp.concatenate([xb, extra[1]], axis=1)], axis=0)

    def q_pairs(j, g):
        q2 = (q_ref[j * BLOCK:(j + 1) * BLOCK, g * 2 * LANES:(g + 1) * 2 * LANES] * ATTN_SCALE).astype(BF16)
        return jnp.concatenate([q2[:, :LANES], q2[:, LANES:]], axis=0)

    logits = [lax.dot_general(q_pairs(j, g), kk[j, g], (((1,), (1,)), ((), ())), preferred_element_type=F32)
              + bias_ref[g * 2 * BLOCK:(g + 1) * 2 * BLOCK, :] for j, g in units]
    logits = [jnp.where(prev_col & (c_idx == 0), MASK_VALUE, l_) if j == 0 else l_
              for (j, g), l_ in zip(units, logits)]
    sinks = [[jnp.where(top, sink_ref[4 * g + half], sink_ref[4 * g + 2 + half]) for half in range(2)]
             for j, g in units]
    ms = [[jnp.maximum(jnp.max(l_[:, half * G2:(half + 1) * G2], axis=-1, keepdims=True), s_[half])
           for half in range(2)] for l_, s_ in zip(logits, sinks)]
    es = [jnp.concatenate([jnp.exp(l_[:, half * G2:(half + 1) * G2] - m_[half]) for half in range(2)],
                          axis=1).astype(BF16) for l_, m_ in zip(logits, ms)]
    res = [jnp.dot(e_, vv[u], preferred_element_type=F32) for e_, u in zip(es, units)]
    for (j, g), r_, m_, s_ in zip(units, res, ms, sinks):
        den = r_[:, LANES:] + jnp.where(lo, jnp.exp(s_[0] - m_[0]), jnp.exp(s_[1] - m_[1]))
        out = r_[:, :LANES] / den
        rows = slice(j * BLOCK, (j + 1) * BLOCK)
        o_ref[rows, (2 * g) * LANES:(2 * g + 1) * LANES] = out[:BLOCK]
        o_ref[rows, (2 * g + 1) * LANES:(2 * g + 2) * LANES] = out[BLOCK:]


def _swa(zq, zkv, bkt, rel_bias, sinks):
    B, S, _ = zq.shape
    n_blk = TQ_SWA // BLOCK
    smem = lambda a: pl.BlockSpec(a.shape, lambda b, c: (0,) * a.ndim, memory_space=pltpu.SMEM)
    return pl.pallas_call(
        functools.partial(_swa_kernel, n_blk),
        grid=(B, S // TQ_SWA),
        in_specs=[
            pl.BlockSpec((None, TQ_SWA, SWA_WIDTH), lambda b, c: (b, c, 0)),
            pl.BlockSpec((None, TQ_SWA, 2 * SWA_KV_WIDTH), lambda b, c: (b, c, 0)),
            pl.BlockSpec((None, BLOCK, 2 * SWA_KV_WIDTH),
                         lambda b, c: (b, jnp.maximum(c * n_blk - 1, 0), 0)),
            pl.BlockSpec(bkt.shape, lambda b, c: (0, 0)),
            smem(rel_bias),
            smem(sinks),
        ],
        out_specs=pl.BlockSpec((None, TQ_SWA, SWA_WIDTH), lambda b, c: (b, c, 0)),
        out_shape=jax.ShapeDtypeStruct((B, S, SWA_WIDTH), F32),
        scratch_shapes=[pltpu.VMEM((SWA_Q_HEADS // 2 * BLOCK, 2 * 2 * BLOCK), F32)],
        compiler_params=pltpu.CompilerParams(
            dimension_semantics=("arbitrary", "arbitrary"), vmem_limit_bytes=VMEM_LIMIT),
        name="swa",
    )(zq, zkv, zkv, bkt, rel_bias, sinks)


def _post_kernel(final, x_ref, ro_ref, so_ref, p_ref, wo_ref, gm_ref, up_ref, dn_ref, pp_ref, pg_ref,
                 gate_ref, fg_ref, o_ref):
    W = RWKV_WIDTH
    x = x_ref[...]
    x = x + _bdot(ro_ref[...], wo_ref[0:W, :]) + _bdot(so_ref[...], wo_ref[W:2 * W, :])
    u = _rms(x, gm_ref[...]).astype(BF16)
    acc = jnp.zeros_like(x)
    for c in range(D_FF // FF_CHUNK):
        cs = slice(c * FF_CHUNK, (c + 1) * FF_CHUNK)
        hid = jnp.dot(u, up_ref[:, cs], preferred_element_type=F32)
        hid = jnp.square(jnp.maximum(hid, 0.0))
        acc = acc + jnp.dot(hid.astype(BF16), dn_ref[cs, :], preferred_element_type=F32)
    x = x + acc
    e = _rms(_bdot(p_ref[...], pp_ref[...]), pg_ref[...])
    x = x + e * _sigmoid(_bdot(x, gate_ref[...]))
    if final:
        x = _rms(x, fg_ref[...])
    o_ref[...] = x


def _post(layer, final, x2, ro, so, p_all, wo, gm, up, dn, pp, pg, gate, fg):
    T = x2.shape[0]
    n_tiles = T // TM_PROJ
    tile = lambda n: pl.BlockSpec((TM_PROJ, n), lambda i: (i, 0))
    p_tile = pl.BlockSpec((TM_PROJ, PLE_DIM), lambda i: (layer * n_tiles + i, 0))
    row = lambda a: pl.BlockSpec(a.shape, lambda i: (0, 0), pipeline_mode=pl.Buffered(1))
    mat = lambda a: pl.BlockSpec((None,) + a.shape[1:], lambda i: (layer, 0, 0), pipeline_mode=pl.Buffered(1))
    return pl.pallas_call(
        functools.partial(_post_kernel, final),
        grid=(n_tiles,),
        in_specs=[tile(D_MODEL), tile(RWKV_WIDTH), tile(SWA_WIDTH), p_tile,
                  mat(wo), row(gm), mat(up), mat(dn), mat(pp), row(pg), mat(gate), row(fg)],
        out_specs=tile(D_MODEL),
        out_shape=jax.ShapeDtypeStruct((T, D_MODEL), F32),
        compiler_params=pltpu.CompilerParams(
            dimension_semantics=("arbitrary",), vmem_limit_bytes=VMEM_LIMIT),
        name="post_final" if final else "post",
    )(x2, ro, so, p_all, wo, gm, up, dn, pp, pg, gate, fg)


def _bucket_tile():
    max_exact = REL_BUCKETS // 2
    dist = (np.arange(BLOCK)[:, None] + BLOCK) - np.arange(2 * BLOCK)[None, :]
    n = np.maximum(dist, 0)
    nf = np.maximum(n, 1).astype(np.float32)
    scaled = (np.log(nf / np.float32(max_exact)) / np.float32(math.log(REL_MAX_DIST / max_exact))
              * np.float32(REL_BUCKETS - max_exact)).astype(np.float32)
    large = np.minimum(max_exact + scaled.astype(np.int32), REL_BUCKETS - 1)
    bucket = np.where(n < max_exact, n, large)
    valid = (dist >= 0) & (dist < WINDOW)
    return np.where(valid, bucket, -1).astype(np.int32)


def _scan_consts(tt):
    t = np.arange(tt)
    same = (t[:, None] // CHUNK) == (t[None, :] // CHUNK)
    tril = same & (t[:, None] >= t[None, :])
    hh = np.arange(4 * HEAD_DIM) // HEAD_DIM
    seg = hh[:, None] == hh[None, :]
    return (jnp.asarray(seg, BF16), jnp.asarray(np.concatenate([tril, tril], axis=1), BF16))


def kernel(x, p, norm_mix_g, w_in, mu_shift, w0, w_up, a0, a_up, g_up, vres_down, mu_vres, v0, vres_up,
           k_k, k_a, r_k, lnx_g, lnx_b, sinks, rel_bias, w_out, norm_mlp_g, w_ff_up, w_ff_down,
           ple_proj, ple_norm_g, ple_gate, final_norm_g):
    B, S, _ = x.shape
    depth = w_in.shape[0]
    T = B * S
    n_chunks = SCAN_CHUNKS_PER_STEP
    row = lambda a: a.reshape(1, -1).astype(F32)

    bkt = jnp.asarray(_bucket_tile())
    consts = _scan_consts(n_chunks * CHUNK)

    w_in_b, w_out_b = w_in.astype(BF16), w_out.astype(BF16)
    up_b, dn_b = w_ff_up.astype(BF16), w_ff_down.astype(BF16)
    pp_b, gate_b = ple_proj.astype(BF16), ple_gate.astype(BF16)
    p_all = p.reshape(depth * T, PLE_DIM)

    x2 = x.reshape(T, D_MODEL)
    v_first = None
    for i in range(depth):
        w_vres = None
        if i > 0:
            w_vres = jnp.pad(vres_down[i - 1], ((0, 0), (0, VRES_PAD - MV_LORA))).astype(BF16)
        zs = _inproj(i, x2, row(norm_mix_g[i]), w_in_b, w_vres)
        zr = zs[0].reshape(B, S, RWKV_COLS)
        zq = zs[1].reshape(B, S, SWA_WIDTH)
        zkv = zs[2].reshape(B, S, 2 * SWA_KV_WIDTH)

        prm = {
            "mu": row(mu_shift[i]), "w0": row(w0[i]), "w_up": w_up[i].astype(BF16), "a0": row(a0[i]),
            "a_up": a_up[i].astype(BF16), "g_up": g_up[i].astype(BF16), "k_k": row(k_k[i]),
            "k_a": row(k_a[i]), "r_k": row(r_k[i]), "lnx_g": row(lnx_g[i]), "lnx_b": row(lnx_b[i]),
        }
        if i == 0:
            rwkv_out, v_first = _rwkv(False, zr, None, None, prm, consts, n_chunks)
        else:
            prm["mu_vres"] = jnp.pad(row(mu_vres[i - 1]), ((0, 0), (0, VRES_PAD - MV_LORA)))
            prm["v0"] = row(v0[i - 1])
            prm["vres_up"] = jnp.pad(vres_up[i - 1], ((0, VRES_PAD - MV_LORA), (0, 0))).astype(BF16)
            zv = zs[3].reshape(B, S, VRES_PAD)
            (rwkv_out,) = _rwkv(True, zr, zv, v_first, prm, consts, n_chunks)

        swa_out = _swa(zq, zkv, bkt, rel_bias.astype(F32), sinks[i].astype(F32))

        x2 = _post(i, i == depth - 1, x2, rwkv_out.reshape(T, RWKV_WIDTH), swa_out.reshape(T, SWA_WIDTH),
                   p_all, w_out_b, row(norm_mlp_g[i]), up_b, dn_b, pp_b, row(ple_norm_g[i]), gate_b,
                   row(final_norm_g))
    return x2.reshape(B, S, D_MODEL)
```

```python
import functools
import math

import jax
import jax.numpy as jnp
import numpy as np
from jax import lax
from jax.experimental import pallas as pl
from jax.experimental.pallas import tpu as pltpu

F32 = jnp.float32
BF16 = jnp.bfloat16

D_MODEL = 1024
PLE_DIM = 256
HEADS = 8
HEAD_DIM = 64
RWKV_WIDTH = HEADS * HEAD_DIM
DECAY_LORA = 64
AAA_LORA = 64
MV_LORA = 32
GATE_LORA = 128
LNX_EPS = 64e-5
SWA_Q_HEADS = 8
SWA_KV_HEADS = 2
SWA_WIDTH = SWA_Q_HEADS * HEAD_DIM
SWA_KV_WIDTH = SWA_KV_HEADS * HEAD_DIM
WINDOW = 128
BLOCK = 128
ATTN_SCALE = 1.0 / math.sqrt(HEAD_DIM)
REL_BUCKETS = 32
REL_MAX_DIST = 128
D_FF = 4 * D_MODEL
NORM_EPS = 1e-6
RWKV_COLS = 3 * RWKV_WIDTH + DECAY_LORA + AAA_LORA + GATE_LORA
SWA_COLS = SWA_WIDTH + 2 * SWA_KV_WIDTH
IN_COLS = RWKV_COLS + SWA_COLS

DECAY_OFFSET = 0.5
KK_NORM_FLOOR = 1e-12

LANES = 128
MXU_WIDTH = 256
VRES_PAD = LANES
CHUNK = 64
SCAN_CHUNKS_PER_STEP = 4
MASK_VALUE = -1e30

TM_PROJ = 512
TQ_SWA = 2048
FF_CHUNK = 1024
VMEM_LIMIT = 56 * 1024 * 1024


def _rms(x, g):
    ms = jnp.mean(x * x, axis=-1, keepdims=True)
    return (x * lax.rsqrt(ms + NORM_EPS)) * g


def _bdot(a, b):
    return jnp.dot(a.astype(BF16), b.astype(BF16), preferred_element_type=F32)


def _split2(x):
    hi = x.astype(BF16)
    lo = (x - hi.astype(F32)).astype(BF16)
    return hi, lo


def _dot01_left(m01x2, x):
    hi, lo = _split2(x)
    return jnp.dot(m01x2, jnp.concatenate([hi, lo], axis=0), preferred_element_type=F32)


def _head_sums(x, seg):
    half = seg.shape[0]
    return jnp.concatenate(
        [jnp.dot(x[:, c * half:(c + 1) * half].astype(BF16), seg, preferred_element_type=F32)
         for c in range(x.shape[1] // half)], axis=1)


def _inproj_kernel(has_vres, x_ref, g_ref, w_ref, *refs):
    h = _rms(x_ref[...], g_ref[...]).astype(BF16)
    out_refs = refs[1:] if has_vres else refs
    off = 0
    for o_ref in out_refs[:3]:
        n = o_ref.shape[-1]
        o_ref[...] = jnp.dot(h, w_ref[:, off:off + n], preferred_element_type=F32)
        off += n
    if has_vres:
        out_refs[3][...] = jnp.dot(h, refs[0][...], preferred_element_type=F32)


def _inproj(layer, x2, g, w_all, w_vres):
    T = x2.shape[0]
    widths = [RWKV_COLS, SWA_WIDTH, 2 * SWA_KV_WIDTH]
    in_specs = [
        pl.BlockSpec((TM_PROJ, D_MODEL), lambda i: (i, 0)),
        pl.BlockSpec((1, D_MODEL), lambda i: (0, 0)),
        pl.BlockSpec((None, D_MODEL, IN_COLS), lambda i: (layer, 0, 0)),
    ]
    args = [x2, g, w_all]
    if w_vres is not None:
        widths.append(VRES_PAD)
        in_specs.append(pl.BlockSpec(w_vres.shape, lambda i: (0, 0)))
        args.append(w_vres)
    return pl.pallas_call(
        functools.partial(_inproj_kernel, w_vres is not None),
        grid=(T // TM_PROJ,),
        in_specs=in_specs,
        out_specs=[pl.BlockSpec((TM_PROJ, n), lambda i: (i, 0)) for n in widths],
        out_shape=[jax.ShapeDtypeStruct((T, n), F32) for n in widths],
        compiler_params=pltpu.CompilerParams(
            dimension_semantics=("arbitrary",), vmem_limit_bytes=VMEM_LIMIT),
        name="inproj",
    )(*args)


def _sigmoid(x):
    return 0.5 * jnp.tanh(0.5 * x) + 0.5


def _shift(z, prev_row, mu):
    rolled = pltpu.roll(z, 1, 0)
    row = lax.broadcasted_iota(jnp.int32, z.shape, 0)
    prev = jnp.where(row == 0, prev_row, rolled)
    return z + (prev - z) * mu


def _rwkv_kernel(has_vres, n_chunks, *refs):
    if has_vres:
        (zr_ref, zv_ref, vf_ref, mu_ref, w0_ref, wup_ref, a0_ref, aup_ref, gup_ref, kk_ref, ka_ref,
         rk_ref, lg_ref, lb_ref, muv_ref, v0_ref, vup_ref, seg_ref, tril_ref,
         out_ref, h_ref, prev_ref, prevv_ref) = refs
    else:
        (zr_ref, mu_ref, w0_ref, wup_ref, a0_ref, aup_ref, gup_ref, kk_ref, ka_ref,
         rk_ref, lg_ref, lb_ref, seg_ref, tril_ref,
         out_ref, vf_out_ref, h_ref, prev_ref) = refs

    c_idx = pl.program_id(1)

    @pl.when(c_idx == 0)
    def _():
        h_ref[...] = jnp.zeros_like(h_ref)
        prev_ref[...] = jnp.zeros_like(prev_ref)
        if has_vres:
            prevv_ref[...] = jnp.zeros_like(prevv_ref)

    W = RWKV_WIDTH
    z = zr_ref[...]
    tt = z.shape[0]
    zs = _shift(z, prev_ref[...], mu_ref[...])
    prev_ref[...] = z[tt - 1:tt, :]
    r = zs[:, 0:W]
    k = zs[:, W:2 * W]
    v = zs[:, 2 * W:3 * W]
    o1 = 3 * W
    xw = zs[:, o1:o1 + DECAY_LORA]
    xa = zs[:, o1 + DECAY_LORA:o1 + DECAY_LORA + AAA_LORA]
    xg = zs[:, o1 + DECAY_LORA + AAA_LORA:]

    ld = -math.exp(-DECAY_OFFSET) * _sigmoid(w0_ref[...] + _bdot(jnp.tanh(xw), wup_ref[...]))
    a = _sigmoid(a0_ref[...] + _bdot(xa, aup_ref[...]))
    g = _bdot(_sigmoid(xg), gup_ref[...])
    if has_vres:
        zv = zv_ref[...]
        zvs = _shift(zv, prevv_ref[...], muv_ref[...])
        prevv_ref[...] = zv[tt - 1:tt, :]
        v = v + (vf_ref[...] - v) * _sigmoid(v0_ref[...] + _bdot(zvs, vup_ref[...]))
    else:
        vf_out_ref[...] = v

    seg = seg_ref[...]
    kk = k * kk_ref[...]
    kk = kk * jnp.minimum(lax.rsqrt(_head_sums(kk * kk, seg)), 1.0 / KK_NORM_FLOOR)
    kadj = k * (1.0 + (a - 1.0) * ka_ref[...])
    b = kk * a

    c = _dot01_left(tril_ref[...], ld)
    cl = jnp.concatenate(
        [jnp.broadcast_to(c[(ch + 1) * CHUNK - 1:(ch + 1) * CHUNK, :], (CHUNK, W)) for ch in range(n_chunks)],
        axis=0)
    qt = kk * jnp.exp(c - ld)
    rt = r * jnp.exp(c)
    en = jnp.exp(-c)
    kt = kadj * en
    bt = b * en
    el = jnp.exp(cl - c)
    kh = kadj * el
    bh = b * el
    pl_all = jnp.exp(cl)

    L = CHUNK
    P2 = 2 * HEAD_DIM
    lo = lax.broadcasted_iota(jnp.int32, (1, P2), 1) < HEAD_DIM
    ri = lax.broadcasted_iota(jnp.int32, (L, P2), 0)
    ci = lax.broadcasted_iota(jnp.int32, (L, P2), 1) % HEAD_DIM
    strict = ri > ci
    incl = ri >= ci
    eye2 = (ri == ci).astype(F32)
    units = [(ch, pr) for ch in range(n_chunks) for pr in range(HEADS // 2)]

    cat0 = lambda *xs: jnp.concatenate(xs, axis=0)
    cat1 = lambda *xs: jnp.concatenate(xs, axis=1)
    mm = lambda a_, b_: jnp.dot(a_, b_, preferred_element_type=F32)
    nt = lambda a_, b_: lax.dot_general(a_, b_, (((1,), (1,)), ((), ())), preferred_element_type=F32)
    tn = lambda a_, b_: lax.dot_general(a_, b_, (((0,), (0,)), ((), ())), preferred_element_type=F32)
    zero = jnp.zeros((), BF16)

    def bd(x_):
        x_ = x_.astype(BF16)
        return cat0(jnp.where(lo, x_, zero), jnp.where(lo, zero, x_))

    def diag_blocks(x_):
        sel = lo if x_.shape[1] == P2 else jnp.concatenate([lo] * (x_.shape[1] // P2), axis=1)
        return jnp.where(sel, x_[:HEAD_DIM], x_[HEAD_DIM:])

    def per_unit(arr, dtype=BF16):
        arr = arr.astype(dtype)
        return [arr[ch * L:(ch + 1) * L, pr * P2:(pr + 1) * P2] for ch, pr in units]

    qh, rh, kth, bth, vh = per_unit(qt), per_unit(rt), per_unit(kt), per_unit(bt), per_unit(v)
    khh, bhh = per_unit(kh), per_unit(bh)
    rf = per_unit(rt, F32)
    amat = [nt(cat0(q_, r_), cat0(bd(k_), bd(b_))) for q_, r_, k_, b_ in zip(qh, rh, kth, bth)]
    a_qk = [jnp.where(strict, m_[:L, :P2], 0.0).astype(BF16) for m_ in amat]
    a_qb = [jnp.where(strict, m_[:L, P2:], 0.0) for m_ in amat]
    a_rk = [jnp.where(incl, m_[L:, :P2], 0.0).astype(BF16) for m_ in amat]
    a_rb = [jnp.where(incl, m_[L:, P2:], 0.0).astype(BF16) for m_ in amat]
    tinv = [eye2 - m_ for m_ in a_qb]
    pw = [m_.astype(BF16) for m_ in a_qb]
    pw = [mm(p_, bd(p_)).astype(BF16) for p_ in pw]
    for _ in range(int(math.log2(L)) - 2):
        both = [mm(cat0(t_.astype(BF16), p_), bd(p_)) for t_, p_ in zip(tinv, pw)]
        tinv = [t_ + b_[:L] for t_, b_ in zip(tinv, both)]
        pw = [b_[L:].astype(BF16) for b_ in both]
    tinv = [t_ + mm(t_.astype(BF16), bd(p_)) for t_, p_ in zip(tinv, pw)]
    av = [mm(cat0(ak_, ar_), bd(v_)) for ak_, ar_, v_ in zip(a_qk, a_rk, vh)]
    x = [mm(t_.astype(BF16), cat1(bd(q_), bd(av_[:L]))).astype(BF16)
         for t_, q_, av_ in zip(tinv, qh, av)]
    ry = [cat1(r_, av_[L:]) - mm(ab_, cat1(bd(x_[:, :P2]), bd(x_[:, P2:])))
          for r_, av_, ab_, x_ in zip(rf, av, a_rb, x)]
    bx = [diag_blocks(tn(b_, x_)) for b_, x_ in zip(bhh, x)]
    kv = [diag_blocks(tn(k_, v_)) for k_, v_ in zip(khh, vh)]
    mmat = [(eye2 * pl_all[ch * L:ch * L + 1, pr * P2:(pr + 1) * P2] - bx_[:, :P2]).astype(BF16)
            for (ch, pr), bx_ in zip(units, bx)]
    nmat = [kv_ - bx_[:, P2:] for kv_, bx_ in zip(kv, bx)]

    n_pairs = HEADS // 2
    assert n_chunks % 2 == 0
    comp = {}
    for ch in range(0, n_chunks, 2):
        for pr in range(n_pairs):
            u0, u1 = ch * n_pairs + pr, (ch + 1) * n_pairs + pr
            both = mm(mmat[u1], cat1(bd(mmat[u0]), bd(nmat[u0])))
            comp[ch, pr] = (both[:, :P2].astype(BF16), both[:, P2:] + nmat[u1])
    h_cur = [h_ref[:, pr * P2:(pr + 1) * P2] for pr in range(n_pairs)]
    y_rows = [None] * n_chunks
    for ch in range(0, n_chunks, 2):
        y0, y1, h_mid = [], [], []
        for pr in range(n_pairs):
            u0 = ch * n_pairs + pr
            m2, n2 = comp[ch, pr]
            both = mm(cat0(ry[u0][:, :P2].astype(BF16), mmat[u0], m2), bd(h_cur[pr]))
            y0.append(both[:L] + ry[u0][:, P2:])
            h_mid.append(both[L:2 * L] + nmat[u0])
            h_cur[pr] = both[2 * L:] + n2
        for pr in range(n_pairs):
            u1 = (ch + 1) * n_pairs + pr
            y1.append(mm(ry[u1][:, :P2].astype(BF16), bd(h_mid[pr])) + ry[u1][:, P2:])
        y_rows[ch], y_rows[ch + 1] = cat1(*y0), cat1(*y1)
    h_ref[...] = cat1(*h_cur)
    y = cat0(*y_rows)

    inv_n = 1.0 / HEAD_DIM
    ym = _head_sums(y, seg) * inv_n
    yc = y - ym
    yv = _head_sums(yc * yc, seg) * inv_n
    y = yc * lax.rsqrt(yv + LNX_EPS)
    y = y * lg_ref[...] + lb_ref[...]
    y = y + _head_sums(r * kadj * rk_ref[...], seg) * v
    out_ref[...] = y * g


def _rwkv(has_vres, zr, zv, v_first, prm, consts, n_chunks):
    B, S, _ = zr.shape
    tt = n_chunks * CHUNK
    W = RWKV_WIDTH
    tile = lambda n: pl.BlockSpec((None, tt, n), lambda b, c: (b, c, 0))
    full = lambda a: pl.BlockSpec(a.shape, lambda b, c: (0,) * a.ndim)
    names = ["mu", "w0", "w_up", "a0", "a_up", "g_up", "k_k", "k_a", "r_k", "lnx_g", "lnx_b"]
    if has_vres:
        names += ["mu_vres", "v0", "vres_up"]
    params = [prm[n] for n in names] + list(consts)
    if has_vres:
        acts = [zr, zv, v_first]
        act_specs = [tile(RWKV_COLS), tile(VRES_PAD), tile(W)]
        out_shape = [jax.ShapeDtypeStruct((B, S, W), F32)]
        out_specs = [tile(W)]
        scratch = [pltpu.VMEM((HEAD_DIM, W), F32), pltpu.VMEM((1, RWKV_COLS), F32),
                   pltpu.VMEM((1, VRES_PAD), F32)]
    else:
        acts = [zr]
        act_specs = [tile(RWKV_COLS)]
        out_shape = [jax.ShapeDtypeStruct((B, S, W), F32)] * 2
        out_specs = [tile(W), tile(W)]
        scratch = [pltpu.VMEM((HEAD_DIM, W), F32), pltpu.VMEM((1, RWKV_COLS), F32)]
    return pl.pallas_call(
        functools.partial(_rwkv_kernel, has_vres, n_chunks),
        grid=(B, S // tt),
        in_specs=act_specs + [full(a) for a in params],
        out_specs=out_specs,
        out_shape=out_shape,
        scratch_shapes=scratch,
        compiler_params=pltpu.CompilerParams(
            dimension_semantics=("arbitrary", "arbitrary"), vmem_limit_bytes=VMEM_LIMIT),
        name="rwkv_vres" if has_vres else "rwkv",
    )(*acts, *params)


def _swa_kernel(n_blk, q_ref, kv_ref, kvp_ref, bkt_ref, relb_ref, sink_ref, o_ref, bias_ref):
    b_idx = pl.program_id(0)
    c_idx = pl.program_id(1)
    G2 = 2 * BLOCK

    @pl.when((b_idx == 0) & (c_idx == 0))
    def _build_bias():
        bkt = bkt_ref[...]
        for h in range(SWA_Q_HEADS):
            acc = jnp.full(bkt.shape, MASK_VALUE, F32)
            for k in range(REL_BUCKETS):
                acc = jnp.where(bkt == k, relb_ref[k, h], acc)
            pair, half = divmod(h, 2)
            bias_ref[pair * BLOCK:(pair + 1) * BLOCK, half * G2:(half + 1) * G2] = acc

    lo = lax.broadcasted_iota(jnp.int32, (1, LANES), 1) < HEAD_DIM
    ones_lo = jnp.broadcast_to(lo.astype(BF16), (G2, LANES))
    ones_hi = jnp.broadcast_to((~lo).astype(BF16), (G2, LANES))
    top = lax.broadcasted_iota(jnp.int32, (2 * BLOCK, 1), 0) < BLOCK
    prev_col = (lax.broadcasted_iota(jnp.int32, (2 * BLOCK, 2 * G2), 1) % G2) < BLOCK
    units = [(j, g) for j in range(n_blk) for g in range(SWA_KV_HEADS)]

    kk, vv = {}, {}
    for j in range(n_blk):
        if j == 0:
            kv = jnp.concatenate([kvp_ref[...], kv_ref[0:BLOCK, :]], axis=0)
        else:
            kv = kv_ref[(j - 1) * BLOCK:(j + 1) * BLOCK, :]
        kcol, vcol = kv[:, :LANES], kv[:, LANES:]
        for arr, dst, extra in ((kcol, kk, None), (vcol, vv, (ones_lo, ones_hi))):
            g0a = jnp.where(lo, arr, 0.0)
            g1b = jnp.where(lo, 0.0, arr)
            g0b = pltpu.roll(g0a, HEAD_DIM, 1)
            g1a = pltpu.roll(g1b, HEAD_DIM, 1)
            for g, (xa, xb) in enumerate(((g0a, g0b), (g1a, g1b))):
                xa, xb = xa.astype(BF16), xb.astype(BF16)
                if extra is None:
                    dst[j, g] = jnp.concatenate([xa, xb], axis=0)
                else:
                    dst[j, g] = jnp.concatenate(
                        [jnp.concatenate([xa, extra[0]], axis=1),
                         jnp.concatenate([xb, extra[1]], axis=1)], axis=0)

    def q_pairs(j, g):
        q2 = (q_ref[j * BLOCK:(j + 1) * BLOCK, g * 2 * LANES:(g + 1) * 2 * LANES] * ATTN_SCALE).astype(BF16)
        return jnp.concatenate([q2[:, :LANES], q2[:, LANES:]], axis=0)

    logits = [lax.dot_general(q_pairs(j, g), kk[j, g], (((1,), (1,)), ((), ())), preferred_element_type=F32)
              + bias_ref[g * 2 * BLOCK:(g + 1) * 2 * BLOCK, :] for j, g in units]
    logits = [jnp.where(prev_col & (c_idx == 0), MASK_VALUE, l_) if j == 0 else l_
              for (j, g), l_ in zip(units, logits)]
    sinks = [[jnp.where(top, sink_ref[4 * g + half], sink_ref[4 * g + 2 + half]) for half in range(2)]
             for j, g in units]
    ms = [[jnp.maximum(jnp.max(l_[:, half * G2:(half + 1) * G2], axis=-1, keepdims=True), s_[half])
           for half in range(2)] for l_, s_ in zip(logits, sinks)]
    es = [jnp.concatenate([jnp.exp(l_[:, half * G2:(half + 1) * G2] - m_[half]) for half in range(2)],
                          axis=1).astype(BF16) for l_, m_ in zip(logits, ms)]
    res = [jnp.dot(e_, vv[u], preferred_element_type=F32) for e_, u in zip(es, units)]
    for (j, g), r_, m_, s_ in zip(units, res, ms, sinks):
        den = r_[:, LANES:] + jnp.where(lo, jnp.exp(s_[0] - m_[0]), jnp.exp(s_[1] - m_[1]))
        out = r_[:, :LANES] / den
        rows = slice(j * BLOCK, (j + 1) * BLOCK)
        o_ref[rows, (2 * g) * LANES:(2 * g + 1) * LANES] = out[:BLOCK]
        o_ref[rows, (2 * g + 1) * LANES:(2 * g + 2) * LANES] = out[BLOCK:]


def _swa(zq, zkv, bkt, rel_bias, sinks):
    B, S, _ = zq.shape
    n_blk = TQ_SWA // BLOCK
    smem = lambda a: pl.BlockSpec(a.shape, lambda b, c: (0,) * a.ndim, memory_space=pltpu.SMEM)
    return pl.pallas_call(
        functools.partial(_swa_kernel, n_blk),
        grid=(B, S // TQ_SWA),
        in_specs=[
            pl.BlockSpec((None, TQ_SWA, SWA_WIDTH), lambda b, c: (b, c, 0)),
            pl.BlockSpec((None, TQ_SWA, 2 * SWA_KV_WIDTH), lambda b, c: (b, c, 0)),
            pl.BlockSpec((None, BLOCK, 2 * SWA_KV_WIDTH),
                         lambda b, c: (b, jnp.maximum(c * n_blk - 1, 0), 0)),
            pl.BlockSpec(bkt.shape, lambda b, c: (0, 0)),
            smem(rel_bias),
            smem(sinks),
        ],
        out_specs=pl.BlockSpec((None, TQ_SWA, SWA_WIDTH), lambda b, c: (b, c, 0)),
        out_shape=jax.ShapeDtypeStruct((B, S, SWA_WIDTH), F32),
        scratch_shapes=[pltpu.VMEM((SWA_Q_HEADS // 2 * BLOCK, 2 * 2 * BLOCK), F32)],
        compiler_params=pltpu.CompilerParams(
            dimension_semantics=("arbitrary", "arbitrary"), vmem_limit_bytes=VMEM_LIMIT),
        name="swa",
    )(zq, zkv, zkv, bkt, rel_bias, sinks)


def _post_kernel(final, x_ref, ro_ref, so_ref, p_ref, wo_ref, gm_ref, up_ref, dn_ref, pp_ref, pg_ref,
                 gate_ref, fg_ref, o_ref):
    W = RWKV_WIDTH
    x = x_ref[...]
    x = x + _bdot(ro_ref[...], wo_ref[0:W, :]) + _bdot(so_ref[...], wo_ref[W:2 * W, :])
    u = _rms(x, gm_ref[...]).astype(BF16)
    acc = jnp.zeros_like(x)
    for c in range(D_FF // FF_CHUNK):
        cs = slice(c * FF_CHUNK, (c + 1) * FF_CHUNK)
        hid = jnp.dot(u, up_ref[:, cs], preferred_element_type=F32)
        hid = jnp.square(jnp.maximum(hid, 0.0))
        acc = acc + jnp.dot(hid.astype(BF16), dn_ref[cs, :], preferred_element_type=F32)
    x = x + acc
    e = _rms(_bdot(p_ref[...], pp_ref[...]), pg_ref[...])
    x = x + e * _sigmoid(_bdot(x, gate_ref[...]))
    if final:
        x = _rms(x, fg_ref[...])
    o_ref[...] = x


def _post(layer, final, x2, ro, so, p_all, wo, gm, up, dn, pp, pg, gate, fg):
    T = x2.shape[0]
    n_tiles = T // TM_PROJ
    tile = lambda n: pl.BlockSpec((TM_PROJ, n), lambda i: (i, 0))
    p_tile = pl.BlockSpec((TM_PROJ, PLE_DIM), lambda i: (layer * n_tiles + i, 0))
    row = lambda a: pl.BlockSpec(a.shape, lambda i: (0, 0), pipeline_mode=pl.Buffered(1))
    mat = lambda a: pl.BlockSpec((None,) + a.shape[1:], lambda i: (layer, 0, 0), pipeline_mode=pl.Buffered(1))
    return pl.pallas_call(
        functools.partial(_post_kernel, final),
        grid=(n_tiles,),
        in_specs=[tile(D_MODEL), tile(RWKV_WIDTH), tile(SWA_WIDTH), p_tile,
                  mat(wo), row(gm), mat(up), mat(dn), mat(pp), row(pg), mat(gate), row(fg)],
        out_specs=tile(D_MODEL),
        out_shape=jax.ShapeDtypeStruct((T, D_MODEL), F32),
        compiler_params=pltpu.CompilerParams(
            dimension_semantics=("arbitrary",), vmem_limit_bytes=VMEM_LIMIT),
        name="post_final" if final else "post",
    )(x2, ro, so, p_all, wo, gm, up, dn, pp, pg, gate, fg)


def _bucket_tile():
    max_exact = REL_BUCKETS // 2
    dist = (np.arange(BLOCK)[:, None] + BLOCK) - np.arange(2 * BLOCK)[None, :]
    n = np.maximum(dist, 0)
    nf = np.maximum(n, 1).astype(np.float32)
    scaled = (np.log(nf / np.float32(max_exact)) / np.float32(math.log(REL_MAX_DIST / max_exact))
              * np.float32(REL_BUCKETS - max_exact)).astype(np.float32)
    large = np.minimum(max_exact + scaled.astype(np.int32), REL_BUCKETS - 1)
    bucket = np.where(n < max_exact, n, large)
    valid = (dist >= 0) & (dist < WINDOW)
    return np.where(valid, bucket, -1).astype(np.int32)


def _scan_consts(tt):
    t = np.arange(tt)
    same = (t[:, None] // CHUNK) == (t[None, :] // CHUNK)
    tril = same & (t[:, None] >= t[None, :])
    hh = np.arange(MXU_WIDTH) // HEAD_DIM
    seg = hh[:, None] == hh[None, :]
    return (jnp.asarray(seg, BF16), jnp.asarray(np.concatenate([tril, tril], axis=1), BF16))


def kernel(x, p, norm_mix_g, w_in, mu_shift, w0, w_up, a0, a_up, g_up, vres_down, mu_vres, v0, vres_up,
           k_k, k_a, r_k, lnx_g, lnx_b, sinks, rel_bias, w_out, norm_mlp_g, w_ff_up, w_ff_down,
           ple_proj, ple_norm_g, ple_gate, final_norm_g):
    B, S, _ = x.shape
    depth = w_in.shape[0]
    T = B * S
    n_chunks = SCAN_CHUNKS_PER_STEP
    assert x.shape[2] == D_MODEL and w_in.shape[1:] == (D_MODEL, IN_COLS) and p.shape == (depth, B, S, PLE_DIM)
    assert S % TQ_SWA == 0 and S % (n_chunks * CHUNK) == 0 and S % TM_PROJ == 0
    row = lambda a: a.reshape(1, -1).astype(F32)

    bkt = jnp.asarray(_bucket_tile())
    consts = _scan_consts(n_chunks * CHUNK)

    w_in_b, w_out_b = w_in.astype(BF16), w_out.astype(BF16)
    up_b, dn_b = w_ff_up.astype(BF16), w_ff_down.astype(BF16)
    pp_b, gate_b = ple_proj.astype(BF16), ple_gate.astype(BF16)
    p_all = p.reshape(depth * T, PLE_DIM)

    x2 = x.reshape(T, D_MODEL)
    v_first = None
    for i in range(depth):
        w_vres = None
        if i > 0:
            w_vres = jnp.pad(vres_down[i - 1], ((0, 0), (0, VRES_PAD - MV_LORA))).astype(BF16)
        zs = _inproj(i, x2, row(norm_mix_g[i]), w_in_b, w_vres)
        zr = zs[0].reshape(B, S, RWKV_COLS)
        zq = zs[1].reshape(B, S, SWA_WIDTH)
        zkv = zs[2].reshape(B, S, 2 * SWA_KV_WIDTH)

        prm = {
            "mu": row(mu_shift[i]), "w0": row(w0[i]), "w_up": w_up[i].astype(BF16), "a0": row(a0[i]),
            "a_up": a_up[i].astype(BF16), "g_up": g_up[i].astype(BF16), "k_k": row(k_k[i]),
            "k_a": row(k_a[i]), "r_k": row(r_k[i]), "lnx_g": row(lnx_g[i]), "lnx_b": row(lnx_b[i]),
        }
        if i == 0:
            rwkv_out, v_first = _rwkv(False, zr, None, None, prm, consts, n_chunks)
        else:
            prm["mu_vres"] = jnp.pad(row(mu_vres[i - 1]), ((0, 0), (0, VRES_PAD - MV_LORA)))
            prm["v0"] = row(v0[i - 1])
            prm["vres_up"] = jnp.pad(vres_up[i - 1], ((0, VRES_PAD - MV_LORA), (0, 0))).astype(BF16)
            zv = zs[3].reshape(B, S, VRES_PAD)
            (rwkv_out,) = _rwkv(True, zr, zv, v_first, prm, consts, n_chunks)

        swa_out = _swa(zq, zkv, bkt, rel_bias.astype(F32), sinks[i].astype(F32))

        x2 = _post(i, i == depth - 1, x2, rwkv_out.reshape(T, RWKV_WIDTH), swa_out.reshape(T, SWA_WIDTH),
                   p_all, w_out_b, row(norm_mlp_g[i]), up_b, dn_b, pp_b, row(ple_norm_g[i]), gate_b,
                   row(final_norm_g))
    return x2.reshape(B, S, D_MODEL)
```

```python
import functools
import math

import jax
import jax.numpy as jnp
import numpy as np
from jax import lax
from jax.experimental import pallas as pl
from jax.experimental.pallas import tpu as pltpu

F32 = jnp.float32
BF16 = jnp.bfloat16

D_MODEL = 1024
PLE_DIM = 256
HEADS = 8
HEAD_DIM = 64
RWKV_WIDTH = HEADS * HEAD_DIM
DECAY_LORA = 64
AAA_LORA = 64
MV_LORA = 32
GATE_LORA = 128
LNX_EPS = 64e-5
SWA_Q_HEADS = 8
SWA_KV_HEADS = 2
SWA_WIDTH = SWA_Q_HEADS * HEAD_DIM
SWA_KV_WIDTH = SWA_KV_HEADS * HEAD_DIM
WINDOW = 128
BLOCK = 128
ATTN_SCALE = 1.0 / math.sqrt(HEAD_DIM)
REL_BUCKETS = 32
REL_MAX_DIST = 128
D_FF = 4 * D_MODEL
NORM_EPS = 1e-6
RWKV_COLS = 3 * RWKV_WIDTH + DECAY_LORA + AAA_LORA + GATE_LORA
SWA_COLS = SWA_WIDTH + 2 * SWA_KV_WIDTH
IN_COLS = RWKV_COLS + SWA_COLS

DECAY_OFFSET = 0.5
KK_NORM_FLOOR = 1e-12

LANES = 128
MXU_WIDTH = 256
VRES_PAD = LANES
CHUNK = 64
SCAN_CHUNKS_PER_STEP = 4
MASK_VALUE = -1e30

TM_PROJ = 512
FF_CHUNK = 1024
VMEM_LIMIT = 56 * 1024 * 1024


def _rms(x, g):
    ms = jnp.mean(x * x, axis=-1, keepdims=True)
    return (x * lax.rsqrt(ms + NORM_EPS)) * g


def _bdot(a, b):
    return jnp.dot(a.astype(BF16), b.astype(BF16), preferred_element_type=F32)


def _split2(x):
    hi = x.astype(BF16)
    lo = (x - hi.astype(F32)).astype(BF16)
    return hi, lo


def _dot01_left(m01x2, x):
    hi, lo = _split2(x)
    return jnp.dot(m01x2, jnp.concatenate([hi, lo], axis=0), preferred_element_type=F32)


def _head_sums(x, seg):
    half = seg.shape[0]
    return jnp.concatenate(
        [jnp.dot(x[:, c * half:(c + 1) * half].astype(BF16), seg, preferred_element_type=F32)
         for c in range(x.shape[1] // half)], axis=1)


def _sigmoid(x):
    return 0.5 * jnp.tanh(0.5 * x) + 0.5


def _shift(z, prev_row, mu):
    rolled = pltpu.roll(z, 1, 0)
    row = lax.broadcasted_iota(jnp.int32, z.shape, 0)
    prev = jnp.where(row == 0, prev_row, rolled)
    return z + (prev - z) * mu


def _rwkv_kernel(has_vres, n_chunks, *refs):
    if has_vres:
        (zr_ref, zv_ref, vf_ref, mu_ref, w0_ref, wup_ref, a0_ref, aup_ref, gup_ref, kk_ref, ka_ref,
         rk_ref, lg_ref, lb_ref, muv_ref, v0_ref, vup_ref, seg_ref, tril_ref,
         out_ref, h_ref, prev_ref, prevv_ref) = refs
    else:
        (zr_ref, mu_ref, w0_ref, wup_ref, a0_ref, aup_ref, gup_ref, kk_ref, ka_ref,
         rk_ref, lg_ref, lb_ref, seg_ref, tril_ref,
         out_ref, vf_out_ref, h_ref, prev_ref) = refs

    c_idx = pl.program_id(1)

    @pl.when(c_idx == 0)
    def _():
        h_ref[...] = jnp.zeros_like(h_ref)
        prev_ref[...] = jnp.zeros_like(prev_ref)
        if has_vres:
            prevv_ref[...] = jnp.zeros_like(prevv_ref)

    W = RWKV_WIDTH
    z = zr_ref[...]
    tt = z.shape[0]
    zs = _shift(z, prev_ref[...], mu_ref[...])
    prev_ref[...] = z[tt - 1:tt, :]
    r = zs[:, 0:W]
    k = zs[:, W:2 * W]
    v = zs[:, 2 * W:3 * W]
    o1 = 3 * W
    xw = zs[:, o1:o1 + DECAY_LORA]
    xa = zs[:, o1 + DECAY_LORA:o1 + DECAY_LORA + AAA_LORA]
    xg = zs[:, o1 + DECAY_LORA + AAA_LORA:]

    ld = -math.exp(-DECAY_OFFSET) * _sigmoid(w0_ref[...] + _bdot(jnp.tanh(xw), wup_ref[...]))
    a = _sigmoid(a0_ref[...] + _bdot(xa, aup_ref[...]))
    g = _bdot(_sigmoid(xg), gup_ref[...])
    if has_vres:
        zv = zv_ref[...]
        zvs = _shift(zv, prevv_ref[...], muv_ref[...])
        prevv_ref[...] = zv[tt - 1:tt, :]
        v = v + (vf_ref[...] - v) * _sigmoid(v0_ref[...] + _bdot(zvs, vup_ref[...]))
    else:
        vf_out_ref[...] = v

    seg = seg_ref[...]
    kk = k * kk_ref[...]
    kk = kk * jnp.minimum(lax.rsqrt(_head_sums(kk * kk, seg)), 1.0 / KK_NORM_FLOOR)
    kadj = k * (1.0 + (a - 1.0) * ka_ref[...])
    b = kk * a

    c = _dot01_left(tril_ref[...], ld)
    cl = jnp.concatenate(
        [jnp.broadcast_to(c[(ch + 1) * CHUNK - 1:(ch + 1) * CHUNK, :], (CHUNK, W)) for ch in range(n_chunks)],
        axis=0)
    qt = kk * jnp.exp(c - ld)
    rt = r * jnp.exp(c)
    en = jnp.exp(-c)
    kt = kadj * en
    bt = b * en
    el = jnp.exp(cl - c)
    kh = kadj * el
    bh = b * el
    pl_all = jnp.exp(cl)

    L = CHUNK
    P2 = 2 * HEAD_DIM
    lo = lax.broadcasted_iota(jnp.int32, (1, P2), 1) < HEAD_DIM
    ri = lax.broadcasted_iota(jnp.int32, (L, P2), 0)
    ci = lax.broadcasted_iota(jnp.int32, (L, P2), 1) % HEAD_DIM
    strict = ri > ci
    incl = ri >= ci
    eye2 = (ri == ci).astype(F32)
    units = [(ch, pr) for ch in range(n_chunks) for pr in range(HEADS // 2)]

    cat0 = lambda *xs: jnp.concatenate(xs, axis=0)
    cat1 = lambda *xs: jnp.concatenate(xs, axis=1)
    mm = lambda a_, b_: jnp.dot(a_, b_, preferred_element_type=F32)
    nt = lambda a_, b_: lax.dot_general(a_, b_, (((1,), (1,)), ((), ())), preferred_element_type=F32)
    tn = lambda a_, b_: lax.dot_general(a_, b_, (((0,), (0,)), ((), ())), preferred_element_type=F32)
    zero = jnp.zeros((), BF16)

    def bd(x_):
        x_ = x_.astype(BF16)
        return cat0(jnp.where(lo, x_, zero), jnp.where(lo, zero, x_))

    def diag_blocks(x_):
        sel = lo if x_.shape[1] == P2 else jnp.concatenate([lo] * (x_.shape[1] // P2), axis=1)
        return jnp.where(sel, x_[:HEAD_DIM], x_[HEAD_DIM:])

    def per_unit(arr, dtype=BF16):
        arr = arr.astype(dtype)
        return [arr[ch * L:(ch + 1) * L, pr * P2:(pr + 1) * P2] for ch, pr in units]

    qh, rh, kth, bth, vh = per_unit(qt), per_unit(rt), per_unit(kt), per_unit(bt), per_unit(v)
    khh, bhh = per_unit(kh), per_unit(bh)
    rf = per_unit(rt, F32)
    amat = [nt(cat0(q_, r_), cat0(bd(k_), bd(b_))) for q_, r_, k_, b_ in zip(qh, rh, kth, bth)]
    a_qk = [jnp.where(strict, m_[:L, :P2], 0.0).astype(BF16) for m_ in amat]
    a_qb = [jnp.where(strict, m_[:L, P2:], 0.0) for m_ in amat]
    a_rk = [jnp.where(incl, m_[L:, :P2], 0.0).astype(BF16) for m_ in amat]
    a_rb = [jnp.where(incl, m_[L:, P2:], 0.0).astype(BF16) for m_ in amat]
    tinv = [eye2 - m_ for m_ in a_qb]
    pw = [m_.astype(BF16) for m_ in a_qb]
    pw = [mm(p_, bd(p_)).astype(BF16) for p_ in pw]
    for _ in range(int(math.log2(L)) - 2):
        both = [mm(cat0(t_.astype(BF16), p_), bd(p_)) for t_, p_ in zip(tinv, pw)]
        tinv = [t_ + b_[:L] for t_, b_ in zip(tinv, both)]
        pw = [b_[L:].astype(BF16) for b_ in both]
    tinv = [t_ + mm(t_.astype(BF16), bd(p_)) for t_, p_ in zip(tinv, pw)]
    av = [mm(cat0(ak_, ar_), bd(v_)) for ak_, ar_, v_ in zip(a_qk, a_rk, vh)]
    x = [mm(t_.astype(BF16), cat1(bd(q_), bd(av_[:L]))).astype(BF16)
         for t_, q_, av_ in zip(tinv, qh, av)]
    ry = [cat1(r_, av_[L:]) - mm(ab_, cat1(bd(x_[:, :P2]), bd(x_[:, P2:])))
          for r_, av_, ab_, x_ in zip(rf, av, a_rb, x)]
    bx = [diag_blocks(tn(b_, x_)) for b_, x_ in zip(bhh, x)]
    kv = [diag_blocks(tn(k_, v_)) for k_, v_ in zip(khh, vh)]
    mmat = [(eye2 * pl_all[ch * L:ch * L + 1, pr * P2:(pr + 1) * P2] - bx_[:, :P2]).astype(BF16)
            for (ch, pr), bx_ in zip(units, bx)]
    nmat = [kv_ - bx_[:, P2:] for kv_, bx_ in zip(kv, bx)]

    n_pairs = HEADS // 2
    assert n_chunks % 2 == 0
    comp = {}
    for ch in range(0, n_chunks, 2):
        for pr in range(n_pairs):
            u0, u1 = ch * n_pairs + pr, (ch + 1) * n_pairs + pr
            both = mm(mmat[u1], cat1(bd(mmat[u0]), bd(nmat[u0])))
            comp[ch, pr] = (both[:, :P2].astype(BF16), both[:, P2:] + nmat[u1])
    h_cur = [h_ref[:, pr * P2:(pr + 1) * P2] for pr in range(n_pairs)]
    y_rows = [None] * n_chunks
    for ch in range(0, n_chunks, 2):
        y0, y1, h_mid = [], [], []
        for pr in range(n_pairs):
            u0 = ch * n_pairs + pr
            m2, n2 = comp[ch, pr]
            both = mm(cat0(ry[u0][:, :P2].astype(BF16), mmat[u0], m2), bd(h_cur[pr]))
            y0.append(both[:L] + ry[u0][:, P2:])
            h_mid.append(both[L:2 * L] + nmat[u0])
            h_cur[pr] = both[2 * L:] + n2
        for pr in range(n_pairs):
            u1 = (ch + 1) * n_pairs + pr
            y1.append(mm(ry[u1][:, :P2].astype(BF16), bd(h_mid[pr])) + ry[u1][:, P2:])
        y_rows[ch], y_rows[ch + 1] = cat1(*y0), cat1(*y1)
    h_ref[...] = cat1(*h_cur)
    y = cat0(*y_rows)

    inv_n = 1.0 / HEAD_DIM
    ym = _head_sums(y, seg) * inv_n
    yc = y - ym
    yv = _head_sums(yc * yc, seg) * inv_n
    y = yc * lax.rsqrt(yv + LNX_EPS)
    y = y * lg_ref[...] + lb_ref[...]
    y = y + _head_sums(r * kadj * rk_ref[...], seg) * v
    out_ref[...] = y * g


def _rwkv(has_vres, zr, zv, v_first, prm, consts, n_chunks):
    B, S, _ = zr.shape
    tt = n_chunks * CHUNK
    W = RWKV_WIDTH
    tile = lambda n: pl.BlockSpec((None, tt, n), lambda b, c: (b, c, 0))
    full = lambda a: pl.BlockSpec(a.shape, lambda b, c: (0,) * a.ndim)
    names = ["mu", "w0", "w_up", "a0", "a_up", "g_up", "k_k", "k_a", "r_k", "lnx_g", "lnx_b"]
    if has_vres:
        names += ["mu_vres", "v0", "vres_up"]
    params = [prm[n] for n in names] + list(consts)
    if has_vres:
        acts = [zr, zv, v_first]
        act_specs = [tile(RWKV_COLS), tile(VRES_PAD), tile(W)]
        out_shape = [jax.ShapeDtypeStruct((B, S, W), F32)]
        out_specs = [tile(W)]
        scratch = [pltpu.VMEM((HEAD_DIM, W), F32), pltpu.VMEM((1, RWKV_COLS), F32),
                   pltpu.VMEM((1, VRES_PAD), F32)]
    else:
        acts = [zr]
        act_specs = [tile(RWKV_COLS)]
        out_shape = [jax.ShapeDtypeStruct((B, S, W), F32)] * 2
        out_specs = [tile(W), tile(W)]
        scratch = [pltpu.VMEM((HEAD_DIM, W), F32), pltpu.VMEM((1, RWKV_COLS), F32)]
    return pl.pallas_call(
        functools.partial(_rwkv_kernel, has_vres, n_chunks),
        grid=(B, S // tt),
        in_specs=act_specs + [full(a) for a in params],
        out_specs=out_specs,
        out_shape=out_shape,
        scratch_shapes=scratch,
        compiler_params=pltpu.CompilerParams(
            dimension_semantics=("arbitrary", "arbitrary"), vmem_limit_bytes=VMEM_LIMIT),
        name="rwkv_vres" if has_vres else "rwkv",
    )(*acts, *params)


_DONE = object()


def _interleave(*gens):
    live = list(gens)
    while live:
        live = [g_ for g_ in live if next(g_, _DONE) is not _DONE]


def _mix_kernel(has_vres, tiles_per_seq, x_ref, g_ref, w_ref, bkt_ref, relb_ref, sink_ref, *refs):
    if has_vres:
        wv_ref, zr_ref, o_ref, zv_ref, bias_ref, kvp_ref = refs
    else:
        zr_ref, o_ref, bias_ref, kvp_ref = refs
    i = pl.program_id(0)
    first = lax.rem(i, tiles_per_seq) == 0
    G2 = 2 * BLOCK
    n_blk = TM_PROJ // BLOCK

    @pl.when(i == 0)
    def _build_bias():
        bkt = bkt_ref[...]
        for h in range(SWA_Q_HEADS):
            acc = jnp.full(bkt.shape, MASK_VALUE, F32)
            for k in range(REL_BUCKETS):
                acc = jnp.where(bkt == k, relb_ref[k, h], acc)
            pair, half = divmod(h, 2)
            bias_ref[pair * BLOCK:(pair + 1) * BLOCK, half * G2:(half + 1) * G2] = acc
        kvp_ref[...] = jnp.zeros_like(kvp_ref)

    h_in = _rms(x_ref[...], g_ref[...]).astype(BF16)
    zq = jnp.dot(h_in, w_ref[:, RWKV_COLS:RWKV_COLS + SWA_WIDTH], preferred_element_type=F32)
    zkv = jnp.dot(h_in, w_ref[:, RWKV_COLS + SWA_WIDTH:IN_COLS], preferred_element_type=F32)

    def rwkv_columns():
        for off in range(0, RWKV_COLS, MXU_WIDTH):
            zr_ref[:, off:off + MXU_WIDTH] = jnp.dot(h_in, w_ref[:, off:off + MXU_WIDTH],
                                                     preferred_element_type=F32)
            yield
        if has_vres:
            zv_ref[...] = jnp.dot(h_in, wv_ref[...], preferred_element_type=F32)
            yield

    lo = lax.broadcasted_iota(jnp.int32, (1, LANES), 1) < HEAD_DIM
    ones_lo = jnp.broadcast_to(lo.astype(BF16), (G2, LANES))
    ones_hi = jnp.broadcast_to((~lo).astype(BF16), (G2, LANES))
    top = lax.broadcasted_iota(jnp.int32, (2 * BLOCK, 1), 0) < BLOCK
    prev_col = (lax.broadcasted_iota(jnp.int32, (2 * BLOCK, 2 * G2), 1) % G2) < BLOCK
    units = [(j, g) for j in range(n_blk) for g in range(SWA_KV_HEADS)]

    def attention():
        kk, vv = {}, {}
        for j in range(n_blk):
            if j == 0:
                kv = jnp.concatenate([kvp_ref[...], zkv[0:BLOCK, :]], axis=0)
            else:
                kv = zkv[(j - 1) * BLOCK:(j + 1) * BLOCK, :]
            kcol, vcol = kv[:, :LANES], kv[:, LANES:]
            for arr, dst, extra in ((kcol, kk, None), (vcol, vv, (ones_lo, ones_hi))):
                g0a = jnp.where(lo, arr, 0.0)
                g1b = jnp.where(lo, 0.0, arr)
                g0b = pltpu.roll(g0a, HEAD_DIM, 1)
                g1a = pltpu.roll(g1b, HEAD_DIM, 1)
                for g, (xa, xb) in enumerate(((g0a, g0b), (g1a, g1b))):
                    xa, xb = xa.astype(BF16), xb.astype(BF16)
                    if extra is None:
                        dst[j, g] = jnp.concatenate([xa, xb], axis=0)
                    else:
                        dst[j, g] = jnp.concatenate(
                            [jnp.concatenate([xa, extra[0]], axis=1),
                             jnp.concatenate([xb, extra[1]], axis=1)], axis=0)
            yield
        kvp_ref[...] = zkv[TM_PROJ - BLOCK:TM_PROJ, :]

        def q_pairs(j, g):
            q2 = (zq[j * BLOCK:(j + 1) * BLOCK, g * 2 * LANES:(g + 1) * 2 * LANES] * ATTN_SCALE).astype(BF16)
            return jnp.concatenate([q2[:, :LANES], q2[:, LANES:]], axis=0)

        logits = [lax.dot_general(q_pairs(j, g), kk[j, g], (((1,), (1,)), ((), ())), preferred_element_type=F32)
                  + bias_ref[g * 2 * BLOCK:(g + 1) * 2 * BLOCK, :] for j, g in units]
        logits = [jnp.where(prev_col & first, MASK_VALUE, l_) if j == 0 else l_
                  for (j, g), l_ in zip(units, logits)]
        yield
        sinks = [[jnp.where(top, sink_ref[4 * g + half], sink_ref[4 * g + 2 + half]) for half in range(2)]
                 for j, g in units]
        ms = [[jnp.maximum(jnp.max(l_[:, half * G2:(half + 1) * G2], axis=-1, keepdims=True), s_[half])
               for half in range(2)] for l_, s_ in zip(logits, sinks)]
        yield
        es = [jnp.concatenate([jnp.exp(l_[:, half * G2:(half + 1) * G2] - m_[half]) for half in range(2)],
                              axis=1).astype(BF16) for l_, m_ in zip(logits, ms)]
        yield
        res = [jnp.dot(e_, vv[u], preferred_element_type=F32) for e_, u in zip(es, units)]
        yield
        for (j, g), r_, m_, s_ in zip(units, res, ms, sinks):
            den = r_[:, LANES:] + jnp.where(lo, jnp.exp(s_[0] - m_[0]), jnp.exp(s_[1] - m_[1]))
            out = r_[:, :LANES] / den
            rows = slice(j * BLOCK, (j + 1) * BLOCK)
            o_ref[rows, (2 * g) * LANES:(2 * g + 1) * LANES] = out[:BLOCK]
            o_ref[rows, (2 * g + 1) * LANES:(2 * g + 2) * LANES] = out[BLOCK:]
            if g == SWA_KV_HEADS - 1:
                yield

    _interleave(attention(), rwkv_columns())


def _mix(layer, x2, g, w_all, w_vres, bkt, rel_bias, sinks, tiles_per_seq):
    T = x2.shape[0]
    smem = lambda a: pl.BlockSpec(a.shape, lambda i: (0,) * a.ndim, memory_space=pltpu.SMEM)
    in_specs = [
        pl.BlockSpec((TM_PROJ, D_MODEL), lambda i: (i, 0)),
        pl.BlockSpec((1, D_MODEL), lambda i: (0, 0)),
        pl.BlockSpec((None, D_MODEL, IN_COLS), lambda i: (layer, 0, 0)),
        pl.BlockSpec(bkt.shape, lambda i: (0, 0)),
        smem(rel_bias),
        smem(sinks),
    ]
    args = [x2, g, w_all, bkt, rel_bias, sinks]
    widths = [RWKV_COLS, SWA_WIDTH]
    if w_vres is not None:
        widths.append(VRES_PAD)
        in_specs.append(pl.BlockSpec(w_vres.shape, lambda i: (0, 0)))
        args.append(w_vres)
    return pl.pallas_call(
        functools.partial(_mix_kernel, w_vres is not None, tiles_per_seq),
        grid=(T // TM_PROJ,),
        in_specs=in_specs,
        out_specs=[pl.BlockSpec((TM_PROJ, n), lambda i: (i, 0)) for n in widths],
        out_shape=[jax.ShapeDtypeStruct((T, n), F32) for n in widths],
        scratch_shapes=[pltpu.VMEM((SWA_Q_HEADS // 2 * BLOCK, 2 * 2 * BLOCK), F32),
                        pltpu.VMEM((BLOCK, 2 * SWA_KV_WIDTH), F32)],
        compiler_params=pltpu.CompilerParams(
            dimension_semantics=("arbitrary",), vmem_limit_bytes=VMEM_LIMIT),
        name="mix",
    )(*args)


def _post_kernel(final, x_ref, ro_ref, so_ref, p_ref, wo_ref, gm_ref, up_ref, dn_ref, pp_ref, pg_ref,
                 gate_ref, fg_ref, o_ref):
    W = RWKV_WIDTH
    x = x_ref[...]
    x = x + _bdot(ro_ref[...], wo_ref[0:W, :]) + _bdot(so_ref[...], wo_ref[W:2 * W, :])
    u = _rms(x, gm_ref[...]).astype(BF16)
    acc = jnp.zeros_like(x)
    for c in range(D_FF // FF_CHUNK):
        cs = slice(c * FF_CHUNK, (c + 1) * FF_CHUNK)
        hid = jnp.dot(u, up_ref[:, cs], preferred_element_type=F32)
        hid = jnp.square(jnp.maximum(hid, 0.0))
        acc = acc + jnp.dot(hid.astype(BF16), dn_ref[cs, :], preferred_element_type=F32)
    x = x + acc
    e = _rms(_bdot(p_ref[...], pp_ref[...]), pg_ref[...])
    x = x + e * _sigmoid(_bdot(x, gate_ref[...]))
    if final:
        x = _rms(x, fg_ref[...])
    o_ref[...] = x


def _post(layer, final, x2, ro, so, p_all, wo, gm, up, dn, pp, pg, gate, fg):
    T = x2.shape[0]
    n_tiles = T // TM_PROJ
    tile = lambda n: pl.BlockSpec((TM_PROJ, n), lambda i: (i, 0))
    p_tile = pl.BlockSpec((TM_PROJ, PLE_DIM), lambda i: (layer * n_tiles + i, 0))
    row = lambda a: pl.BlockSpec(a.shape, lambda i: (0, 0), pipeline_mode=pl.Buffered(1))
    mat = lambda a: pl.BlockSpec((None,) + a.shape[1:], lambda i: (layer, 0, 0), pipeline_mode=pl.Buffered(1))
    return pl.pallas_call(
        functools.partial(_post_kernel, final),
        grid=(n_tiles,),
        in_specs=[tile(D_MODEL), tile(RWKV_WIDTH), tile(SWA_WIDTH), p_tile,
                  mat(wo), row(gm), mat(up), mat(dn), mat(pp), row(pg), mat(gate), row(fg)],
        out_specs=tile(D_MODEL),
        out_shape=jax.ShapeDtypeStruct((T, D_MODEL), F32),
        compiler_params=pltpu.CompilerParams(
            dimension_semantics=("arbitrary",), vmem_limit_bytes=VMEM_LIMIT),
        name="post_final" if final else "post",
    )(x2, ro, so, p_all, wo, gm, up, dn, pp, pg, gate, fg)


def _bucket_tile():
    max_exact = REL_BUCKETS // 2
    dist = (np.arange(BLOCK)[:, None] + BLOCK) - np.arange(2 * BLOCK)[None, :]
    n = np.maximum(dist, 0)
    nf = np.maximum(n, 1).astype(np.float32)
    scaled = (np.log(nf / np.float32(max_exact)) / np.float32(math.log(REL_MAX_DIST / max_exact))
              * np.float32(REL_BUCKETS - max_exact)).astype(np.float32)
    large = np.minimum(max_exact + scaled.astype(np.int32), REL_BUCKETS - 1)
    bucket = np.where(n < max_exact, n, large)
    valid = (dist >= 0) & (dist < WINDOW)
    return np.where(valid, bucket, -1).astype(np.int32)


def _scan_consts(tt):
    t = np.arange(tt)
    same = (t[:, None] // CHUNK) == (t[None, :] // CHUNK)
    tril = same & (t[:, None] >= t[None, :])
    hh = np.arange(MXU_WIDTH) // HEAD_DIM
    seg = hh[:, None] == hh[None, :]
    return (jnp.asarray(seg, BF16), jnp.asarray(np.concatenate([tril, tril], axis=1), BF16))


def kernel(x, p, norm_mix_g, w_in, mu_shift, w0, w_up, a0, a_up, g_up, vres_down, mu_vres, v0, vres_up,
           k_k, k_a, r_k, lnx_g, lnx_b, sinks, rel_bias, w_out, norm_mlp_g, w_ff_up, w_ff_down,
           ple_proj, ple_norm_g, ple_gate, final_norm_g):
    B, S, _ = x.shape
    depth = w_in.shape[0]
    T = B * S
    n_chunks = SCAN_CHUNKS_PER_STEP
    assert x.shape[2] == D_MODEL and w_in.shape[1:] == (D_MODEL, IN_COLS) and p.shape == (depth, B, S, PLE_DIM)
    assert S % (n_chunks * CHUNK) == 0 and S % TM_PROJ == 0 and TM_PROJ % BLOCK == 0
    row = lambda a: a.reshape(1, -1).astype(F32)

    bkt = jnp.asarray(_bucket_tile())
    consts = _scan_consts(n_chunks * CHUNK)

    w_in_b, w_out_b = w_in.astype(BF16), w_out.astype(BF16)
    up_b, dn_b = w_ff_up.astype(BF16), w_ff_down.astype(BF16)
    pp_b, gate_b = ple_proj.astype(BF16), ple_gate.astype(BF16)
    p_all = p.reshape(depth * T, PLE_DIM)

    x2 = x.reshape(T, D_MODEL)
    v_first = None
    for i in range(depth):
        w_vres = None
        if i > 0:
            w_vres = jnp.pad(vres_down[i - 1], ((0, 0), (0, VRES_PAD - MV_LORA))).astype(BF16)
        zs = _mix(i, x2, row(norm_mix_g[i]), w_in_b, w_vres, bkt, rel_bias.astype(F32), sinks[i].astype(F32),
                  S // TM_PROJ)
        zr = zs[0].reshape(B, S, RWKV_COLS)
        swa_out = zs[1]

        prm = {
            "mu": row(mu_shift[i]), "w0": row(w0[i]), "w_up": w_up[i].astype(BF16), "a0": row(a0[i]),
            "a_up": a_up[i].astype(BF16), "g_up": g_up[i].astype(BF16), "k_k": row(k_k[i]),
            "k_a": row(k_a[i]), "r_k": row(r_k[i]), "lnx_g": row(lnx_g[i]), "lnx_b": row(lnx_b[i]),
        }
        if i == 0:
            rwkv_out, v_first = _rwkv(False, zr, None, None, prm, consts, n_chunks)
        else:
            prm["mu_vres"] = jnp.pad(row(mu_vres[i - 1]), ((0, 0), (0, VRES_PAD - MV_LORA)))
            prm["v0"] = row(v0[i - 1])
            prm["vres_up"] = jnp.pad(vres_up[i - 1], ((0, VRES_PAD - MV_LORA), (0, 0))).astype(BF16)
            zv = zs[2].reshape(B, S, VRES_PAD)
            (rwkv_out,) = _rwkv(True, zr, zv, v_first, prm, consts, n_chunks)

        x2 = _post(i, i == depth - 1, x2, rwkv_out.reshape(T, RWKV_WIDTH), swa_out,
                   p_all, w_out_b, row(norm_mlp_g[i]), up_b, dn_b, pp_b, row(ple_norm_g[i]), gate_b,
                   row(final_norm_g))
    return x2.reshape(B, S, D_MODEL)
```

```python
import functools
import math

import jax
import jax.numpy as jnp
import numpy as np
from jax import lax
from jax.experimental import pallas as pl
from jax.experimental.pallas import tpu as pltpu

F32 = jnp.float32
BF16 = jnp.bfloat16

D_MODEL = 1024
PLE_DIM = 256
HEADS = 8
HEAD_DIM = 64
RWKV_WIDTH = HEADS * HEAD_DIM
DECAY_LORA = 64
AAA_LORA = 64
MV_LORA = 32
GATE_LORA = 128
LNX_EPS = 64e-5
SWA_Q_HEADS = 8
SWA_KV_HEADS = 2
SWA_WIDTH = SWA_Q_HEADS * HEAD_DIM
SWA_KV_WIDTH = SWA_KV_HEADS * HEAD_DIM
WINDOW = 128
BLOCK = 128
ATTN_SCALE = 1.0 / math.sqrt(HEAD_DIM)
REL_BUCKETS = 32
REL_MAX_DIST = 128
D_FF = 4 * D_MODEL
NORM_EPS = 1e-6
RWKV_COLS = 3 * RWKV_WIDTH + DECAY_LORA + AAA_LORA + GATE_LORA
SWA_COLS = SWA_WIDTH + 2 * SWA_KV_WIDTH
IN_COLS = RWKV_COLS + SWA_COLS

DECAY_OFFSET = 0.5
KK_NORM_FLOOR = 1e-12

LANES = 128
MXU_WIDTH = 256
VRES_PAD = LANES
CHUNK = 64
SCAN_CHUNKS_PER_STEP = 4
MASK_VALUE = -1e30

TM_PROJ = 512
FF_CHUNK = 1024
VMEM_LIMIT = 56 * 1024 * 1024


def _rms(x, g):
    ms = jnp.mean(x * x, axis=-1, keepdims=True)
    return (x * lax.rsqrt(ms + NORM_EPS)) * g


def _bdot(a, b):
    return jnp.dot(a.astype(BF16), b.astype(BF16), preferred_element_type=F32)


def _split2(x):
    hi = x.astype(BF16)
    lo = (x - hi.astype(F32)).astype(BF16)
    return hi, lo


def _dot01_left(m01x2, x):
    hi, lo = _split2(x)
    return jnp.dot(m01x2, jnp.concatenate([hi, lo], axis=0), preferred_element_type=F32)


def _head_sums(x, seg):
    half = seg.shape[0]
    return jnp.concatenate(
        [jnp.dot(x[:, c * half:(c + 1) * half].astype(BF16), seg, preferred_element_type=F32)
         for c in range(x.shape[1] // half)], axis=1)


def _sigmoid(x):
    return 0.5 * jnp.tanh(0.5 * x) + 0.5


def _shift(z, prev_row, mu):
    rolled = pltpu.roll(z, 1, 0)
    row = lax.broadcasted_iota(jnp.int32, z.shape, 0)
    prev = jnp.where(row == 0, prev_row, rolled)
    return z + (prev - z) * mu


def _rwkv_kernel(has_vres, n_chunks, *refs):
    if has_vres:
        (zr_ref, zv_ref, vf_ref, mu_ref, w0_ref, wup_ref, a0_ref, aup_ref, gup_ref, kk_ref, ka_ref,
         rk_ref, lg_ref, lb_ref, muv_ref, v0_ref, vup_ref, seg_ref, tril_ref,
         out_ref, h_ref, prev_ref, prevv_ref) = refs
    else:
        (zr_ref, mu_ref, w0_ref, wup_ref, a0_ref, aup_ref, gup_ref, kk_ref, ka_ref,
         rk_ref, lg_ref, lb_ref, seg_ref, tril_ref,
         out_ref, vf_out_ref, h_ref, prev_ref) = refs

    c_idx = pl.program_id(1)

    @pl.when(c_idx == 0)
    def _():
        h_ref[...] = jnp.zeros_like(h_ref)
        prev_ref[...] = jnp.zeros_like(prev_ref)
        if has_vres:
            prevv_ref[...] = jnp.zeros_like(prevv_ref)

    W = RWKV_WIDTH
    z = zr_ref[...]
    tt = z.shape[0]
    zs = _shift(z, prev_ref[...], mu_ref[...])
    prev_ref[...] = z[tt - 1:tt, :]
    r = zs[:, 0:W]
    k = zs[:, W:2 * W]
    v = zs[:, 2 * W:3 * W]
    o1 = 3 * W
    xw = zs[:, o1:o1 + DECAY_LORA]
    xa = zs[:, o1 + DECAY_LORA:o1 + DECAY_LORA + AAA_LORA]
    xg = zs[:, o1 + DECAY_LORA + AAA_LORA:]

    ld = -math.exp(-DECAY_OFFSET) * _sigmoid(w0_ref[...] + _bdot(jnp.tanh(xw), wup_ref[...]))
    a = _sigmoid(a0_ref[...] + _bdot(xa, aup_ref[...]))
    g = _bdot(_sigmoid(xg), gup_ref[...])
    if has_vres:
        zv = zv_ref[...]
        zvs = _shift(zv, prevv_ref[...], muv_ref[...])
        prevv_ref[...] = zv[tt - 1:tt, :]
        v = v + (vf_ref[...] - v) * _sigmoid(v0_ref[...] + _bdot(zvs, vup_ref[...]))
    else:
        vf_out_ref[...] = v

    seg = seg_ref[...]
    kk = k * kk_ref[...]
    kk = kk * jnp.minimum(lax.rsqrt(_head_sums(kk * kk, seg)), 1.0 / KK_NORM_FLOOR)
    kadj = k * (1.0 + (a - 1.0) * ka_ref[...])
    b = kk * a

    c = _dot01_left(tril_ref[...], ld)
    cl = jnp.concatenate(
        [jnp.broadcast_to(c[(ch + 1) * CHUNK - 1:(ch + 1) * CHUNK, :], (CHUNK, W)) for ch in range(n_chunks)],
        axis=0)
    qt = kk * jnp.exp(c - ld)
    rt = r * jnp.exp(c)
    en = jnp.exp(-c)
    kt = kadj * en
    bt = b * en
    el = jnp.exp(cl - c)
    kh = kadj * el
    bh = b * el
    pl_all = jnp.exp(cl)

    L = CHUNK
    P2 = 2 * HEAD_DIM
    lo = lax.broadcasted_iota(jnp.int32, (1, P2), 1) < HEAD_DIM
    ri = lax.broadcasted_iota(jnp.int32, (L, P2), 0)
    ci = lax.broadcasted_iota(jnp.int32, (L, P2), 1) % HEAD_DIM
    strict = ri > ci
    incl = ri >= ci
    eye2 = (ri == ci).astype(F32)
    units = [(ch, pr) for ch in range(n_chunks) for pr in range(HEADS // 2)]

    cat0 = lambda *xs: jnp.concatenate(xs, axis=0)
    cat1 = lambda *xs: jnp.concatenate(xs, axis=1)
    mm = lambda a_, b_: jnp.dot(a_, b_, preferred_element_type=F32)
    nt = lambda a_, b_: lax.dot_general(a_, b_, (((1,), (1,)), ((), ())), preferred_element_type=F32)
    tn = lambda a_, b_: lax.dot_general(a_, b_, (((0,), (0,)), ((), ())), preferred_element_type=F32)
    zero = jnp.zeros((), BF16)

    def bd(x_):
        x_ = x_.astype(BF16)
        return cat0(jnp.where(lo, x_, zero), jnp.where(lo, zero, x_))

    def diag_blocks(x_):
        sel = lo if x_.shape[1] == P2 else jnp.concatenate([lo] * (x_.shape[1] // P2), axis=1)
        return jnp.where(sel, x_[:HEAD_DIM], x_[HEAD_DIM:])

    def per_unit(arr, dtype=BF16):
        arr = arr.astype(dtype)
        return [arr[ch * L:(ch + 1) * L, pr * P2:(pr + 1) * P2] for ch, pr in units]

    qh, rh, kth, bth, vh = per_unit(qt), per_unit(rt), per_unit(kt), per_unit(bt), per_unit(v)
    khh, bhh = per_unit(kh), per_unit(bh)
    rf = per_unit(rt, F32)
    amat = [nt(cat0(q_, r_), cat0(bd(k_), bd(b_))) for q_, r_, k_, b_ in zip(qh, rh, kth, bth)]
    a_qk = [jnp.where(strict, m_[:L, :P2], 0.0).astype(BF16) for m_ in amat]
    a_qb = [jnp.where(strict, m_[:L, P2:], 0.0) for m_ in amat]
    a_rk = [jnp.where(incl, m_[L:, :P2], 0.0).astype(BF16) for m_ in amat]
    a_rb = [jnp.where(incl, m_[L:, P2:], 0.0).astype(BF16) for m_ in amat]
    tinv = [eye2 - m_ for m_ in a_qb]
    pw = [m_.astype(BF16) for m_ in a_qb]
    pw = [mm(p_, bd(p_)).astype(BF16) for p_ in pw]
    for _ in range(int(math.log2(L)) - 2):
        both = [mm(cat0(t_.astype(BF16), p_), bd(p_)) for t_, p_ in zip(tinv, pw)]
        tinv = [t_ + b_[:L] for t_, b_ in zip(tinv, both)]
        pw = [b_[L:].astype(BF16) for b_ in both]
    tinv = [t_ + mm(t_.astype(BF16), bd(p_)) for t_, p_ in zip(tinv, pw)]
    t_sp = [_split2(t_) for t_ in tinv]
    a_sp = [_split2(m_) for m_ in a_qb]
    at_h = [mm(cat0(ah_, al_), bd(th_)) for (ah_, al_), (th_, _) in zip(a_sp, t_sp)]
    at_l = [mm(ah_, bd(tl_)) for (ah_, _), (_, tl_) in zip(a_sp, t_sp)]
    resid = [eye2 - t_ - h_[:L] - h_[L:] - l_ for t_, h_, l_ in zip(tinv, at_h, at_l)]
    tinv = [t_ + mm(th_, bd(r_)) for t_, (th_, _), r_ in zip(tinv, t_sp, resid)]
    av = [mm(cat0(ak_, ar_), bd(v_)) for ak_, ar_, v_ in zip(a_qk, a_rk, vh)]
    x = [mm(t_.astype(BF16), cat1(bd(q_), bd(av_[:L]))).astype(BF16)
         for t_, q_, av_ in zip(tinv, qh, av)]
    ry = [cat1(r_, av_[L:]) - mm(ab_, cat1(bd(x_[:, :P2]), bd(x_[:, P2:])))
          for r_, av_, ab_, x_ in zip(rf, av, a_rb, x)]
    bx = [diag_blocks(tn(b_, x_)) for b_, x_ in zip(bhh, x)]
    kv = [diag_blocks(tn(k_, v_)) for k_, v_ in zip(khh, vh)]
    mmat = [(eye2 * pl_all[ch * L:ch * L + 1, pr * P2:(pr + 1) * P2] - bx_[:, :P2]).astype(BF16)
            for (ch, pr), bx_ in zip(units, bx)]
    nmat = [kv_ - bx_[:, P2:] for kv_, bx_ in zip(kv, bx)]

    n_pairs = HEADS // 2
    assert n_chunks % 2 == 0
    comp = {}
    for ch in range(0, n_chunks, 2):
        for pr in range(n_pairs):
            u0, u1 = ch * n_pairs + pr, (ch + 1) * n_pairs + pr
            both = mm(mmat[u1], cat1(bd(mmat[u0]), bd(nmat[u0])))
            comp[ch, pr] = (both[:, :P2].astype(BF16), both[:, P2:] + nmat[u1])
    h_cur = [h_ref[:, pr * P2:(pr + 1) * P2] for pr in range(n_pairs)]
    y_rows = [None] * n_chunks
    for ch in range(0, n_chunks, 2):
        y0, y1, h_mid = [], [], []
        for pr in range(n_pairs):
            u0 = ch * n_pairs + pr
            m2, n2 = comp[ch, pr]
            both = mm(cat0(ry[u0][:, :P2].astype(BF16), mmat[u0], m2), bd(h_cur[pr]))
            y0.append(both[:L] + ry[u0][:, P2:])
            h_mid.append(both[L:2 * L] + nmat[u0])
            h_cur[pr] = both[2 * L:] + n2
        for pr in range(n_pairs):
            u1 = (ch + 1) * n_pairs + pr
            y1.append(mm(ry[u1][:, :P2].astype(BF16), bd(h_mid[pr])) + ry[u1][:, P2:])
        y_rows[ch], y_rows[ch + 1] = cat1(*y0), cat1(*y1)
    h_ref[...] = cat1(*h_cur)
    y = cat0(*y_rows)

    inv_n = 1.0 / HEAD_DIM
    ym = _head_sums(y, seg) * inv_n
    yc = y - ym
    yv = _head_sums(yc * yc, seg) * inv_n
    y = yc * lax.rsqrt(yv + LNX_EPS)
    y = y * lg_ref[...] + lb_ref[...]
    y = y + _head_sums(r * kadj * rk_ref[...], seg) * v
    out_ref[...] = y * g


def _rwkv(has_vres, zr, zv, v_first, prm, consts, n_chunks):
    B, S, _ = zr.shape
    tt = n_chunks * CHUNK
    W = RWKV_WIDTH
    tile = lambda n: pl.BlockSpec((None, tt, n), lambda b, c: (b, c, 0))
    full = lambda a: pl.BlockSpec(a.shape, lambda b, c: (0,) * a.ndim)
    names = ["mu", "w0", "w_up", "a0", "a_up", "g_up", "k_k", "k_a", "r_k", "lnx_g", "lnx_b"]
    if has_vres:
        names += ["mu_vres", "v0", "vres_up"]
    params = [prm[n] for n in names] + list(consts)
    if has_vres:
        acts = [zr, zv, v_first]
        act_specs = [tile(RWKV_COLS), tile(VRES_PAD), tile(W)]
        out_shape = [jax.ShapeDtypeStruct((B, S, W), F32)]
        out_specs = [tile(W)]
        scratch = [pltpu.VMEM((HEAD_DIM, W), F32), pltpu.VMEM((1, RWKV_COLS), F32),
                   pltpu.VMEM((1, VRES_PAD), F32)]
    else:
        acts = [zr]
        act_specs = [tile(RWKV_COLS)]
        out_shape = [jax.ShapeDtypeStruct((B, S, W), F32)] * 2
        out_specs = [tile(W), tile(W)]
        scratch = [pltpu.VMEM((HEAD_DIM, W), F32), pltpu.VMEM((1, RWKV_COLS), F32)]
    return pl.pallas_call(
        functools.partial(_rwkv_kernel, has_vres, n_chunks),
        grid=(B, S // tt),
        in_specs=act_specs + [full(a) for a in params],
        out_specs=out_specs,
        out_shape=out_shape,
        scratch_shapes=scratch,
        compiler_params=pltpu.CompilerParams(
            dimension_semantics=("arbitrary", "arbitrary"), vmem_limit_bytes=VMEM_LIMIT),
        name="rwkv_vres" if has_vres else "rwkv",
    )(*acts, *params)


_DONE = object()


def _interleave(*gens):
    live = list(gens)
    while live:
        live = [g_ for g_ in live if next(g_, _DONE) is not _DONE]


def _mix_kernel(has_vres, tiles_per_seq, x_ref, g_ref, w_ref, bkt_ref, relb_ref, sink_ref, *refs):
    if has_vres:
        wv_ref, zr_ref, o_ref, zv_ref, bias_ref, kvp_ref = refs
    else:
        zr_ref, o_ref, bias_ref, kvp_ref = refs
    i = pl.program_id(0)
    first = lax.rem(i, tiles_per_seq) == 0
    G2 = 2 * BLOCK
    n_blk = TM_PROJ // BLOCK

    @pl.when(i == 0)
    def _build_bias():
        bkt = bkt_ref[...]
        for h in range(SWA_Q_HEADS):
            acc = jnp.full(bkt.shape, MASK_VALUE, F32)
            for k in range(REL_BUCKETS):
                acc = jnp.where(bkt == k, relb_ref[k, h], acc)
            pair, half = divmod(h, 2)
            bias_ref[pair * BLOCK:(pair + 1) * BLOCK, half * G2:(half + 1) * G2] = acc
        kvp_ref[...] = jnp.zeros_like(kvp_ref)

    h_in = _rms(x_ref[...], g_ref[...]).astype(BF16)
    zq = jnp.dot(h_in, w_ref[:, RWKV_COLS:RWKV_COLS + SWA_WIDTH], preferred_element_type=F32)
    zkv = jnp.dot(h_in, w_ref[:, RWKV_COLS + SWA_WIDTH:IN_COLS], preferred_element_type=F32)

    def rwkv_columns():
        for off in range(0, RWKV_COLS, MXU_WIDTH):
            zr_ref[:, off:off + MXU_WIDTH] = jnp.dot(h_in, w_ref[:, off:off + MXU_WIDTH],
                                                     preferred_element_type=F32)
            yield
        if has_vres:
            zv_ref[...] = jnp.dot(h_in, wv_ref[...], preferred_element_type=F32)
            yield

    lo = lax.broadcasted_iota(jnp.int32, (1, LANES), 1) < HEAD_DIM
    ones_lo = jnp.broadcast_to(lo.astype(BF16), (G2, LANES))
    ones_hi = jnp.broadcast_to((~lo).astype(BF16), (G2, LANES))
    top = lax.broadcasted_iota(jnp.int32, (2 * BLOCK, 1), 0) < BLOCK
    prev_col = (lax.broadcasted_iota(jnp.int32, (2 * BLOCK, 2 * G2), 1) % G2) < BLOCK
    units = [(j, g) for j in range(n_blk) for g in range(SWA_KV_HEADS)]

    def attention():
        kk, vv = {}, {}
        for j in range(n_blk):
            if j == 0:
                kv = jnp.concatenate([kvp_ref[...], zkv[0:BLOCK, :]], axis=0)
            else:
                kv = zkv[(j - 1) * BLOCK:(j + 1) * BLOCK, :]
            kcol, vcol = kv[:, :LANES], kv[:, LANES:]
            for arr, dst, extra in ((kcol, kk, None), (vcol, vv, (ones_lo, ones_hi))):
                g0a = jnp.where(lo, arr, 0.0)
                g1b = jnp.where(lo, 0.0, arr)
                g0b = pltpu.roll(g0a, HEAD_DIM, 1)
                g1a = pltpu.roll(g1b, HEAD_DIM, 1)
                for g, (xa, xb) in enumerate(((g0a, g0b), (g1a, g1b))):
                    xa, xb = xa.astype(BF16), xb.astype(BF16)
                    if extra is None:
                        dst[j, g] = jnp.concatenate([xa, xb], axis=0)
                    else:
                        dst[j, g] = jnp.concatenate(
                            [jnp.concatenate([xa, extra[0]], axis=1),
                             jnp.concatenate([xb, extra[1]], axis=1)], axis=0)
            yield
        kvp_ref[...] = zkv[TM_PROJ - BLOCK:TM_PROJ, :]

        def q_pairs(j, g):
            q2 = (zq[j * BLOCK:(j + 1) * BLOCK, g * 2 * LANES:(g + 1) * 2 * LANES] * ATTN_SCALE).astype(BF16)
            return jnp.concatenate([q2[:, :LANES], q2[:, LANES:]], axis=0)

        logits = [lax.dot_general(q_pairs(j, g), kk[j, g], (((1,), (1,)), ((), ())), preferred_element_type=F32)
                  + bias_ref[g * 2 * BLOCK:(g + 1) * 2 * BLOCK, :] for j, g in units]
        logits = [jnp.where(prev_col & first, MASK_VALUE, l_) if j == 0 else l_
                  for (j, g), l_ in zip(units, logits)]
        yield
        sinks = [[jnp.where(top, sink_ref[4 * g + half], sink_ref[4 * g + 2 + half]) for half in range(2)]
                 for j, g in units]
        ms = [[jnp.maximum(jnp.max(l_[:, half * G2:(half + 1) * G2], axis=-1, keepdims=True), s_[half])
               for half in range(2)] for l_, s_ in zip(logits, sinks)]
        yield
        es = [jnp.concatenate([jnp.exp(l_[:, half * G2:(half + 1) * G2] - m_[half]) for half in range(2)],
                              axis=1).astype(BF16) for l_, m_ in zip(logits, ms)]
        yield
        res = [jnp.dot(e_, vv[u], preferred_element_type=F32) for e_, u in zip(es, units)]
        yield
        for (j, g), r_, m_, s_ in zip(units, res, ms, sinks):
            den = r_[:, LANES:] + jnp.where(lo, jnp.exp(s_[0] - m_[0]), jnp.exp(s_[1] - m_[1]))
            out = r_[:, :LANES] / den
            rows = slice(j * BLOCK, (j + 1) * BLOCK)
            o_ref[rows, (2 * g) * LANES:(2 * g + 1) * LANES] = out[:BLOCK]
            o_ref[rows, (2 * g + 1) * LANES:(2 * g + 2) * LANES] = out[BLOCK:]
            if g == SWA_KV_HEADS - 1:
                yield

    _interleave(attention(), rwkv_columns())


def _mix(layer, x2, g, w_all, w_vres, bkt, rel_bias, sinks, tiles_per_seq):
    T = x2.shape[0]
    smem = lambda a: pl.BlockSpec(a.shape, lambda i: (0,) * a.ndim, memory_space=pltpu.SMEM)
    in_specs = [
        pl.BlockSpec((TM_PROJ, D_MODEL), lambda i: (i, 0)),
        pl.BlockSpec((1, D_MODEL), lambda i: (0, 0)),
        pl.BlockSpec((None, D_MODEL, IN_COLS), lambda i: (layer, 0, 0)),
        pl.BlockSpec(bkt.shape, lambda i: (0, 0)),
        smem(rel_bias),
        smem(sinks),
    ]
    args = [x2, g, w_all, bkt, rel_bias, sinks]
    widths = [RWKV_COLS, SWA_WIDTH]
    if w_vres is not None:
        widths.append(VRES_PAD)
        in_specs.append(pl.BlockSpec(w_vres.shape, lambda i: (0, 0)))
        args.append(w_vres)
    return pl.pallas_call(
        functools.partial(_mix_kernel, w_vres is not None, tiles_per_seq),
        grid=(T // TM_PROJ,),
        in_specs=in_specs,
        out_specs=[pl.BlockSpec((TM_PROJ, n), lambda i: (i, 0)) for n in widths],
        out_shape=[jax.ShapeDtypeStruct((T, n), F32) for n in widths],
        scratch_shapes=[pltpu.VMEM((SWA_Q_HEADS // 2 * BLOCK, 2 * 2 * BLOCK), F32),
                        pltpu.VMEM((BLOCK, 2 * SWA_KV_WIDTH), F32)],
        compiler_params=pltpu.CompilerParams(
            dimension_semantics=("arbitrary",), vmem_limit_bytes=VMEM_LIMIT),
        name="mix",
    )(*args)


def _post_kernel(final, x_ref, ro_ref, so_ref, p_ref, wo_ref, gm_ref, up_ref, dn_ref, pp_ref, pg_ref,
                 gate_ref, fg_ref, o_ref):
    W = RWKV_WIDTH
    x = x_ref[...]
    x = x + _bdot(ro_ref[...], wo_ref[0:W, :]) + _bdot(so_ref[...], wo_ref[W:2 * W, :])
    u = _rms(x, gm_ref[...]).astype(BF16)
    acc = jnp.zeros_like(x)
    for c in range(D_FF // FF_CHUNK):
        cs = slice(c * FF_CHUNK, (c + 1) * FF_CHUNK)
        hid = jnp.dot(u, up_ref[:, cs], preferred_element_type=F32)
        hid = jnp.square(jnp.maximum(hid, 0.0))
        acc = acc + jnp.dot(hid.astype(BF16), dn_ref[cs, :], preferred_element_type=F32)
    x = x + acc
    e = _rms(_bdot(p_ref[...], pp_ref[...]), pg_ref[...])
    x = x + e * _sigmoid(_bdot(x, gate_ref[...]))
    if final:
        x = _rms(x, fg_ref[...])
    o_ref[...] = x


def _post(layer, final, x2, ro, so, p_all, wo, gm, up, dn, pp, pg, gate, fg):
    T = x2.shape[0]
    n_tiles = T // TM_PROJ
    tile = lambda n: pl.BlockSpec((TM_PROJ, n), lambda i: (i, 0))
    p_tile = pl.BlockSpec((TM_PROJ, PLE_DIM), lambda i: (layer * n_tiles + i, 0))
    row = lambda a: pl.BlockSpec(a.shape, lambda i: (0, 0), pipeline_mode=pl.Buffered(1))
    mat = lambda a: pl.BlockSpec((None,) + a.shape[1:], lambda i: (layer, 0, 0), pipeline_mode=pl.Buffered(1))
    return pl.pallas_call(
        functools.partial(_post_kernel, final),
        grid=(n_tiles,),
        in_specs=[tile(D_MODEL), tile(RWKV_WIDTH), tile(SWA_WIDTH), p_tile,
                  mat(wo), row(gm), mat(up), mat(dn), mat(pp), row(pg), mat(gate), row(fg)],
        out_specs=tile(D_MODEL),
        out_shape=jax.ShapeDtypeStruct((T, D_MODEL), F32),
        compiler_params=pltpu.CompilerParams(
            dimension_semantics=("arbitrary",), vmem_limit_bytes=VMEM_LIMIT),
        name="post_final" if final else "post",
    )(x2, ro, so, p_all, wo, gm, up, dn, pp, pg, gate, fg)


def _bucket_tile():
    max_exact = REL_BUCKETS // 2
    dist = (np.arange(BLOCK)[:, None] + BLOCK) - np.arange(2 * BLOCK)[None, :]
    n = np.maximum(dist, 0)
    nf = np.maximum(n, 1).astype(np.float32)
    scaled = (np.log(nf / np.float32(max_exact)) / np.float32(math.log(REL_MAX_DIST / max_exact))
              * np.float32(REL_BUCKETS - max_exact)).astype(np.float32)
    large = np.minimum(max_exact + scaled.astype(np.int32), REL_BUCKETS - 1)
    bucket = np.where(n < max_exact, n, large)
    valid = (dist >= 0) & (dist < WINDOW)
    return np.where(valid, bucket, -1).astype(np.int32)


def _scan_consts(tt):
    t = np.arange(tt)
    same = (t[:, None] // CHUNK) == (t[None, :] // CHUNK)
    tril = same & (t[:, None] >= t[None, :])
    hh = np.arange(MXU_WIDTH) // HEAD_DIM
    seg = hh[:, None] == hh[None, :]
    return (jnp.asarray(seg, BF16), jnp.asarray(np.concatenate([tril, tril], axis=1), BF16))


def kernel(x, p, norm_mix_g, w_in, mu_shift, w0, w_up, a0, a_up, g_up, vres_down, mu_vres, v0, vres_up,
           k_k, k_a, r_k, lnx_g, lnx_b, sinks, rel_bias, w_out, norm_mlp_g, w_ff_up, w_ff_down,
           ple_proj, ple_norm_g, ple_gate, final_norm_g):
    B, S, _ = x.shape
    depth = w_in.shape[0]
    T = B * S
    n_chunks = SCAN_CHUNKS_PER_STEP
    assert x.shape[2] == D_MODEL and w_in.shape[1:] == (D_MODEL, IN_COLS) and p.shape == (depth, B, S, PLE_DIM)
    assert S % (n_chunks * CHUNK) == 0 and S % TM_PROJ == 0 and TM_PROJ % BLOCK == 0
    row = lambda a: a.reshape(1, -1).astype(F32)

    bkt = jnp.asarray(_bucket_tile())
    consts = _scan_consts(n_chunks * CHUNK)

    w_in_b, w_out_b = w_in.astype(BF16), w_out.astype(BF16)
    up_b, dn_b = w_ff_up.astype(BF16), w_ff_down.astype(BF16)
    pp_b, gate_b = ple_proj.astype(BF16), ple_gate.astype(BF16)
    p_all = p.reshape(depth * T, PLE_DIM)

    x2 = x.reshape(T, D_MODEL)
    v_first = None
    for i in range(depth):
        w_vres = None
        if i > 0:
            w_vres = jnp.pad(vres_down[i - 1], ((0, 0), (0, VRES_PAD - MV_LORA))).astype(BF16)
        zs = _mix(i, x2, row(norm_mix_g[i]), w_in_b, w_vres, bkt, rel_bias.astype(F32), sinks[i].astype(F32),
                  S // TM_PROJ)
        zr = zs[0].reshape(B, S, RWKV_COLS)
        swa_out = zs[1]

        prm = {
            "mu": row(mu_shift[i]), "w0": row(w0[i]), "w_up": w_up[i].astype(BF16), "a0": row(a0[i]),
            "a_up": a_up[i].astype(BF16), "g_up": g_up[i].astype(BF16), "k_k": row(k_k[i]),
            "k_a": row(k_a[i]), "r_k": row(r_k[i]), "lnx_g": row(lnx_g[i]), "lnx_b": row(lnx_b[i]),
        }
        if i == 0:
            rwkv_out, v_first = _rwkv(False, zr, None, None, prm, consts, n_chunks)
        else:
            prm["mu_vres"] = jnp.pad(row(mu_vres[i - 1]), ((0, 0), (0, VRES_PAD - MV_LORA)))
            prm["v0"] = row(v0[i - 1])
            prm["vres_up"] = jnp.pad(vres_up[i - 1], ((0, VRES_PAD - MV_LORA), (0, 0))).astype(BF16)
            zv = zs[2].reshape(B, S, VRES_PAD)
            (rwkv_out,) = _rwkv(True, zr, zv, v_first, prm, consts, n_chunks)

        x2 = _post(i, i == depth - 1, x2, rwkv_out.reshape(T, RWKV_WIDTH), swa_out,
                   p_all, w_out_b, row(norm_mlp_g[i]), up_b, dn_b, pp_b, row(ple_norm_g[i]), gate_b,
                   row(final_norm_g))
    return x2.reshape(B, S, D_MODEL)
```

```python
import functools
import math

import jax
import jax.numpy as jnp
import numpy as np
from jax import lax
from jax.experimental import pallas as pl
from jax.experimental.pallas import tpu as pltpu

F32 = jnp.float32
BF16 = jnp.bfloat16

D_MODEL = 1024
PLE_DIM = 256
HEADS = 8
HEAD_DIM = 64
RWKV_WIDTH = HEADS * HEAD_DIM
DECAY_LORA = 64
AAA_LORA = 64
MV_LORA = 32
GATE_LORA = 128
LNX_EPS = 64e-5
SWA_Q_HEADS = 8
SWA_KV_HEADS = 2
SWA_WIDTH = SWA_Q_HEADS * HEAD_DIM
SWA_KV_WIDTH = SWA_KV_HEADS * HEAD_DIM
WINDOW = 128
BLOCK = 128
ATTN_SCALE = 1.0 / math.sqrt(HEAD_DIM)
REL_BUCKETS = 32
REL_MAX_DIST = 128
D_FF = 4 * D_MODEL
NORM_EPS = 1e-6
RWKV_COLS = 3 * RWKV_WIDTH + DECAY_LORA + AAA_LORA + GATE_LORA
SWA_COLS = SWA_WIDTH + 2 * SWA_KV_WIDTH
IN_COLS = RWKV_COLS + SWA_COLS

DECAY_OFFSET = 0.5
KK_NORM_FLOOR = 1e-12

LANES = 128
MXU_WIDTH = 256
VRES_PAD = LANES
CHUNK = 64
SCAN_CHUNKS_PER_STEP = 4
MASK_VALUE = -1e30

TM_PROJ = 512
FF_CHUNK = 1024
VMEM_LIMIT = 56 * 1024 * 1024


def _rms(x, g):
    ms = jnp.mean(x * x, axis=-1, keepdims=True)
    return (x * lax.rsqrt(ms + NORM_EPS)) * g


def _bdot(a, b):
    return jnp.dot(a.astype(BF16), b.astype(BF16), preferred_element_type=F32)


def _split2(x):
    hi = x.astype(BF16)
    lo = (x - hi.astype(F32)).astype(BF16)
    return hi, lo


def _dot01_left(m01x2, x):
    hi, lo = _split2(x)
    return jnp.dot(m01x2, jnp.concatenate([hi, lo], axis=0), preferred_element_type=F32)


def _head_sums(x, seg):
    half = seg.shape[0]
    return jnp.concatenate(
        [jnp.dot(x[:, c * half:(c + 1) * half].astype(BF16), seg, preferred_element_type=F32)
         for c in range(x.shape[1] // half)], axis=1)


def _sigmoid(x):
    return 0.5 * jnp.tanh(0.5 * x) + 0.5


def _shift(z, prev_row, mu):
    rolled = pltpu.roll(z, 1, 0)
    row = lax.broadcasted_iota(jnp.int32, z.shape, 0)
    prev = jnp.where(row == 0, prev_row, rolled)
    return z + (prev - z) * mu


def _rwkv_kernel(has_vres, n_chunks, *refs):
    if has_vres:
        (zr_ref, zv_ref, vf_ref, mu_ref, w0_ref, wup_ref, a0_ref, aup_ref, gup_ref, kk_ref, ka_ref,
         rk_ref, lg_ref, lb_ref, muv_ref, v0_ref, vup_ref, seg_ref, tril_ref,
         out_ref, h_ref, prev_ref, prevv_ref) = refs
    else:
        (zr_ref, mu_ref, w0_ref, wup_ref, a0_ref, aup_ref, gup_ref, kk_ref, ka_ref,
         rk_ref, lg_ref, lb_ref, seg_ref, tril_ref,
         out_ref, vf_out_ref, h_ref, prev_ref) = refs

    c_idx = pl.program_id(1)

    @pl.when(c_idx == 0)
    def _():
        h_ref[...] = jnp.zeros_like(h_ref)
        prev_ref[...] = jnp.zeros_like(prev_ref)
        if has_vres:
            prevv_ref[...] = jnp.zeros_like(prevv_ref)

    W = RWKV_WIDTH
    z = zr_ref[...]
    tt = z.shape[0]
    zs = _shift(z, prev_ref[...], mu_ref[...])
    prev_ref[...] = z[tt - 1:tt, :]
    r = zs[:, 0:W]
    k = zs[:, W:2 * W]
    v = zs[:, 2 * W:3 * W]
    o1 = 3 * W
    xw = zs[:, o1:o1 + DECAY_LORA]
    xa = zs[:, o1 + DECAY_LORA:o1 + DECAY_LORA + AAA_LORA]
    xg = zs[:, o1 + DECAY_LORA + AAA_LORA:]

    ld = -math.exp(-DECAY_OFFSET) * _sigmoid(w0_ref[...] + _bdot(jnp.tanh(xw), wup_ref[...]))
    a = _sigmoid(a0_ref[...] + _bdot(xa, aup_ref[...]))
    g = _bdot(_sigmoid(xg), gup_ref[...])
    if has_vres:
        zv = zv_ref[...]
        zvs = _shift(zv, prevv_ref[...], muv_ref[...])
        prevv_ref[...] = zv[tt - 1:tt, :]
        v = v + (vf_ref[...] - v) * _sigmoid(v0_ref[...] + _bdot(zvs, vup_ref[...]))
    else:
        vf_out_ref[...] = v

    seg = seg_ref[...]
    kk = k * kk_ref[...]
    kk = kk * jnp.minimum(lax.rsqrt(_head_sums(kk * kk, seg)), 1.0 / KK_NORM_FLOOR)
    kadj = k * (1.0 + (a - 1.0) * ka_ref[...])
    b = kk * a

    c = _dot01_left(tril_ref[...], ld)
    cl = jnp.concatenate(
        [jnp.broadcast_to(c[(ch + 1) * CHUNK - 1:(ch + 1) * CHUNK, :], (CHUNK, W)) for ch in range(n_chunks)],
        axis=0)
    qt = kk * jnp.exp(c - ld)
    rt = r * jnp.exp(c)
    en = jnp.exp(-c)
    kt = kadj * en
    bt = b * en
    el = jnp.exp(cl - c)
    kh = kadj * el
    bh = b * el
    pl_all = jnp.exp(cl)

    L = CHUNK
    P2 = 2 * HEAD_DIM
    lo = lax.broadcasted_iota(jnp.int32, (1, P2), 1) < HEAD_DIM
    ri = lax.broadcasted_iota(jnp.int32, (L, P2), 0)
    ci = lax.broadcasted_iota(jnp.int32, (L, P2), 1) % HEAD_DIM
    strict = ri > ci
    incl = ri >= ci
    eye2 = (ri == ci).astype(F32)
    units = [(ch, pr) for ch in range(n_chunks) for pr in range(HEADS // 2)]

    cat0 = lambda *xs: jnp.concatenate(xs, axis=0)
    cat1 = lambda *xs: jnp.concatenate(xs, axis=1)
    mm = lambda a_, b_: jnp.dot(a_, b_, preferred_element_type=F32)
    nt = lambda a_, b_: lax.dot_general(a_, b_, (((1,), (1,)), ((), ())), preferred_element_type=F32)
    tn = lambda a_, b_: lax.dot_general(a_, b_, (((0,), (0,)), ((), ())), preferred_element_type=F32)
    zero = jnp.zeros((), BF16)

    def bd(x_):
        x_ = x_.astype(BF16)
        return cat0(jnp.where(lo, x_, zero), jnp.where(lo, zero, x_))

    def diag_blocks(x_):
        sel = lo if x_.shape[1] == P2 else jnp.concatenate([lo] * (x_.shape[1] // P2), axis=1)
        return jnp.where(sel, x_[:HEAD_DIM], x_[HEAD_DIM:])

    def per_unit(arr, dtype=BF16):
        arr = arr.astype(dtype)
        return [arr[ch * L:(ch + 1) * L, pr * P2:(pr + 1) * P2] for ch, pr in units]

    qh, rh, kth, bth, vh = per_unit(qt), per_unit(rt), per_unit(kt), per_unit(bt), per_unit(v)
    khh, bhh = per_unit(kh), per_unit(bh)
    rf = per_unit(rt, F32)
    amat = [nt(cat0(q_, r_), cat0(bd(k_), bd(b_))) for q_, r_, k_, b_ in zip(qh, rh, kth, bth)]
    a_qk = [jnp.where(strict, m_[:L, :P2], 0.0).astype(BF16) for m_ in amat]
    a_qb = [jnp.where(strict, m_[:L, P2:], 0.0) for m_ in amat]
    a_rk = [jnp.where(incl, m_[L:, :P2], 0.0).astype(BF16) for m_ in amat]
    a_rb = [jnp.where(incl, m_[L:, P2:], 0.0).astype(BF16) for m_ in amat]
    tinv = [eye2 - m_ for m_ in a_qb]
    pw = [m_.astype(BF16) for m_ in a_qb]
    pw = [mm(p_, bd(p_)).astype(BF16) for p_ in pw]
    for _ in range(int(math.log2(L)) - 3):
        both = [mm(cat0(t_.astype(BF16), p_), bd(p_)) for t_, p_ in zip(tinv, pw)]
        tinv = [t_ + b_[:L] for t_, b_ in zip(tinv, both)]
        pw = [b_[L:].astype(BF16) for b_ in both]
    tinv = [t_ + mm(t_.astype(BF16), bd(p_)) for t_, p_ in zip(tinv, pw)]
    t_sp = [_split2(t_) for t_ in tinv]
    a_sp = [_split2(m_) for m_ in a_qb]
    at_h = [mm(cat0(ah_, al_), bd(th_)) for (ah_, al_), (th_, _) in zip(a_sp, t_sp)]
    at_l = [mm(ah_, bd(tl_)) for (ah_, _), (_, tl_) in zip(a_sp, t_sp)]
    resid = [eye2 - t_ - h_[:L] - h_[L:] - l_ for t_, h_, l_ in zip(tinv, at_h, at_l)]
    tinv = [t_ + mm(th_, bd(r_)) for t_, (th_, _), r_ in zip(tinv, t_sp, resid)]
    av = [mm(cat0(ak_, ar_), bd(v_)) for ak_, ar_, v_ in zip(a_qk, a_rk, vh)]
    x = [mm(t_.astype(BF16), cat1(bd(q_), bd(av_[:L]))).astype(BF16)
         for t_, q_, av_ in zip(tinv, qh, av)]
    ry = [cat1(r_, av_[L:]) - mm(ab_, cat1(bd(x_[:, :P2]), bd(x_[:, P2:])))
          for r_, av_, ab_, x_ in zip(rf, av, a_rb, x)]
    bx = [diag_blocks(tn(b_, x_)) for b_, x_ in zip(bhh, x)]
    kv = [diag_blocks(tn(k_, v_)) for k_, v_ in zip(khh, vh)]
    mmat = [(eye2 * pl_all[ch * L:ch * L + 1, pr * P2:(pr + 1) * P2] - bx_[:, :P2]).astype(BF16)
            for (ch, pr), bx_ in zip(units, bx)]
    nmat = [kv_ - bx_[:, P2:] for kv_, bx_ in zip(kv, bx)]

    n_pairs = HEADS // 2
    assert n_chunks % 2 == 0
    comp = {}
    for ch in range(0, n_chunks, 2):
        for pr in range(n_pairs):
            u0, u1 = ch * n_pairs + pr, (ch + 1) * n_pairs + pr
            both = mm(mmat[u1], cat1(bd(mmat[u0]), bd(nmat[u0])))
            comp[ch, pr] = (both[:, :P2].astype(BF16), both[:, P2:] + nmat[u1])
    h_cur = [h_ref[:, pr * P2:(pr + 1) * P2] for pr in range(n_pairs)]
    y_rows = [None] * n_chunks
    for ch in range(0, n_chunks, 2):
        y0, y1, h_mid = [], [], []
        for pr in range(n_pairs):
            u0 = ch * n_pairs + pr
            m2, n2 = comp[ch, pr]
            both = mm(cat0(ry[u0][:, :P2].astype(BF16), mmat[u0], m2), bd(h_cur[pr]))
            y0.append(both[:L] + ry[u0][:, P2:])
            h_mid.append(both[L:2 * L] + nmat[u0])
            h_cur[pr] = both[2 * L:] + n2
        for pr in range(n_pairs):
            u1 = (ch + 1) * n_pairs + pr
            y1.append(mm(ry[u1][:, :P2].astype(BF16), bd(h_mid[pr])) + ry[u1][:, P2:])
        y_rows[ch], y_rows[ch + 1] = cat1(*y0), cat1(*y1)
    h_ref[...] = cat1(*h_cur)
    y = cat0(*y_rows)

    inv_n = 1.0 / HEAD_DIM
    ym = _head_sums(y, seg) * inv_n
    yc = y - ym
    yv = _head_sums(yc * yc, seg) * inv_n
    y = yc * lax.rsqrt(yv + LNX_EPS)
    y = y * lg_ref[...] + lb_ref[...]
    y = y + _head_sums(r * kadj * rk_ref[...], seg) * v
    out_ref[...] = y * g


def _rwkv(has_vres, zr, zv, v_first, prm, consts, n_chunks):
    B, S, _ = zr.shape
    tt = n_chunks * CHUNK
    W = RWKV_WIDTH
    tile = lambda n: pl.BlockSpec((None, tt, n), lambda b, c: (b, c, 0))
    full = lambda a: pl.BlockSpec(a.shape, lambda b, c: (0,) * a.ndim)
    names = ["mu", "w0", "w_up", "a0", "a_up", "g_up", "k_k", "k_a", "r_k", "lnx_g", "lnx_b"]
    if has_vres:
        names += ["mu_vres", "v0", "vres_up"]
    params = [prm[n] for n in names] + list(consts)
    if has_vres:
        acts = [zr, zv, v_first]
        act_specs = [tile(RWKV_COLS), tile(VRES_PAD), tile(W)]
        out_shape = [jax.ShapeDtypeStruct((B, S, W), F32)]
        out_specs = [tile(W)]
        scratch = [pltpu.VMEM((HEAD_DIM, W), F32), pltpu.VMEM((1, RWKV_COLS), F32),
                   pltpu.VMEM((1, VRES_PAD), F32)]
    else:
        acts = [zr]
        act_specs = [tile(RWKV_COLS)]
        out_shape = [jax.ShapeDtypeStruct((B, S, W), F32)] * 2
        out_specs = [tile(W), tile(W)]
        scratch = [pltpu.VMEM((HEAD_DIM, W), F32), pltpu.VMEM((1, RWKV_COLS), F32)]
    return pl.pallas_call(
        functools.partial(_rwkv_kernel, has_vres, n_chunks),
        grid=(B, S // tt),
        in_specs=act_specs + [full(a) for a in params],
        out_specs=out_specs,
        out_shape=out_shape,
        scratch_shapes=scratch,
        compiler_params=pltpu.CompilerParams(
            dimension_semantics=("arbitrary", "arbitrary"), vmem_limit_bytes=VMEM_LIMIT),
        name="rwkv_vres" if has_vres else "rwkv",
    )(*acts, *params)


_DONE = object()


def _interleave(*gens):
    live = list(gens)
    while live:
        live = [g_ for g_ in live if next(g_, _DONE) is not _DONE]


def _mix_kernel(has_vres, tiles_per_seq, x_ref, g_ref, w_ref, bkt_ref, relb_ref, sink_ref, *refs):
    if has_vres:
        wv_ref, zr_ref, o_ref, zv_ref, bias_ref, kvp_ref = refs
    else:
        zr_ref, o_ref, bias_ref, kvp_ref = refs
    i = pl.program_id(0)
    first = lax.rem(i, tiles_per_seq) == 0
    G2 = 2 * BLOCK
    n_blk = TM_PROJ // BLOCK

    @pl.when(i == 0)
    def _build_bias():
        bkt = bkt_ref[...]
        for h in range(SWA_Q_HEADS):
            acc = jnp.full(bkt.shape, MASK_VALUE, F32)
            for k in range(REL_BUCKETS):
                acc = jnp.where(bkt == k, relb_ref[k, h], acc)
            pair, half = divmod(h, 2)
            bias_ref[pair * BLOCK:(pair + 1) * BLOCK, half * G2:(half + 1) * G2] = acc
        kvp_ref[...] = jnp.zeros_like(kvp_ref)

    h_in = _rms(x_ref[...], g_ref[...]).astype(BF16)
    zq = jnp.dot(h_in, w_ref[:, RWKV_COLS:RWKV_COLS + SWA_WIDTH], preferred_element_type=F32)
    zkv = jnp.dot(h_in, w_ref[:, RWKV_COLS + SWA_WIDTH:IN_COLS], preferred_element_type=F32)

    def rwkv_columns():
        for off in range(0, RWKV_COLS, MXU_WIDTH):
            zr_ref[:, off:off + MXU_WIDTH] = jnp.dot(h_in, w_ref[:, off:off + MXU_WIDTH],
                                                     preferred_element_type=F32)
            yield
        if has_vres:
            zv_ref[...] = jnp.dot(h_in, wv_ref[...], preferred_element_type=F32)
            yield

    lo = lax.broadcasted_iota(jnp.int32, (1, LANES), 1) < HEAD_DIM
    ones_lo = jnp.broadcast_to(lo.astype(BF16), (G2, LANES))
    ones_hi = jnp.broadcast_to((~lo).astype(BF16), (G2, LANES))
    top = lax.broadcasted_iota(jnp.int32, (2 * BLOCK, 1), 0) < BLOCK
    prev_col = (lax.broadcasted_iota(jnp.int32, (2 * BLOCK, 2 * G2), 1) % G2) < BLOCK
    units = [(j, g) for j in range(n_blk) for g in range(SWA_KV_HEADS)]

    def attention():
        kk, vv = {}, {}
        for j in range(n_blk):
            if j == 0:
                kv = jnp.concatenate([kvp_ref[...], zkv[0:BLOCK, :]], axis=0)
            else:
                kv = zkv[(j - 1) * BLOCK:(j + 1) * BLOCK, :]
            kcol, vcol = kv[:, :LANES], kv[:, LANES:]
            for arr, dst, extra in ((kcol, kk, None), (vcol, vv, (ones_lo, ones_hi))):
                g0a = jnp.where(lo, arr, 0.0)
                g1b = jnp.where(lo, 0.0, arr)
                g0b = pltpu.roll(g0a, HEAD_DIM, 1)
                g1a = pltpu.roll(g1b, HEAD_DIM, 1)
                for g, (xa, xb) in enumerate(((g0a, g0b), (g1a, g1b))):
                    xa, xb = xa.astype(BF16), xb.astype(BF16)
                    if extra is None:
                        dst[j, g] = jnp.concatenate([xa, xb], axis=0)
                    else:
                        dst[j, g] = jnp.concatenate(
                            [jnp.concatenate([xa, extra[0]], axis=1),
                             jnp.concatenate([xb, extra[1]], axis=1)], axis=0)
            yield
        kvp_ref[...] = zkv[TM_PROJ - BLOCK:TM_PROJ, :]

        def q_pairs(j, g):
            q2 = (zq[j * BLOCK:(j + 1) * BLOCK, g * 2 * LANES:(g + 1) * 2 * LANES] * ATTN_SCALE).astype(BF16)
            return jnp.concatenate([q2[:, :LANES], q2[:, LANES:]], axis=0)

        logits = [lax.dot_general(q_pairs(j, g), kk[j, g], (((1,), (1,)), ((), ())), preferred_element_type=F32)
                  + bias_ref[g * 2 * BLOCK:(g + 1) * 2 * BLOCK, :] for j, g in units]
        logits = [jnp.where(prev_col & first, MASK_VALUE, l_) if j == 0 else l_
                  for (j, g), l_ in zip(units, logits)]
        yield
        sinks = [[jnp.where(top, sink_ref[4 * g + half], sink_ref[4 * g + 2 + half]) for half in range(2)]
                 for j, g in units]
        ms = [[jnp.maximum(jnp.max(l_[:, half * G2:(half + 1) * G2], axis=-1, keepdims=True), s_[half])
               for half in range(2)] for l_, s_ in zip(logits, sinks)]
        yield
        es = [jnp.concatenate([jnp.exp(l_[:, half * G2:(half + 1) * G2] - m_[half]) for half in range(2)],
                              axis=1).astype(BF16) for l_, m_ in zip(logits, ms)]
        yield
        res = [jnp.dot(e_, vv[u], preferred_element_type=F32) for e_, u in zip(es, units)]
        yield
        for (j, g), r_, m_, s_ in zip(units, res, ms, sinks):
            den = r_[:, LANES:] + jnp.where(lo, jnp.exp(s_[0] - m_[0]), jnp.exp(s_[1] - m_[1]))
            out = r_[:, :LANES] / den
            rows = slice(j * BLOCK, (j + 1) * BLOCK)
            o_ref[rows, (2 * g) * LANES:(2 * g + 1) * LANES] = out[:BLOCK]
            o_ref[rows, (2 * g + 1) * LANES:(2 * g + 2) * LANES] = out[BLOCK:]
            if g == SWA_KV_HEADS - 1:
                yield

    _interleave(attention(), rwkv_columns())


def _mix(layer, x2, g, w_all, w_vres, bkt, rel_bias, sinks, tiles_per_seq):
    T = x2.shape[0]
    smem = lambda a: pl.BlockSpec(a.shape, lambda i: (0,) * a.ndim, memory_space=pltpu.SMEM)
    in_specs = [
        pl.BlockSpec((TM_PROJ, D_MODEL), lambda i: (i, 0)),
        pl.BlockSpec((1, D_MODEL), lambda i: (0, 0)),
        pl.BlockSpec((None, D_MODEL, IN_COLS), lambda i: (layer, 0, 0)),
        pl.BlockSpec(bkt.shape, lambda i: (0, 0)),
        smem(rel_bias),
        smem(sinks),
    ]
    args = [x2, g, w_all, bkt, rel_bias, sinks]
    widths = [RWKV_COLS, SWA_WIDTH]
    if w_vres is not None:
        widths.append(VRES_PAD)
        in_specs.append(pl.BlockSpec(w_vres.shape, lambda i: (0, 0)))
        args.append(w_vres)
    return pl.pallas_call(
        functools.partial(_mix_kernel, w_vres is not None, tiles_per_seq),
        grid=(T // TM_PROJ,),
        in_specs=in_specs,
        out_specs=[pl.BlockSpec((TM_PROJ, n), lambda i: (i, 0)) for n in widths],
        out_shape=[jax.ShapeDtypeStruct((T, n), F32) for n in widths],
        scratch_shapes=[pltpu.VMEM((SWA_Q_HEADS // 2 * BLOCK, 2 * 2 * BLOCK), F32),
                        pltpu.VMEM((BLOCK, 2 * SWA_KV_WIDTH), F32)],
        compiler_params=pltpu.CompilerParams(
            dimension_semantics=("arbitrary",), vmem_limit_bytes=VMEM_LIMIT),
        name="mix",
    )(*args)


def _post_kernel(final, x_ref, ro_ref, so_ref, p_ref, wo_ref, gm_ref, up_ref, dn_ref, pp_ref, pg_ref,
                 gate_ref, fg_ref, o_ref):
    W = RWKV_WIDTH
    x = x_ref[...]
    x = x + _bdot(ro_ref[...], wo_ref[0:W, :]) + _bdot(so_ref[...], wo_ref[W:2 * W, :])
    u = _rms(x, gm_ref[...]).astype(BF16)
    acc = jnp.zeros_like(x)
    for c in range(D_FF // FF_CHUNK):
        cs = slice(c * FF_CHUNK, (c + 1) * FF_CHUNK)
        hid = jnp.dot(u, up_ref[:, cs], preferred_element_type=F32)
        hid = jnp.square(jnp.maximum(hid, 0.0))
        acc = acc + jnp.dot(hid.astype(BF16), dn_ref[cs, :], preferred_element_type=F32)
    x = x + acc
    e = _rms(_bdot(p_ref[...], pp_ref[...]), pg_ref[...])
    x = x + e * _sigmoid(_bdot(x, gate_ref[...]))
    if final:
        x = _rms(x, fg_ref[...])
    o_ref[...] = x


def _post(layer, final, x2, ro, so, p_all, wo, gm, up, dn, pp, pg, gate, fg):
    T = x2.shape[0]
    n_tiles = T // TM_PROJ
    tile = lambda n: pl.BlockSpec((TM_PROJ, n), lambda i: (i, 0))
    p_tile = pl.BlockSpec((TM_PROJ, PLE_DIM), lambda i: (layer * n_tiles + i, 0))
    row = lambda a: pl.BlockSpec(a.shape, lambda i: (0, 0), pipeline_mode=pl.Buffered(1))
    mat = lambda a: pl.BlockSpec((None,) + a.shape[1:], lambda i: (layer, 0, 0), pipeline_mode=pl.Buffered(1))
    return pl.pallas_call(
        functools.partial(_post_kernel, final),
        grid=(n_tiles,),
        in_specs=[tile(D_MODEL), tile(RWKV_WIDTH), tile(SWA_WIDTH), p_tile,
                  mat(wo), row(gm), mat(up), mat(dn), mat(pp), row(pg), mat(gate), row(fg)],
        out_specs=tile(D_MODEL),
        out_shape=jax.ShapeDtypeStruct((T, D_MODEL), F32),
        compiler_params=pltpu.CompilerParams(
            dimension_semantics=("arbitrary",), vmem_limit_bytes=VMEM_LIMIT),
        name="post_final" if final else "post",
    )(x2, ro, so, p_all, wo, gm, up, dn, pp, pg, gate, fg)


def _bucket_tile():
    max_exact = REL_BUCKETS // 2
    dist = (np.arange(BLOCK)[:, None] + BLOCK) - np.arange(2 * BLOCK)[None, :]
    n = np.maximum(dist, 0)
    nf = np.maximum(n, 1).astype(np.float32)
    scaled = (np.log(nf / np.float32(max_exact)) / np.float32(math.log(REL_MAX_DIST / max_exact))
              * np.float32(REL_BUCKETS - max_exact)).astype(np.float32)
    large = np.minimum(max_exact + scaled.astype(np.int32), REL_BUCKETS - 1)
    bucket = np.where(n < max_exact, n, large)
    valid = (dist >= 0) & (dist < WINDOW)
    return np.where(valid, bucket, -1).astype(np.int32)


def _scan_consts(tt):
    t = np.arange(tt)
    same = (t[:, None] // CHUNK) == (t[None, :] // CHUNK)
    tril = same & (t[:, None] >= t[None, :])
    hh = np.arange(MXU_WIDTH) // HEAD_DIM
    seg = hh[:, None] == hh[None, :]
    return (jnp.asarray(seg, BF16), jnp.asarray(np.concatenate([tril, tril], axis=1), BF16))


def kernel(x, p, norm_mix_g, w_in, mu_shift, w0, w_up, a0, a_up, g_up, vres_down, mu_vres, v0, vres_up,
           k_k, k_a, r_k, lnx_g, lnx_b, sinks, rel_bias, w_out, norm_mlp_g, w_ff_up, w_ff_down,
           ple_proj, ple_norm_g, ple_gate, final_norm_g):
    B, S, _ = x.shape
    depth = w_in.shape[0]
    T = B * S
    n_chunks = SCAN_CHUNKS_PER_STEP
    assert x.shape[2] == D_MODEL and w_in.shape[1:] == (D_MODEL, IN_COLS) and p.shape == (depth, B, S, PLE_DIM)
    assert S % (n_chunks * CHUNK) == 0 and S % TM_PROJ == 0 and TM_PROJ % BLOCK == 0
    row = lambda a: a.reshape(1, -1).astype(F32)

    bkt = jnp.asarray(_bucket_tile())
    consts = _scan_consts(n_chunks * CHUNK)

    w_in_b, w_out_b = w_in.astype(BF16), w_out.astype(BF16)
    up_b, dn_b = w_ff_up.astype(BF16), w_ff_down.astype(BF16)
    pp_b, gate_b = ple_proj.astype(BF16), ple_gate.astype(BF16)
    p_all = p.reshape(depth * T, PLE_DIM)

    x2 = x.reshape(T, D_MODEL)
    v_first = None
    for i in range(depth):
        w_vres = None
        if i > 0:
            w_vres = jnp.pad(vres_down[i - 1], ((0, 0), (0, VRES_PAD - MV_LORA))).astype(BF16)
        zs = _mix(i, x2, row(norm_mix_g[i]), w_in_b, w_vres, bkt, rel_bias.astype(F32), sinks[i].astype(F32),
                  S // TM_PROJ)
        zr = zs[0].reshape(B, S, RWKV_COLS)
        swa_out = zs[1]

        prm = {
            "mu": row(mu_shift[i]), "w0": row(w0[i]), "w_up": w_up[i].astype(BF16), "a0": row(a0[i]),
            "a_up": a_up[i].astype(BF16), "g_up": g_up[i].astype(BF16), "k_k": row(k_k[i]),
            "k_a": row(k_a[i]), "r_k": row(r_k[i]), "lnx_g": row(lnx_g[i]), "lnx_b": row(lnx_b[i]),
        }
        if i == 0:
            rwkv_out, v_first = _rwkv(False, zr, None, None, prm, consts, n_chunks)
        else:
            prm["mu_vres"] = jnp.pad(row(mu_vres[i - 1]), ((0, 0), (0, VRES_PAD - MV_LORA)))
            prm["v0"] = row(v0[i - 1])
            prm["vres_up"] = jnp.pad(vres_up[i - 1], ((0, VRES_PAD - MV_LORA), (0, 0))).astype(BF16)
            zv = zs[2].reshape(B, S, VRES_PAD)
            (rwkv_out,) = _rwkv(True, zr, zv, v_first, prm, consts, n_chunks)

        x2 = _post(i, i == depth - 1, x2, rwkv_out.reshape(T, RWKV_WIDTH), swa_out,
                   p_all, w_out_b, row(norm_mlp_g[i]), up_b, dn_b, pp_b, row(ple_norm_g[i]), gate_b,
                   row(final_norm_g))
    return x2.reshape(B, S, D_MODEL)
```

```python
import functools
import math

import jax
import jax.numpy as jnp
import numpy as np
from jax import lax
from jax.experimental import pallas as pl
from jax.experimental.pallas import tpu as pltpu

F32 = jnp.float32
BF16 = jnp.bfloat16

D_MODEL = 1024
PLE_DIM = 256
HEADS = 8
HEAD_DIM = 64
RWKV_WIDTH = HEADS * HEAD_DIM
DECAY_LORA = 64
AAA_LORA = 64
MV_LORA = 32
GATE_LORA = 128
LNX_EPS = 64e-5
SWA_Q_HEADS = 8
SWA_KV_HEADS = 2
SWA_WIDTH = SWA_Q_HEADS * HEAD_DIM
SWA_KV_WIDTH = SWA_KV_HEADS * HEAD_DIM
WINDOW = 128
BLOCK = 128
ATTN_SCALE = 1.0 / math.sqrt(HEAD_DIM)
REL_BUCKETS = 32
REL_MAX_DIST = 128
D_FF = 4 * D_MODEL
NORM_EPS = 1e-6
RWKV_COLS = 3 * RWKV_WIDTH + DECAY_LORA + AAA_LORA + GATE_LORA
SWA_COLS = SWA_WIDTH + 2 * SWA_KV_WIDTH
IN_COLS = RWKV_COLS + SWA_COLS

DECAY_OFFSET = 0.5
KK_NORM_FLOOR = 1e-12

LANES = 128
MXU_WIDTH = 256
VRES_PAD = LANES
CHUNK = 64
SCAN_CHUNKS_PER_STEP = 8
MASK_VALUE = -1e30

TM_PROJ = 512
FF_CHUNK = 1024
VMEM_LIMIT = 56 * 1024 * 1024


def _rms(x, g):
    ms = jnp.mean(x * x, axis=-1, keepdims=True)
    return (x * lax.rsqrt(ms + NORM_EPS)) * g


def _bdot(a, b):
    return jnp.dot(a.astype(BF16), b.astype(BF16), preferred_element_type=F32)


def _split2(x):
    hi = x.astype(BF16)
    lo = (x - hi.astype(F32)).astype(BF16)
    return hi, lo


def _dot01_left(m01x2, x):
    hi, lo = _split2(x)
    return jnp.dot(m01x2, jnp.concatenate([hi, lo], axis=0), preferred_element_type=F32)


def _head_sums(x, seg):
    half = seg.shape[0]
    return jnp.concatenate(
        [jnp.dot(x[:, c * half:(c + 1) * half].astype(BF16), seg, preferred_element_type=F32)
         for c in range(x.shape[1] // half)], axis=1)


def _sigmoid(x):
    return 0.5 * jnp.tanh(0.5 * x) + 0.5


def _shift(z, prev_row, mu):
    rolled = pltpu.roll(z, 1, 0)
    row = lax.broadcasted_iota(jnp.int32, z.shape, 0)
    prev = jnp.where(row == 0, prev_row, rolled)
    return z + (prev - z) * mu


def _rwkv_kernel(has_vres, n_chunks, *refs):
    if has_vres:
        (zr_ref, zv_ref, vf_ref, mu_ref, w0_ref, wup_ref, a0_ref, aup_ref, gup_ref, kk_ref, ka_ref,
         rk_ref, lg_ref, lb_ref, muv_ref, v0_ref, vup_ref, seg_ref, tril_ref,
         out_ref, h_ref, prev_ref, prevv_ref) = refs
    else:
        (zr_ref, mu_ref, w0_ref, wup_ref, a0_ref, aup_ref, gup_ref, kk_ref, ka_ref,
         rk_ref, lg_ref, lb_ref, seg_ref, tril_ref,
         out_ref, vf_out_ref, h_ref, prev_ref) = refs

    c_idx = pl.program_id(1)

    @pl.when(c_idx == 0)
    def _():
        h_ref[...] = jnp.zeros_like(h_ref)
        prev_ref[...] = jnp.zeros_like(prev_ref)
        if has_vres:
            prevv_ref[...] = jnp.zeros_like(prevv_ref)

    W = RWKV_WIDTH
    z = zr_ref[...]
    tt = z.shape[0]
    zs = _shift(z, prev_ref[...], mu_ref[...])
    prev_ref[...] = z[tt - 1:tt, :]
    r = zs[:, 0:W]
    k = zs[:, W:2 * W]
    v = zs[:, 2 * W:3 * W]
    o1 = 3 * W
    xw = zs[:, o1:o1 + DECAY_LORA]
    xa = zs[:, o1 + DECAY_LORA:o1 + DECAY_LORA + AAA_LORA]
    xg = zs[:, o1 + DECAY_LORA + AAA_LORA:]

    ld = -math.exp(-DECAY_OFFSET) * _sigmoid(w0_ref[...] + _bdot(jnp.tanh(xw), wup_ref[...]))
    a = _sigmoid(a0_ref[...] + _bdot(xa, aup_ref[...]))
    g = _bdot(_sigmoid(xg), gup_ref[...])
    if has_vres:
        zv = zv_ref[...]
        zvs = _shift(zv, prevv_ref[...], muv_ref[...])
        prevv_ref[...] = zv[tt - 1:tt, :]
        v = v + (vf_ref[...] - v) * _sigmoid(v0_ref[...] + _bdot(zvs, vup_ref[...]))
    else:
        vf_out_ref[...] = v

    seg = seg_ref[...]
    kk = k * kk_ref[...]
    kk = kk * jnp.minimum(lax.rsqrt(_head_sums(kk * kk, seg)), 1.0 / KK_NORM_FLOOR)
    kadj = k * (1.0 + (a - 1.0) * ka_ref[...])
    b = kk * a

    c = _dot01_left(tril_ref[...], ld)
    cl = jnp.concatenate(
        [jnp.broadcast_to(c[(ch + 1) * CHUNK - 1:(ch + 1) * CHUNK, :], (CHUNK, W)) for ch in range(n_chunks)],
        axis=0)
    qt = kk * jnp.exp(c - ld)
    rt = r * jnp.exp(c)
    en = jnp.exp(-c)
    kt = kadj * en
    bt = b * en
    el = jnp.exp(cl - c)
    kh = kadj * el
    bh = b * el
    pl_all = jnp.exp(cl)

    L = CHUNK
    P2 = 2 * HEAD_DIM
    lo = lax.broadcasted_iota(jnp.int32, (1, P2), 1) < HEAD_DIM
    ri = lax.broadcasted_iota(jnp.int32, (L, P2), 0)
    ci = lax.broadcasted_iota(jnp.int32, (L, P2), 1) % HEAD_DIM
    strict = ri > ci
    incl = ri >= ci
    eye2 = (ri == ci).astype(F32)
    units = [(ch, pr) for ch in range(n_chunks) for pr in range(HEADS // 2)]

    cat0 = lambda *xs: jnp.concatenate(xs, axis=0)
    cat1 = lambda *xs: jnp.concatenate(xs, axis=1)
    mm = lambda a_, b_: jnp.dot(a_, b_, preferred_element_type=F32)
    nt = lambda a_, b_: lax.dot_general(a_, b_, (((1,), (1,)), ((), ())), preferred_element_type=F32)
    tn = lambda a_, b_: lax.dot_general(a_, b_, (((0,), (0,)), ((), ())), preferred_element_type=F32)
    zero = jnp.zeros((), BF16)

    def bd(x_):
        x_ = x_.astype(BF16)
        return cat0(jnp.where(lo, x_, zero), jnp.where(lo, zero, x_))

    def diag_blocks(x_):
        sel = lo if x_.shape[1] == P2 else jnp.concatenate([lo] * (x_.shape[1] // P2), axis=1)
        return jnp.where(sel, x_[:HEAD_DIM], x_[HEAD_DIM:])

    def per_unit(arr, dtype=BF16):
        arr = arr.astype(dtype)
        return [arr[ch * L:(ch + 1) * L, pr * P2:(pr + 1) * P2] for ch, pr in units]

    qh, rh, kth, bth, vh = per_unit(qt), per_unit(rt), per_unit(kt), per_unit(bt), per_unit(v)
    khh, bhh = per_unit(kh), per_unit(bh)
    rf = per_unit(rt, F32)
    amat = [nt(cat0(q_, r_), cat0(bd(k_), bd(b_))) for q_, r_, k_, b_ in zip(qh, rh, kth, bth)]
    a_qk = [jnp.where(strict, m_[:L, :P2], 0.0).astype(BF16) for m_ in amat]
    a_qb = [jnp.where(strict, m_[:L, P2:], 0.0) for m_ in amat]
    a_rk = [jnp.where(incl, m_[L:, :P2], 0.0).astype(BF16) for m_ in amat]
    a_rb = [jnp.where(incl, m_[L:, P2:], 0.0).astype(BF16) for m_ in amat]
    tinv = [eye2 - m_ for m_ in a_qb]
    pw = [m_.astype(BF16) for m_ in a_qb]
    pw = [mm(p_, bd(p_)).astype(BF16) for p_ in pw]
    for _ in range(int(math.log2(L)) - 3):
        both = [mm(cat0(t_.astype(BF16), p_), bd(p_)) for t_, p_ in zip(tinv, pw)]
        tinv = [t_ + b_[:L] for t_, b_ in zip(tinv, both)]
        pw = [b_[L:].astype(BF16) for b_ in both]
    tinv = [t_ + mm(t_.astype(BF16), bd(p_)) for t_, p_ in zip(tinv, pw)]
    t_sp = [_split2(t_) for t_ in tinv]
    a_sp = [_split2(m_) for m_ in a_qb]
    at_h = [mm(cat0(ah_, al_), bd(th_)) for (ah_, al_), (th_, _) in zip(a_sp, t_sp)]
    at_l = [mm(ah_, bd(tl_)) for (ah_, _), (_, tl_) in zip(a_sp, t_sp)]
    resid = [eye2 - t_ - h_[:L] - h_[L:] - l_ for t_, h_, l_ in zip(tinv, at_h, at_l)]
    tinv = [t_ + mm(th_, bd(r_)) for t_, (th_, _), r_ in zip(tinv, t_sp, resid)]
    av = [mm(cat0(ak_, ar_), bd(v_)) for ak_, ar_, v_ in zip(a_qk, a_rk, vh)]
    x = [mm(t_.astype(BF16), cat1(bd(q_), bd(av_[:L]))).astype(BF16)
         for t_, q_, av_ in zip(tinv, qh, av)]
    ry = [cat1(r_, av_[L:]) - mm(ab_, cat1(bd(x_[:, :P2]), bd(x_[:, P2:])))
          for r_, av_, ab_, x_ in zip(rf, av, a_rb, x)]
    bx = [diag_blocks(tn(b_, x_)) for b_, x_ in zip(bhh, x)]
    kv = [diag_blocks(tn(k_, v_)) for k_, v_ in zip(khh, vh)]
    mmat = [(eye2 * pl_all[ch * L:ch * L + 1, pr * P2:(pr + 1) * P2] - bx_[:, :P2]).astype(BF16)
            for (ch, pr), bx_ in zip(units, bx)]
    nmat = [kv_ - bx_[:, P2:] for kv_, bx_ in zip(kv, bx)]

    n_pairs = HEADS // 2
    assert n_chunks % 2 == 0
    comp = {}
    for ch in range(0, n_chunks, 2):
        for pr in range(n_pairs):
            u0, u1 = ch * n_pairs + pr, (ch + 1) * n_pairs + pr
            both = mm(mmat[u1], cat1(bd(mmat[u0]), bd(nmat[u0])))
            comp[ch, pr] = (both[:, :P2].astype(BF16), both[:, P2:] + nmat[u1])
    h_cur = [h_ref[:, pr * P2:(pr + 1) * P2] for pr in range(n_pairs)]
    y_rows = [None] * n_chunks
    for ch in range(0, n_chunks, 2):
        y0, y1, h_mid = [], [], []
        for pr in range(n_pairs):
            u0 = ch * n_pairs + pr
            m2, n2 = comp[ch, pr]
            both = mm(cat0(ry[u0][:, :P2].astype(BF16), mmat[u0], m2), bd(h_cur[pr]))
            y0.append(both[:L] + ry[u0][:, P2:])
            h_mid.append(both[L:2 * L] + nmat[u0])
            h_cur[pr] = both[2 * L:] + n2
        for pr in range(n_pairs):
            u1 = (ch + 1) * n_pairs + pr
            y1.append(mm(ry[u1][:, :P2].astype(BF16), bd(h_mid[pr])) + ry[u1][:, P2:])
        y_rows[ch], y_rows[ch + 1] = cat1(*y0), cat1(*y1)
    h_ref[...] = cat1(*h_cur)
    y = cat0(*y_rows)

    inv_n = 1.0 / HEAD_DIM
    ym = _head_sums(y, seg) * inv_n
    yc = y - ym
    yv = _head_sums(yc * yc, seg) * inv_n
    y = yc * lax.rsqrt(yv + LNX_EPS)
    y = y * lg_ref[...] + lb_ref[...]
    y = y + _head_sums(r * kadj * rk_ref[...], seg) * v
    out_ref[...] = y * g


def _rwkv(has_vres, zr, zv, v_first, prm, consts, n_chunks):
    B, S, _ = zr.shape
    tt = n_chunks * CHUNK
    W = RWKV_WIDTH
    tile = lambda n: pl.BlockSpec((None, tt, n), lambda b, c: (b, c, 0))
    full = lambda a: pl.BlockSpec(a.shape, lambda b, c: (0,) * a.ndim)
    names = ["mu", "w0", "w_up", "a0", "a_up", "g_up", "k_k", "k_a", "r_k", "lnx_g", "lnx_b"]
    if has_vres:
        names += ["mu_vres", "v0", "vres_up"]
    params = [prm[n] for n in names] + list(consts)
    if has_vres:
        acts = [zr, zv, v_first]
        act_specs = [tile(RWKV_COLS), tile(VRES_PAD), tile(W)]
        out_shape = [jax.ShapeDtypeStruct((B, S, W), F32)]
        out_specs = [tile(W)]
        scratch = [pltpu.VMEM((HEAD_DIM, W), F32), pltpu.VMEM((1, RWKV_COLS), F32),
                   pltpu.VMEM((1, VRES_PAD), F32)]
    else:
        acts = [zr]
        act_specs = [tile(RWKV_COLS)]
        out_shape = [jax.ShapeDtypeStruct((B, S, W), F32)] * 2
        out_specs = [tile(W), tile(W)]
        scratch = [pltpu.VMEM((HEAD_DIM, W), F32), pltpu.VMEM((1, RWKV_COLS), F32)]
    return pl.pallas_call(
        functools.partial(_rwkv_kernel, has_vres, n_chunks),
        grid=(B, S // tt),
        in_specs=act_specs + [full(a) for a in params],
        out_specs=out_specs,
        out_shape=out_shape,
        scratch_shapes=scratch,
        compiler_params=pltpu.CompilerParams(
            dimension_semantics=("arbitrary", "arbitrary"), vmem_limit_bytes=VMEM_LIMIT),
        name="rwkv_vres" if has_vres else "rwkv",
    )(*acts, *params)


_DONE = object()


def _interleave(*gens):
    live = list(gens)
    while live:
        live = [g_ for g_ in live if next(g_, _DONE) is not _DONE]


def _mix_kernel(has_vres, tiles_per_seq, x_ref, g_ref, w_ref, bkt_ref, relb_ref, sink_ref, *refs):
    if has_vres:
        wv_ref, zr_ref, o_ref, zv_ref, bias_ref, kvp_ref = refs
    else:
        zr_ref, o_ref, bias_ref, kvp_ref = refs
    i = pl.program_id(0)
    first = lax.rem(i, tiles_per_seq) == 0
    G2 = 2 * BLOCK
    n_blk = TM_PROJ // BLOCK

    @pl.when(i == 0)
    def _build_bias():
        bkt = bkt_ref[...]
        for h in range(SWA_Q_HEADS):
            acc = jnp.full(bkt.shape, MASK_VALUE, F32)
            for k in range(REL_BUCKETS):
                acc = jnp.where(bkt == k, relb_ref[k, h], acc)
            pair, half = divmod(h, 2)
            bias_ref[pair * BLOCK:(pair + 1) * BLOCK, half * G2:(half + 1) * G2] = acc
        kvp_ref[...] = jnp.zeros_like(kvp_ref)

    h_in = _rms(x_ref[...], g_ref[...]).astype(BF16)
    zq = jnp.dot(h_in, w_ref[:, RWKV_COLS:RWKV_COLS + SWA_WIDTH], preferred_element_type=F32)
    zkv = jnp.dot(h_in, w_ref[:, RWKV_COLS + SWA_WIDTH:IN_COLS], preferred_element_type=F32)

    def rwkv_columns():
        for off in range(0, RWKV_COLS, MXU_WIDTH):
            zr_ref[:, off:off + MXU_WIDTH] = jnp.dot(h_in, w_ref[:, off:off + MXU_WIDTH],
                                                     preferred_element_type=F32)
            yield
        if has_vres:
            zv_ref[...] = jnp.dot(h_in, wv_ref[...], preferred_element_type=F32)
            yield

    lo = lax.broadcasted_iota(jnp.int32, (1, LANES), 1) < HEAD_DIM
    ones_lo = jnp.broadcast_to(lo.astype(BF16), (G2, LANES))
    ones_hi = jnp.broadcast_to((~lo).astype(BF16), (G2, LANES))
    top = lax.broadcasted_iota(jnp.int32, (2 * BLOCK, 1), 0) < BLOCK
    prev_col = (lax.broadcasted_iota(jnp.int32, (2 * BLOCK, 2 * G2), 1) % G2) < BLOCK
    units = [(j, g) for j in range(n_blk) for g in range(SWA_KV_HEADS)]

    def attention():
        kk, vv = {}, {}
        for j in range(n_blk):
            if j == 0:
                kv = jnp.concatenate([kvp_ref[...], zkv[0:BLOCK, :]], axis=0)
            else:
                kv = zkv[(j - 1) * BLOCK:(j + 1) * BLOCK, :]
            kcol, vcol = kv[:, :LANES], kv[:, LANES:]
            for arr, dst, extra in ((kcol, kk, None), (vcol, vv, (ones_lo, ones_hi))):
                g0a = jnp.where(lo, arr, 0.0)
                g1b = jnp.where(lo, 0.0, arr)
                g0b = pltpu.roll(g0a, HEAD_DIM, 1)
                g1a = pltpu.roll(g1b, HEAD_DIM, 1)
                for g, (xa, xb) in enumerate(((g0a, g0b), (g1a, g1b))):
                    xa, xb = xa.astype(BF16), xb.astype(BF16)
                    if extra is None:
                        dst[j, g] = jnp.concatenate([xa, xb], axis=0)
                    else:
                        dst[j, g] = jnp.concatenate(
                            [jnp.concatenate([xa, extra[0]], axis=1),
                             jnp.concatenate([xb, extra[1]], axis=1)], axis=0)
            yield
        kvp_ref[...] = zkv[TM_PROJ - BLOCK:TM_PROJ, :]

        def q_pairs(j, g):
            q2 = (zq[j * BLOCK:(j + 1) * BLOCK, g * 2 * LANES:(g + 1) * 2 * LANES] * ATTN_SCALE).astype(BF16)
            return jnp.concatenate([q2[:, :LANES], q2[:, LANES:]], axis=0)

        logits = [lax.dot_general(q_pairs(j, g), kk[j, g], (((1,), (1,)), ((), ())), preferred_element_type=F32)
                  + bias_ref[g * 2 * BLOCK:(g + 1) * 2 * BLOCK, :] for j, g in units]
        logits = [jnp.where(prev_col & first, MASK_VALUE, l_) if j == 0 else l_
                  for (j, g), l_ in zip(units, logits)]
        yield
        sinks = [[jnp.where(top, sink_ref[4 * g + half], sink_ref[4 * g + 2 + half]) for half in range(2)]
                 for j, g in units]
        ms = [[jnp.maximum(jnp.max(l_[:, half * G2:(half + 1) * G2], axis=-1, keepdims=True), s_[half])
               for half in range(2)] for l_, s_ in zip(logits, sinks)]
        yield
        es = [jnp.concatenate([jnp.exp(l_[:, half * G2:(half + 1) * G2] - m_[half]) for half in range(2)],
                              axis=1).astype(BF16) for l_, m_ in zip(logits, ms)]
        yield
        res = [jnp.dot(e_, vv[u], preferred_element_type=F32) for e_, u in zip(es, units)]
        yield
        for (j, g), r_, m_, s_ in zip(units, res, ms, sinks):
            den = r_[:, LANES:] + jnp.where(lo, jnp.exp(s_[0] - m_[0]), jnp.exp(s_[1] - m_[1]))
            out = r_[:, :LANES] / den
            rows = slice(j * BLOCK, (j + 1) * BLOCK)
            o_ref[rows, (2 * g) * LANES:(2 * g + 1) * LANES] = out[:BLOCK]
            o_ref[rows, (2 * g + 1) * LANES:(2 * g + 2) * LANES] = out[BLOCK:]
            if g == SWA_KV_HEADS - 1:
                yield

    _interleave(attention(), rwkv_columns())


def _mix(layer, x2, g, w_all, w_vres, bkt, rel_bias, sinks, tiles_per_seq):
    T = x2.shape[0]
    smem = lambda a: pl.BlockSpec(a.shape, lambda i: (0,) * a.ndim, memory_space=pltpu.SMEM)
    in_specs = [
        pl.BlockSpec((TM_PROJ, D_MODEL), lambda i: (i, 0)),
        pl.BlockSpec((1, D_MODEL), lambda i: (0, 0)),
        pl.BlockSpec((None, D_MODEL, IN_COLS), lambda i: (layer, 0, 0)),
        pl.BlockSpec(bkt.shape, lambda i: (0, 0)),
        smem(rel_bias),
        smem(sinks),
    ]
    args = [x2, g, w_all, bkt, rel_bias, sinks]
    widths = [RWKV_COLS, SWA_WIDTH]
    if w_vres is not None:
        widths.append(VRES_PAD)
        in_specs.append(pl.BlockSpec(w_vres.shape, lambda i: (0, 0)))
        args.append(w_vres)
    return pl.pallas_call(
        functools.partial(_mix_kernel, w_vres is not None, tiles_per_seq),
        grid=(T // TM_PROJ,),
        in_specs=in_specs,
        out_specs=[pl.BlockSpec((TM_PROJ, n), lambda i: (i, 0)) for n in widths],
        out_shape=[jax.ShapeDtypeStruct((T, n), F32) for n in widths],
        scratch_shapes=[pltpu.VMEM((SWA_Q_HEADS // 2 * BLOCK, 2 * 2 * BLOCK), F32),
                        pltpu.VMEM((BLOCK, 2 * SWA_KV_WIDTH), F32)],
        compiler_params=pltpu.CompilerParams(
            dimension_semantics=("arbitrary",), vmem_limit_bytes=VMEM_LIMIT),
        name="mix",
    )(*args)


def _post_kernel(final, x_ref, ro_ref, so_ref, p_ref, wo_ref, gm_ref, up_ref, dn_ref, pp_ref, pg_ref,
                 gate_ref, fg_ref, o_ref):
    W = RWKV_WIDTH
    x = x_ref[...]
    x = x + _bdot(ro_ref[...], wo_ref[0:W, :]) + _bdot(so_ref[...], wo_ref[W:2 * W, :])
    u = _rms(x, gm_ref[...]).astype(BF16)
    acc = jnp.zeros_like(x)
    for c in range(D_FF // FF_CHUNK):
        cs = slice(c * FF_CHUNK, (c + 1) * FF_CHUNK)
        hid = jnp.dot(u, up_ref[:, cs], preferred_element_type=F32)
        hid = jnp.square(jnp.maximum(hid, 0.0))
        acc = acc + jnp.dot(hid.astype(BF16), dn_ref[cs, :], preferred_element_type=F32)
    x = x + acc
    e = _rms(_bdot(p_ref[...], pp_ref[...]), pg_ref[...])
    x = x + e * _sigmoid(_bdot(x, gate_ref[...]))
    if final:
        x = _rms(x, fg_ref[...])
    o_ref[...] = x


def _post(layer, final, x2, ro, so, p_all, wo, gm, up, dn, pp, pg, gate, fg):
    T = x2.shape[0]
    n_tiles = T // TM_PROJ
    tile = lambda n: pl.BlockSpec((TM_PROJ, n), lambda i: (i, 0))
    p_tile = pl.BlockSpec((TM_PROJ, PLE_DIM), lambda i: (layer * n_tiles + i, 0))
    row = lambda a: pl.BlockSpec(a.shape, lambda i: (0, 0), pipeline_mode=pl.Buffered(1))
    mat = lambda a: pl.BlockSpec((None,) + a.shape[1:], lambda i: (layer, 0, 0), pipeline_mode=pl.Buffered(1))
    return pl.pallas_call(
        functools.partial(_post_kernel, final),
        grid=(n_tiles,),
        in_specs=[tile(D_MODEL), tile(RWKV_WIDTH), tile(SWA_WIDTH), p_tile,
                  mat(wo), row(gm), mat(up), mat(dn), mat(pp), row(pg), mat(gate), row(fg)],
        out_specs=tile(D_MODEL),
        out_shape=jax.ShapeDtypeStruct((T, D_MODEL), F32),
        compiler_params=pltpu.CompilerParams(
            dimension_semantics=("arbitrary",), vmem_limit_bytes=VMEM_LIMIT),
        name="post_final" if final else "post",
    )(x2, ro, so, p_all, wo, gm, up, dn, pp, pg, gate, fg)


def _bucket_tile():
    max_exact = REL_BUCKETS // 2
    dist = (np.arange(BLOCK)[:, None] + BLOCK) - np.arange(2 * BLOCK)[None, :]
    n = np.maximum(dist, 0)
    nf = np.maximum(n, 1).astype(np.float32)
    scaled = (np.log(nf / np.float32(max_exact)) / np.float32(math.log(REL_MAX_DIST / max_exact))
              * np.float32(REL_BUCKETS - max_exact)).astype(np.float32)
    large = np.minimum(max_exact + scaled.astype(np.int32), REL_BUCKETS - 1)
    bucket = np.where(n < max_exact, n, large)
    valid = (dist >= 0) & (dist < WINDOW)
    return np.where(valid, bucket, -1).astype(np.int32)


def _scan_consts(tt):
    t = np.arange(tt)
    same = (t[:, None] // CHUNK) == (t[None, :] // CHUNK)
    tril = same & (t[:, None] >= t[None, :])
    hh = np.arange(MXU_WIDTH) // HEAD_DIM
    seg = hh[:, None] == hh[None, :]
    return (jnp.asarray(seg, BF16), jnp.asarray(np.concatenate([tril, tril], axis=1), BF16))


def kernel(x, p, norm_mix_g, w_in, mu_shift, w0, w_up, a0, a_up, g_up, vres_down, mu_vres, v0, vres_up,
           k_k, k_a, r_k, lnx_g, lnx_b, sinks, rel_bias, w_out, norm_mlp_g, w_ff_up, w_ff_down,
           ple_proj, ple_norm_g, ple_gate, final_norm_g):
    B, S, _ = x.shape
    depth = w_in.shape[0]
    T = B * S
    n_chunks = SCAN_CHUNKS_PER_STEP
    assert x.shape[2] == D_MODEL and w_in.shape[1:] == (D_MODEL, IN_COLS) and p.shape == (depth, B, S, PLE_DIM)
    assert S % (n_chunks * CHUNK) == 0 and S % TM_PROJ == 0 and TM_PROJ % BLOCK == 0
    row = lambda a: a.reshape(1, -1).astype(F32)

    bkt = jnp.asarray(_bucket_tile())
    consts = _scan_consts(n_chunks * CHUNK)

    w_in_b, w_out_b = w_in.astype(BF16), w_out.astype(BF16)
    up_b, dn_b = w_ff_up.astype(BF16), w_ff_down.astype(BF16)
    pp_b, gate_b = ple_proj.astype(BF16), ple_gate.astype(BF16)
    p_all = p.reshape(depth * T, PLE_DIM)

    x2 = x.reshape(T, D_MODEL)
    v_first = None
    for i in range(depth):
        w_vres = None
        if i > 0:
            w_vres = jnp.pad(vres_down[i - 1], ((0, 0), (0, VRES_PAD - MV_LORA))).astype(BF16)
        zs = _mix(i, x2, row(norm_mix_g[i]), w_in_b, w_vres, bkt, rel_bias.astype(F32), sinks[i].astype(F32),
                  S // TM_PROJ)
        zr = zs[0].reshape(B, S, RWKV_COLS)
        swa_out = zs[1]

        prm = {
            "mu": row(mu_shift[i]), "w0": row(w0[i]), "w_up": w_up[i].astype(BF16), "a0": row(a0[i]),
            "a_up": a_up[i].astype(BF16), "g_up": g_up[i].astype(BF16), "k_k": row(k_k[i]),
            "k_a": row(k_a[i]), "r_k": row(r_k[i]), "lnx_g": row(lnx_g[i]), "lnx_b": row(lnx_b[i]),
        }
        if i == 0:
            rwkv_out, v_first = _rwkv(False, zr, None, None, prm, consts, n_chunks)
        else:
            prm["mu_vres"] = jnp.pad(row(mu_vres[i - 1]), ((0, 0), (0, VRES_PAD - MV_LORA)))
            prm["v0"] = row(v0[i - 1])
            prm["vres_up"] = jnp.pad(vres_up[i - 1], ((0, VRES_PAD - MV_LORA), (0, 0))).astype(BF16)
            zv = zs[2].reshape(B, S, VRES_PAD)
            (rwkv_out,) = _rwkv(True, zr, zv, v_first, prm, consts, n_chunks)

        x2 = _post(i, i == depth - 1, x2, rwkv_out.reshape(T, RWKV_WIDTH), swa_out,
                   p_all, w_out_b, row(norm_mlp_g[i]), up_b, dn_b, pp_b, row(ple_norm_g[i]), gate_b,
                   row(final_norm_g))
    return x2.reshape(B, S, D_MODEL)
```

```python
import functools
import math

import jax
import jax.numpy as jnp
import numpy as np
from jax import lax
from jax.experimental import pallas as pl
from jax.experimental.pallas import tpu as pltpu

F32 = jnp.float32
BF16 = jnp.bfloat16

D_MODEL = 1024
PLE_DIM = 256
HEADS = 8
HEAD_DIM = 64
RWKV_WIDTH = HEADS * HEAD_DIM
DECAY_LORA = 64
AAA_LORA = 64
MV_LORA = 32
GATE_LORA = 128
LNX_EPS = 64e-5
SWA_Q_HEADS = 8
SWA_KV_HEADS = 2
SWA_WIDTH = SWA_Q_HEADS * HEAD_DIM
SWA_KV_WIDTH = SWA_KV_HEADS * HEAD_DIM
WINDOW = 128
BLOCK = 128
ATTN_SCALE = 1.0 / math.sqrt(HEAD_DIM)
REL_BUCKETS = 32
REL_MAX_DIST = 128
D_FF = 4 * D_MODEL
NORM_EPS = 1e-6
RWKV_COLS = 3 * RWKV_WIDTH + DECAY_LORA + AAA_LORA + GATE_LORA
SWA_COLS = SWA_WIDTH + 2 * SWA_KV_WIDTH
IN_COLS = RWKV_COLS + SWA_COLS

DECAY_OFFSET = 0.5
KK_NORM_FLOOR = 1e-12

LANES = 128
MXU_WIDTH = 256
VRES_PAD = LANES
CHUNK = 64
SCAN_CHUNKS_PER_STEP = 8
MASK_VALUE = -1e30

TM_PROJ = 512
FF_CHUNK = 1024
VMEM_LIMIT = 56 * 1024 * 1024


def _rms(x, g):
    ms = jnp.mean(x * x, axis=-1, keepdims=True)
    return (x * lax.rsqrt(ms + NORM_EPS)) * g


def _bdot(a, b):
    return jnp.dot(a.astype(BF16), b.astype(BF16), preferred_element_type=F32)


def _split2(x):
    hi = x.astype(BF16)
    lo = (x - hi.astype(F32)).astype(BF16)
    return hi, lo


def _dot01_left(m01x2, x):
    hi, lo = _split2(x)
    return jnp.dot(m01x2, jnp.concatenate([hi, lo], axis=0), preferred_element_type=F32)


def _head_sums(x, seg):
    half = seg.shape[0]
    return jnp.concatenate(
        [jnp.dot(x[:, c * half:(c + 1) * half].astype(BF16), seg, preferred_element_type=F32)
         for c in range(x.shape[1] // half)], axis=1)


def _sigmoid(x):
    return 0.5 * jnp.tanh(0.5 * x) + 0.5


def _shift(z, prev_row, mu):
    rolled = pltpu.roll(z, 1, 0)
    row = lax.broadcasted_iota(jnp.int32, z.shape, 0)
    prev = jnp.where(row == 0, prev_row, rolled)
    return z + (prev - z) * mu


def _rwkv_kernel(has_vres, n_chunks, *refs):
    if has_vres:
        (zr_ref, zv_ref, vf_ref, mu_ref, w0_ref, wup_ref, a0_ref, aup_ref, gup_ref, kk_ref, ka_ref,
         rk_ref, lg_ref, lb_ref, muv_ref, v0_ref, vup_ref, seg_ref, tril_ref,
         out_ref, h_ref, prev_ref, prevv_ref) = refs
    else:
        (zr_ref, mu_ref, w0_ref, wup_ref, a0_ref, aup_ref, gup_ref, kk_ref, ka_ref,
         rk_ref, lg_ref, lb_ref, seg_ref, tril_ref,
         out_ref, vf_out_ref, h_ref, prev_ref) = refs

    c_idx = pl.program_id(1)

    @pl.when(c_idx == 0)
    def _():
        h_ref[...] = jnp.zeros_like(h_ref)
        prev_ref[...] = jnp.zeros_like(prev_ref)
        if has_vres:
            prevv_ref[...] = jnp.zeros_like(prevv_ref)

    W = RWKV_WIDTH
    z = zr_ref[...]
    tt = z.shape[0]
    zs = _shift(z, prev_ref[...], mu_ref[...])
    prev_ref[...] = z[tt - 1:tt, :]
    r = zs[:, 0:W]
    k = zs[:, W:2 * W]
    v = zs[:, 2 * W:3 * W]
    o1 = 3 * W
    xw = zs[:, o1:o1 + DECAY_LORA]
    xa = zs[:, o1 + DECAY_LORA:o1 + DECAY_LORA + AAA_LORA]
    xg = zs[:, o1 + DECAY_LORA + AAA_LORA:]

    ld = -math.exp(-DECAY_OFFSET) * _sigmoid(w0_ref[...] + _bdot(jnp.tanh(xw), wup_ref[...]))
    a = _sigmoid(a0_ref[...] + _bdot(xa, aup_ref[...]))
    g = _bdot(_sigmoid(xg), gup_ref[...])
    if has_vres:
        zv = zv_ref[...]
        zvs = _shift(zv, prevv_ref[...], muv_ref[...])
        prevv_ref[...] = zv[tt - 1:tt, :]
        v = v + (vf_ref[...] - v) * _sigmoid(v0_ref[...] + _bdot(zvs, vup_ref[...]))
    else:
        vf_out_ref[...] = v

    seg = seg_ref[...]
    kk = k * kk_ref[...]
    kk = kk * jnp.minimum(lax.rsqrt(_head_sums(kk * kk, seg)), 1.0 / KK_NORM_FLOOR)
    kadj = k * (1.0 + (a - 1.0) * ka_ref[...])
    b = kk * a

    c = _dot01_left(tril_ref[...], ld)
    cl = jnp.concatenate(
        [jnp.broadcast_to(c[(ch + 1) * CHUNK - 1:(ch + 1) * CHUNK, :], (CHUNK, W)) for ch in range(n_chunks)],
        axis=0)
    qt = kk * jnp.exp(c - ld)
    rt = r * jnp.exp(c)
    en = jnp.exp(-c)
    kt = kadj * en
    bt = b * en
    el = jnp.exp(cl - c)
    kh = kadj * el
    bh = b * el
    pl_all = jnp.exp(cl)

    L = CHUNK
    P2 = 2 * HEAD_DIM
    lo = lax.broadcasted_iota(jnp.int32, (1, P2), 1) < HEAD_DIM
    ri = lax.broadcasted_iota(jnp.int32, (L, P2), 0)
    ci = lax.broadcasted_iota(jnp.int32, (L, P2), 1) % HEAD_DIM
    strict = ri > ci
    incl = ri >= ci
    eye2 = (ri == ci).astype(F32)
    units = [(ch, pr) for ch in range(n_chunks) for pr in range(HEADS // 2)]

    cat0 = lambda *xs: jnp.concatenate(xs, axis=0)
    cat1 = lambda *xs: jnp.concatenate(xs, axis=1)
    mm = lambda a_, b_: jnp.dot(a_, b_, preferred_element_type=F32)
    nt = lambda a_, b_: lax.dot_general(a_, b_, (((1,), (1,)), ((), ())), preferred_element_type=F32)
    tn = lambda a_, b_: lax.dot_general(a_, b_, (((0,), (0,)), ((), ())), preferred_element_type=F32)
    zero = jnp.zeros((), BF16)

    def bd(x_):
        x_ = x_.astype(BF16)
        return cat0(jnp.where(lo, x_, zero), jnp.where(lo, zero, x_))

    def diag_blocks(x_):
        sel = lo if x_.shape[1] == P2 else jnp.concatenate([lo] * (x_.shape[1] // P2), axis=1)
        return jnp.where(sel, x_[:HEAD_DIM], x_[HEAD_DIM:])

    def per_unit(arr, dtype=BF16):
        arr = arr.astype(dtype)
        return [arr[ch * L:(ch + 1) * L, pr * P2:(pr + 1) * P2] for ch, pr in units]

    qh, rh, kth, bth, vh = per_unit(qt), per_unit(rt), per_unit(kt), per_unit(bt), per_unit(v)
    khh, bhh = per_unit(kh), per_unit(bh)
    rf = per_unit(rt, F32)
    amat = [nt(cat0(q_, r_), cat0(bd(k_), bd(b_))) for q_, r_, k_, b_ in zip(qh, rh, kth, bth)]
    a_qk = [jnp.where(strict, m_[:L, :P2], 0.0).astype(BF16) for m_ in amat]
    a_qb = [jnp.where(strict, m_[:L, P2:], 0.0) for m_ in amat]
    a_rk = [jnp.where(incl, m_[L:, :P2], 0.0).astype(BF16) for m_ in amat]
    a_rb = [jnp.where(incl, m_[L:, P2:], 0.0).astype(BF16) for m_ in amat]
    tinv = [eye2 - m_ for m_ in a_qb]
    pw = [m_.astype(BF16) for m_ in a_qb]
    pw = [mm(p_, bd(p_)).astype(BF16) for p_ in pw]
    for _ in range(int(math.log2(L)) - 3):
        both = [mm(cat0(t_.astype(BF16), p_), bd(p_)) for t_, p_ in zip(tinv, pw)]
        tinv = [t_ + b_[:L] for t_, b_ in zip(tinv, both)]
        pw = [b_[L:].astype(BF16) for b_ in both]
    tinv = [t_ + mm(t_.astype(BF16), bd(p_)) for t_, p_ in zip(tinv, pw)]
    t_sp = [_split2(t_) for t_ in tinv]
    a_sp = [_split2(m_) for m_ in a_qb]
    at_h = [mm(cat0(ah_, al_), bd(th_)) for (ah_, al_), (th_, _) in zip(a_sp, t_sp)]
    at_l = [mm(ah_, bd(tl_)) for (ah_, _), (_, tl_) in zip(a_sp, t_sp)]
    resid = [eye2 - t_ - h_[:L] - h_[L:] - l_ for t_, h_, l_ in zip(tinv, at_h, at_l)]
    tinv = [t_ + mm(th_, bd(r_)) for t_, (th_, _), r_ in zip(tinv, t_sp, resid)]
    av = [mm(cat0(ak_, ar_), bd(v_)) for ak_, ar_, v_ in zip(a_qk, a_rk, vh)]
    x = [mm(t_.astype(BF16), cat1(bd(q_), bd(av_[:L]))).astype(BF16)
         for t_, q_, av_ in zip(tinv, qh, av)]
    ry = [cat1(r_, av_[L:]) - mm(ab_, cat1(bd(x_[:, :P2]), bd(x_[:, P2:])))
          for r_, av_, ab_, x_ in zip(rf, av, a_rb, x)]
    bx = [diag_blocks(tn(b_, x_)) for b_, x_ in zip(bhh, x)]
    kv = [diag_blocks(tn(k_, v_)) for k_, v_ in zip(khh, vh)]
    mmat = [(eye2 * pl_all[ch * L:ch * L + 1, pr * P2:(pr + 1) * P2] - bx_[:, :P2]).astype(BF16)
            for (ch, pr), bx_ in zip(units, bx)]
    nmat = [kv_ - bx_[:, P2:] for kv_, bx_ in zip(kv, bx)]

    n_pairs = HEADS // 2
    assert n_chunks % 2 == 0
    comp = {}
    for ch in range(0, n_chunks, 2):
        for pr in range(n_pairs):
            u0, u1 = ch * n_pairs + pr, (ch + 1) * n_pairs + pr
            both = mm(mmat[u1], cat1(bd(mmat[u0]), bd(nmat[u0])))
            comp[ch, pr] = (both[:, :P2].astype(BF16), both[:, P2:] + nmat[u1])
    h_cur = [h_ref[:, pr * P2:(pr + 1) * P2] for pr in range(n_pairs)]
    y_rows = [None] * n_chunks
    for ch in range(0, n_chunks, 2):
        y0, y1, h_mid = [], [], []
        for pr in range(n_pairs):
            u0 = ch * n_pairs + pr
            m2, n2 = comp[ch, pr]
            both = mm(cat0(ry[u0][:, :P2].astype(BF16), mmat[u0], m2), bd(h_cur[pr]))
            y0.append(both[:L] + ry[u0][:, P2:])
            h_mid.append(both[L:2 * L] + nmat[u0])
            h_cur[pr] = both[2 * L:] + n2
        for pr in range(n_pairs):
            u1 = (ch + 1) * n_pairs + pr
            y1.append(mm(ry[u1][:, :P2].astype(BF16), bd(h_mid[pr])) + ry[u1][:, P2:])
        y_rows[ch], y_rows[ch + 1] = cat1(*y0), cat1(*y1)
    h_ref[...] = cat1(*h_cur)
    y = cat0(*y_rows)

    inv_n = 1.0 / HEAD_DIM
    ym = _head_sums(y, seg) * inv_n
    yc = y - ym
    yv = _head_sums(yc * yc, seg) * inv_n
    y = yc * lax.rsqrt(yv + LNX_EPS)
    y = y * lg_ref[...] + lb_ref[...]
    y = y + _head_sums(r * kadj * rk_ref[...], seg) * v
    out_ref[...] = (y * g).astype(out_ref.dtype)


def _rwkv(has_vres, zr, zv, v_first, prm, consts, n_chunks):
    B, S, _ = zr.shape
    tt = n_chunks * CHUNK
    W = RWKV_WIDTH
    tile = lambda n: pl.BlockSpec((None, tt, n), lambda b, c: (b, c, 0))
    full = lambda a: pl.BlockSpec(a.shape, lambda b, c: (0,) * a.ndim)
    names = ["mu", "w0", "w_up", "a0", "a_up", "g_up", "k_k", "k_a", "r_k", "lnx_g", "lnx_b"]
    if has_vres:
        names += ["mu_vres", "v0", "vres_up"]
    params = [prm[n] for n in names] + list(consts)
    if has_vres:
        acts = [zr, zv, v_first]
        act_specs = [tile(RWKV_COLS), tile(VRES_PAD), tile(W)]
        out_shape = [jax.ShapeDtypeStruct((B, S, W), BF16)]
        out_specs = [tile(W)]
        scratch = [pltpu.VMEM((HEAD_DIM, W), F32), pltpu.VMEM((1, RWKV_COLS), F32),
                   pltpu.VMEM((1, VRES_PAD), F32)]
    else:
        acts = [zr]
        act_specs = [tile(RWKV_COLS)]
        out_shape = [jax.ShapeDtypeStruct((B, S, W), BF16), jax.ShapeDtypeStruct((B, S, W), F32)]
        out_specs = [tile(W), tile(W)]
        scratch = [pltpu.VMEM((HEAD_DIM, W), F32), pltpu.VMEM((1, RWKV_COLS), F32)]
    return pl.pallas_call(
        functools.partial(_rwkv_kernel, has_vres, n_chunks),
        grid=(B, S // tt),
        in_specs=act_specs + [full(a) for a in params],
        out_specs=out_specs,
        out_shape=out_shape,
        scratch_shapes=scratch,
        compiler_params=pltpu.CompilerParams(
            dimension_semantics=("arbitrary", "arbitrary"), vmem_limit_bytes=VMEM_LIMIT),
        name="rwkv_vres" if has_vres else "rwkv",
    )(*acts, *params)


_DONE = object()


def _interleave(*gens):
    live = list(gens)
    while live:
        live = [g_ for g_ in live if next(g_, _DONE) is not _DONE]


def _mix_kernel(has_vres, tiles_per_seq, x_ref, g_ref, w_ref, bkt_ref, relb_ref, sink_ref, *refs):
    if has_vres:
        wv_ref, zr_ref, o_ref, zv_ref, bias_ref, kvp_ref = refs
    else:
        zr_ref, o_ref, bias_ref, kvp_ref = refs
    i = pl.program_id(0)
    first = lax.rem(i, tiles_per_seq) == 0
    G2 = 2 * BLOCK
    n_blk = TM_PROJ // BLOCK

    @pl.when(i == 0)
    def _build_bias():
        bkt = bkt_ref[...]
        for h in range(SWA_Q_HEADS):
            acc = jnp.full(bkt.shape, MASK_VALUE, F32)
            for k in range(REL_BUCKETS):
                acc = jnp.where(bkt == k, relb_ref[k, h], acc)
            pair, half = divmod(h, 2)
            bias_ref[pair * BLOCK:(pair + 1) * BLOCK, half * G2:(half + 1) * G2] = acc
        kvp_ref[...] = jnp.zeros_like(kvp_ref)

    h_in = _rms(x_ref[...], g_ref[...]).astype(BF16)
    zq = jnp.dot(h_in, w_ref[:, RWKV_COLS:RWKV_COLS + SWA_WIDTH], preferred_element_type=F32)
    zkv = jnp.dot(h_in, w_ref[:, RWKV_COLS + SWA_WIDTH:IN_COLS], preferred_element_type=F32)

    def rwkv_columns():
        for off in range(0, RWKV_COLS, MXU_WIDTH):
            zr_ref[:, off:off + MXU_WIDTH] = jnp.dot(h_in, w_ref[:, off:off + MXU_WIDTH],
                                                     preferred_element_type=F32)
            yield
        if has_vres:
            zv_ref[...] = jnp.dot(h_in, wv_ref[...], preferred_element_type=F32)
            yield

    lo = lax.broadcasted_iota(jnp.int32, (1, LANES), 1) < HEAD_DIM
    ones_lo = jnp.broadcast_to(lo.astype(BF16), (G2, LANES))
    ones_hi = jnp.broadcast_to((~lo).astype(BF16), (G2, LANES))
    top = lax.broadcasted_iota(jnp.int32, (2 * BLOCK, 1), 0) < BLOCK
    prev_col = (lax.broadcasted_iota(jnp.int32, (2 * BLOCK, 2 * G2), 1) % G2) < BLOCK
    units = [(j, g) for j in range(n_blk) for g in range(SWA_KV_HEADS)]

    def attention():
        kk, vv = {}, {}
        for j in range(n_blk):
            if j == 0:
                kv = jnp.concatenate([kvp_ref[...], zkv[0:BLOCK, :]], axis=0)
            else:
                kv = zkv[(j - 1) * BLOCK:(j + 1) * BLOCK, :]
            kcol, vcol = kv[:, :LANES], kv[:, LANES:]
            for arr, dst, extra in ((kcol, kk, None), (vcol, vv, (ones_lo, ones_hi))):
                g0a = jnp.where(lo, arr, 0.0)
                g1b = jnp.where(lo, 0.0, arr)
                g0b = pltpu.roll(g0a, HEAD_DIM, 1)
                g1a = pltpu.roll(g1b, HEAD_DIM, 1)
                for g, (xa, xb) in enumerate(((g0a, g0b), (g1a, g1b))):
                    xa, xb = xa.astype(BF16), xb.astype(BF16)
                    if extra is None:
                        dst[j, g] = jnp.concatenate([xa, xb], axis=0)
                    else:
                        dst[j, g] = jnp.concatenate(
                            [jnp.concatenate([xa, extra[0]], axis=1),
                             jnp.concatenate([xb, extra[1]], axis=1)], axis=0)
            yield
        kvp_ref[...] = zkv[TM_PROJ - BLOCK:TM_PROJ, :]

        def q_pairs(j, g):
            q2 = (zq[j * BLOCK:(j + 1) * BLOCK, g * 2 * LANES:(g + 1) * 2 * LANES] * ATTN_SCALE).astype(BF16)
            return jnp.concatenate([q2[:, :LANES], q2[:, LANES:]], axis=0)

        logits = [lax.dot_general(q_pairs(j, g), kk[j, g], (((1,), (1,)), ((), ())), preferred_element_type=F32)
                  + bias_ref[g * 2 * BLOCK:(g + 1) * 2 * BLOCK, :] for j, g in units]
        logits = [jnp.where(prev_col & first, MASK_VALUE, l_) if j == 0 else l_
                  for (j, g), l_ in zip(units, logits)]
        yield
        sinks = [[jnp.where(top, sink_ref[4 * g + half], sink_ref[4 * g + 2 + half]) for half in range(2)]
                 for j, g in units]
        ms = [[jnp.maximum(jnp.max(l_[:, half * G2:(half + 1) * G2], axis=-1, keepdims=True), s_[half])
               for half in range(2)] for l_, s_ in zip(logits, sinks)]
        yield
        es = [jnp.concatenate([jnp.exp(l_[:, half * G2:(half + 1) * G2] - m_[half]) for half in range(2)],
                              axis=1).astype(BF16) for l_, m_ in zip(logits, ms)]
        yield
        res = [jnp.dot(e_, vv[u], preferred_element_type=F32) for e_, u in zip(es, units)]
        yield
        for (j, g), r_, m_, s_ in zip(units, res, ms, sinks):
            den = r_[:, LANES:] + jnp.where(lo, jnp.exp(s_[0] - m_[0]), jnp.exp(s_[1] - m_[1]))
            out = r_[:, :LANES] / den
            rows = slice(j * BLOCK, (j + 1) * BLOCK)
            out = out.astype(o_ref.dtype)
            o_ref[rows, (2 * g) * LANES:(2 * g + 1) * LANES] = out[:BLOCK]
            o_ref[rows, (2 * g + 1) * LANES:(2 * g + 2) * LANES] = out[BLOCK:]
            if g == SWA_KV_HEADS - 1:
                yield

    _interleave(attention(), rwkv_columns())


def _mix(layer, x2, g, w_all, w_vres, bkt, rel_bias, sinks, tiles_per_seq):
    T = x2.shape[0]
    smem = lambda a: pl.BlockSpec(a.shape, lambda i: (0,) * a.ndim, memory_space=pltpu.SMEM)
    in_specs = [
        pl.BlockSpec((TM_PROJ, D_MODEL), lambda i: (i, 0)),
        pl.BlockSpec((1, D_MODEL), lambda i: (0, 0)),
        pl.BlockSpec((None, D_MODEL, IN_COLS), lambda i: (layer, 0, 0)),
        pl.BlockSpec(bkt.shape, lambda i: (0, 0)),
        smem(rel_bias),
        smem(sinks),
    ]
    args = [x2, g, w_all, bkt, rel_bias, sinks]
    widths = [RWKV_COLS, SWA_WIDTH]
    if w_vres is not None:
        widths.append(VRES_PAD)
        in_specs.append(pl.BlockSpec(w_vres.shape, lambda i: (0, 0)))
        args.append(w_vres)
    return pl.pallas_call(
        functools.partial(_mix_kernel, w_vres is not None, tiles_per_seq),
        grid=(T // TM_PROJ,),
        in_specs=in_specs,
        out_specs=[pl.BlockSpec((TM_PROJ, n), lambda i: (i, 0)) for n in widths],
        out_shape=[jax.ShapeDtypeStruct((T, n), BF16 if k == 1 else F32) for k, n in enumerate(widths)],
        scratch_shapes=[pltpu.VMEM((SWA_Q_HEADS // 2 * BLOCK, 2 * 2 * BLOCK), F32),
                        pltpu.VMEM((BLOCK, 2 * SWA_KV_WIDTH), F32)],
        compiler_params=pltpu.CompilerParams(
            dimension_semantics=("arbitrary",), vmem_limit_bytes=VMEM_LIMIT),
        name="mix",
    )(*args)


def _post_kernel(final, x_ref, ro_ref, so_ref, p_ref, wo_ref, gm_ref, up_ref, dn_ref, pp_ref, pg_ref,
                 gate_ref, fg_ref, o_ref):
    W = RWKV_WIDTH
    x = x_ref[...]
    x = x + _bdot(ro_ref[...], wo_ref[0:W, :]) + _bdot(so_ref[...], wo_ref[W:2 * W, :])
    u = _rms(x, gm_ref[...]).astype(BF16)
    acc = jnp.zeros_like(x)
    for c in range(D_FF // FF_CHUNK):
        cs = slice(c * FF_CHUNK, (c + 1) * FF_CHUNK)
        hid = jnp.dot(u, up_ref[:, cs], preferred_element_type=F32)
        hid = jnp.square(jnp.maximum(hid, 0.0))
        acc = acc + jnp.dot(hid.astype(BF16), dn_ref[cs, :], preferred_element_type=F32)
    x = x + acc
    e = _rms(_bdot(p_ref[...], pp_ref[...]), pg_ref[...])
    x = x + e * _sigmoid(_bdot(x, gate_ref[...]))
    if final:
        x = _rms(x, fg_ref[...])
    o_ref[...] = x


def _post(layer, final, x2, ro, so, p_all, wo, gm, up, dn, pp, pg, gate, fg):
    T = x2.shape[0]
    n_tiles = T // TM_PROJ
    tile = lambda n: pl.BlockSpec((TM_PROJ, n), lambda i: (i, 0))
    p_tile = pl.BlockSpec((TM_PROJ, PLE_DIM), lambda i: (layer * n_tiles + i, 0))
    row = lambda a: pl.BlockSpec(a.shape, lambda i: (0, 0), pipeline_mode=pl.Buffered(1))
    mat = lambda a: pl.BlockSpec((None,) + a.shape[1:], lambda i: (layer, 0, 0), pipeline_mode=pl.Buffered(1))
    return pl.pallas_call(
        functools.partial(_post_kernel, final),
        grid=(n_tiles,),
        in_specs=[tile(D_MODEL), tile(RWKV_WIDTH), tile(SWA_WIDTH), p_tile,
                  mat(wo), row(gm), mat(up), mat(dn), mat(pp), row(pg), mat(gate), row(fg)],
        out_specs=tile(D_MODEL),
        out_shape=jax.ShapeDtypeStruct((T, D_MODEL), F32),
        compiler_params=pltpu.CompilerParams(
            dimension_semantics=("arbitrary",), vmem_limit_bytes=VMEM_LIMIT),
        name="post_final" if final else "post",
    )(x2, ro, so, p_all, wo, gm, up, dn, pp, pg, gate, fg)


def _bucket_tile():
    max_exact = REL_BUCKETS // 2
    dist = (np.arange(BLOCK)[:, None] + BLOCK) - np.arange(2 * BLOCK)[None, :]
    n = np.maximum(dist, 0)
    nf = np.maximum(n, 1).astype(np.float32)
    scaled = (np.log(nf / np.float32(max_exact)) / np.float32(math.log(REL_MAX_DIST / max_exact))
              * np.float32(REL_BUCKETS - max_exact)).astype(np.float32)
    large = np.minimum(max_exact + scaled.astype(np.int32), REL_BUCKETS - 1)
    bucket = np.where(n < max_exact, n, large)
    valid = (dist >= 0) & (dist < WINDOW)
    return np.where(valid, bucket, -1).astype(np.int32)


def _scan_consts(tt):
    t = np.arange(tt)
    same = (t[:, None] // CHUNK) == (t[None, :] // CHUNK)
    tril = same & (t[:, None] >= t[None, :])
    hh = np.arange(MXU_WIDTH) // HEAD_DIM
    seg = hh[:, None] == hh[None, :]
    return (jnp.asarray(seg, BF16), jnp.asarray(np.concatenate([tril, tril], axis=1), BF16))


def kernel(x, p, norm_mix_g, w_in, mu_shift, w0, w_up, a0, a_up, g_up, vres_down, mu_vres, v0, vres_up,
           k_k, k_a, r_k, lnx_g, lnx_b, sinks, rel_bias, w_out, norm_mlp_g, w_ff_up, w_ff_down,
           ple_proj, ple_norm_g, ple_gate, final_norm_g):
    B, S, _ = x.shape
    depth = w_in.shape[0]
    T = B * S
    n_chunks = SCAN_CHUNKS_PER_STEP
    assert x.shape[2] == D_MODEL and w_in.shape[1:] == (D_MODEL, IN_COLS) and p.shape == (depth, B, S, PLE_DIM)
    assert S % (n_chunks * CHUNK) == 0 and S % TM_PROJ == 0 and TM_PROJ % BLOCK == 0
    row = lambda a: a.reshape(1, -1).astype(F32)

    bkt = jnp.asarray(_bucket_tile())
    consts = _scan_consts(n_chunks * CHUNK)

    w_in_b, w_out_b = w_in.astype(BF16), w_out.astype(BF16)
    up_b, dn_b = w_ff_up.astype(BF16), w_ff_down.astype(BF16)
    pp_b, gate_b = ple_proj.astype(BF16), ple_gate.astype(BF16)
    p_all = p.reshape(depth * T, PLE_DIM)

    x2 = x.reshape(T, D_MODEL)
    v_first = None
    for i in range(depth):
        w_vres = None
        if i > 0:
            w_vres = jnp.pad(vres_down[i - 1], ((0, 0), (0, VRES_PAD - MV_LORA))).astype(BF16)
        zs = _mix(i, x2, row(norm_mix_g[i]), w_in_b, w_vres, bkt, rel_bias.astype(F32), sinks[i].astype(F32),
                  S // TM_PROJ)
        zr = zs[0].reshape(B, S, RWKV_COLS)
        swa_out = zs[1]

        prm = {
            "mu": row(mu_shift[i]), "w0": row(w0[i]), "w_up": w_up[i].astype(BF16), "a0": row(a0[i]),
            "a_up": a_up[i].astype(BF16), "g_up": g_up[i].astype(BF16), "k_k": row(k_k[i]),
            "k_a": row(k_a[i]), "r_k": row(r_k[i]), "lnx_g": row(lnx_g[i]), "lnx_b": row(lnx_b[i]),
        }
        if i == 0:
            rwkv_out, v_first = _rwkv(False, zr, None, None, prm, consts, n_chunks)
        else:
            prm["mu_vres"] = jnp.pad(row(mu_vres[i - 1]), ((0, 0), (0, VRES_PAD - MV_LORA)))
            prm["v0"] = row(v0[i - 1])
            prm["vres_up"] = jnp.pad(vres_up[i - 1], ((0, VRES_PAD - MV_LORA), (0, 0))).astype(BF16)
            zv = zs[2].reshape(B, S, VRES_PAD)
            (rwkv_out,) = _rwkv(True, zr, zv, v_first, prm, consts, n_chunks)

        x2 = _post(i, i == depth - 1, x2, rwkv_out.reshape(T, RWKV_WIDTH), swa_out,
                   p_all, w_out_b, row(norm_mlp_g[i]), up_b, dn_b, pp_b, row(ple_norm_g[i]), gate_b,
                   row(final_norm_g))
    return x2.reshape(B, S, D_MODEL)
```
